```python
import math
import jax, jax.numpy as jnp
from jax import lax
import numpy as np

D_MODEL = 2048
BATCH = 32
SEQ = 256
DEPTH = 2
DEC_BATCH = 8
DEC_SEQ = 2048
PAST_LEN = 512

GRID_W = 64
CONV_K = 4
CONV_PAD = (2, 1)
CHUNK = 64
EPS = 1e-6
N_EVEN = (DEPTH + 1) // 2
N_ODD = DEPTH // 2
D_FF = 4 * D_MODEL
D_A = D_MODEL // 2
LRU_BLOCKS = 16
LRU_BW = D_A // LRU_BLOCKS
LRU_C = 8.0
H_B = 8
DK_B = 128
DV_B = 128
D_C = D_MODEL // 2
S5_GW = 16
S5_G = D_C // S5_GW
S5_N = 64
H_D = 4
DK_D = 128
DV_D = 256
GLA_RANK = 16
GLA_TAU = 16.0
E_IN = 2 * D_A + 2 * H_B * DK_B + 2 * H_B * DV_B + 4 * H_B
E_OUT = D_A + H_B * DV_B
O_IN = D_C + 2 * H_D * DK_D + 2 * H_D * DV_D + 2 * GLA_RANK
O_OUT = D_C + H_D * DV_D

kernel_name = 'bidir_hybrid_diffusion_step'


def rms_norm(x, g):
    xf = x.astype(jnp.float32)
    y = xf * lax.rsqrt(jnp.mean(xf * xf, axis=-1, keepdims=True) + EPS)
    return (y * g.astype(jnp.float32)).astype(x.dtype)


def l2_norm(x):
    xf = x.astype(jnp.float32)
    return xf * lax.rsqrt(jnp.sum(xf * xf, axis=-1, keepdims=True) + EPS)


def split_cols(z, sizes):
    return jnp.split(z, [int(s) for s in np.cumsum(sizes)[:-1]], axis=-1)


def short_conv(x, w, latent):
    b, t, ch = x.shape
    xs = x.reshape(b * (t // GRID_W), GRID_W, ch) if latent else x
    y = lax.conv_general_dilated(xs, w[:, None, :].astype(x.dtype), window_strides=(1,), padding=[CONV_PAD],
                                 dimension_numbers=('NWC', 'WIO', 'NWC'), feature_group_count=ch)
    return y.reshape(b, t, ch)


def _lin_comb(l, r):
    a_l, b_l = l
    a_r, b_r = r
    return a_r * a_l, a_r * b_l + b_r


def linear_scan(a, b, h0, reverse):
    idx = -1 if reverse else 0
    b = b.at[:, idx].add(a[:, idx] * h0.astype(a.dtype))
    _, h = lax.associative_scan(_lin_comb, (a, b), reverse=reverse, axis=1)
    h_fin = h[:, 0] if reverse else h[:, -1]
    return h, h_fin


def block_diag(x, w, bias):
    xb = x.reshape(x.shape[:-1] + (LRU_BLOCKS, LRU_BW))
    return jnp.einsum('btnc,ncd->btnd', xb, w).reshape(x.shape) + bias


def rg_lru(u, w_a, b_a, w_x, b_x, lam, h0, reverse):
    r = jax.nn.sigmoid(block_diag(u, w_a, b_a))
    i = jax.nn.sigmoid(block_diag(u, w_x, b_x))
    log_a = -LRU_C * r * jax.nn.softplus(-lam.astype(jnp.float32))
    a = jnp.exp(log_a)
    b = jnp.sqrt(-jnp.expm1(2.0 * log_a)) * (i * u)
    return linear_scan(a, b, h0, reverse)


def to_chunks(x):
    b, t, h = x.shape[:3]
    x = x.reshape((b, t // CHUNK, CHUNK, h) + x.shape[3:])
    return jnp.moveaxis(x, (1, 3), (0, 2))


def from_chunks(x):
    x = jnp.moveaxis(x, (0, 2), (1, 3))
    b, n, c, h = x.shape[:4]
    return x.reshape((b, n * c, h) + x.shape[4:])


def gated_delta_chunked(q, k, v, beta, log_a, s0):
    causal = jnp.tril(jnp.ones((CHUNK, CHUNK), dtype=bool))
    strict = jnp.tril(jnp.ones((CHUNK, CHUNK), dtype=bool), k=-1)
    q, k, v, beta, log_a = (to_chunks(a) for a in (q, k, v, beta, log_a))
    g = jnp.cumsum(log_a, axis=-1)
    diff = g[..., :, None] - g[..., None, :]
    dec_incl = jnp.exp(jnp.where(causal, diff, -jnp.inf))
    dec_strict = jnp.where(strict, dec_incl, 0.0)
    k_beta = k * beta[..., None]
    lower = jnp.einsum('nbhik,nbhjk->nbhij', k_beta, k) * dec_strict
    i_plus_l = lower + jnp.eye(CHUNK, dtype=lower.dtype)
    u = lax.linalg.triangular_solve(i_plus_l, v * beta[..., None], left_side=True, lower=True, unit_diagonal=True)
    w = lax.linalg.triangular_solve(i_plus_l, k_beta * jnp.exp(g)[..., None], left_side=True, lower=True,
                                    unit_diagonal=True)
    a_qk = jnp.einsum('nbhik,nbhjk->nbhij', q, k) * dec_incl
    q_g = q * jnp.exp(g)[..., None]
    k_g = k * jnp.exp(g[..., -1:] - g)[..., None]
    g_tot = jnp.exp(g[..., -1])

    def step(s, inp):
        q_c, k_c, u_c, w_c, a_c, gt_c = inp
        v_new = u_c - jnp.einsum('bhck,bhkv->bhcv', w_c, s)
        o = jnp.einsum('bhck,bhkv->bhcv', q_c, s) + jnp.einsum('bhij,bhjv->bhiv', a_c, v_new)
        s = s * gt_c[..., None, None] + jnp.einsum('bhck,bhcv->bhkv', k_c, v_new)
        return s, o

    s_fin, o = lax.scan(step, s0.astype(jnp.float32), (q_g, k_g, u, w, a_qk, g_tot))
    return from_chunks(o), s_fin


def gla_chunked(q, k, v, log_a, s0):
    causal = jnp.tril(jnp.ones((CHUNK, CHUNK), dtype=bool))
    q, k, v, log_a = (to_chunks(a) for a in (q, k, v, log_a))
    g = jnp.cumsum(log_a, axis=-2)
    g_last = g[..., -1:, :]
    q_g = q * jnp.exp(g)
    k_g = k * jnp.exp(g_last - g)

    def step(s, inp):
        q_c, k_c, v_c, g_c, qg_c, kg_c, gl_c = inp
        diff = g_c[..., :, None, :] - g_c[..., None, :, :]
        dec = jnp.exp(jnp.where(causal[:, :, None], diff, -jnp.inf))
        a_qk = jnp.einsum('bhik,bhjk,bhijk->bhij', q_c, k_c, dec)
        o = jnp.einsum('bhck,bhkv->bhcv', qg_c, s) + jnp.einsum('bhij,bhjv->bhiv', a_qk, v_c)
        s = s * jnp.exp(gl_c)[..., 0, :, None] + jnp.einsum('bhck,bhcv->bhkv', kg_c, v_c)
        return s, o

    s_fin, o = lax.scan(step, s0.astype(jnp.float32), (q, k, v, g, q_g, k_g, g_last))
    return from_chunks(o), s_fin


def s5_scan(u_g, lam_re, lam_im, log_step, b_re, b_im, c_re, c_im, s0_re, s0_im, reverse):
    f32 = jnp.float32
    lam = lax.complex(lam_re.astype(f32), lam_im.astype(f32))
    lam_bar = jnp.exp(lam * jnp.exp(log_step.astype(f32))[:, None])
    b_bar = ((lam_bar - 1.0) / lam)[:, :, None] * lax.complex(b_re.astype(f32), b_im.astype(f32))
    bu = jnp.einsum('btgw,gnw->btgn', u_g.astype(jnp.complex64), b_bar)
    a = jnp.broadcast_to(lam_bar, bu.shape)
    s, s_fin = linear_scan(a, bu, lax.complex(s0_re.astype(f32), s0_im.astype(f32)), reverse)
    y = jnp.einsum('btgn,gwn->btgw', s, lax.complex(c_re.astype(f32), c_im.astype(f32))).real
    return y, s_fin


def even_mixer(h, w_in, w_out, conv_a, lru_wa, lru_ba, lru_wx, lru_bx, lru_lambda, conv_qkv,
               dn_a_log, dn_dt_bias, dn_norm_g, h0, s0, latent):
    f32 = jnp.float32
    b, t, _ = h.shape
    z = jnp.dot(h, w_in)
    xa, ga, qkv, go, bl, al = split_cols(z, [D_A, D_A, 2 * H_B * DK_B + H_B * DV_B, H_B * DV_B, 2 * H_B, 2 * H_B])
    u = short_conv(xa, conv_a, latent).astype(f32)
    h_f, hf_fin = rg_lru(u, lru_wa[0], lru_ba[0], lru_wx[0], lru_bx[0], lru_lambda[0], h0[:, 0], False)
    h_b, hb_fin = rg_lru(u, lru_wa[1], lru_ba[1], lru_wx[1], lru_bx[1], lru_lambda[1], h0[:, 1], True)
    y_a = (h_f + h_b) * jax.nn.gelu(ga.astype(f32))
    qkv = jax.nn.silu(short_conv(qkv, conv_qkv, latent).astype(f32))
    q, k, v = split_cols(qkv, [H_B * DK_B, H_B * DK_B, H_B * DV_B])
    q = l2_norm(q.reshape(b, t, H_B, DK_B)) * (DK_B ** -0.5)
    k = l2_norm(k.reshape(b, t, H_B, DK_B))
    v = v.reshape(b, t, H_B, DV_B)
    beta = jax.nn.sigmoid(bl.astype(f32).reshape(b, t, 2, H_B))
    log_a = -jnp.exp(dn_a_log.astype(f32)) * jax.nn.softplus(al.astype(f32).reshape(b, t, 2, H_B)
                                                            + dn_dt_bias.astype(f32))
    o_f, s_f = gated_delta_chunked(q, k, v, beta[:, :, 0], log_a[:, :, 0], s0[:, 0])
    o_b, s_b = gated_delta_chunked(jnp.flip(q, 1), jnp.flip(k, 1), jnp.flip(v, 1), jnp.flip(beta[:, :, 1], 1),
                                   jnp.flip(log_a[:, :, 1], 1), s0[:, 1])
    o = o_f + jnp.flip(o_b, 1)
    y_b = (rms_norm(o, dn_norm_g) * jax.nn.silu(go.astype(f32)).reshape(b, t, H_B, DV_B)).reshape(b, t, H_B * DV_B)
    y = jnp.dot(jnp.concatenate([y_a, y_b], axis=-1).astype(h.dtype), w_out)
    return y, jnp.stack([hf_fin, hb_fin], axis=1), jnp.stack([s_f, s_b], axis=1)


def odd_mixer(h, w_in, w_out, lam_re, lam_im, log_step, b_re, b_im, c_re, c_im, d_skip, w_glu, b_glu,
              gla_wg2, gla_bg, gla_norm_g, s0_re, s0_im, gla_s0):
    f32 = jnp.float32
    b, t, _ = h.shape
    z = jnp.dot(h, w_in)
    u, q, k, v, r, glr = split_cols(z, [D_C, H_D * DK_D, H_D * DK_D, H_D * DV_D, H_D * DV_D, 2 * GLA_RANK])
    uf = u.astype(f32)
    ug = uf.reshape(b, t, S5_G, S5_GW)
    y_f, sf = s5_scan(ug, lam_re[0], lam_im[0], log_step[0], b_re[0], b_im[0], c_re[0], c_im[0],
                      s0_re[:, 0], s0_im[:, 0], False)
    y_bk, sb = s5_scan(ug, lam_re[1], lam_im[1], log_step[1], b_re[1], b_im[1], c_re[1], c_im[1],
                       s0_re[:, 1], s0_im[:, 1], True)
    yc = (y_f + y_bk).reshape(b, t, D_C) + d_skip.astype(f32) * uf
    zc = jax.nn.gelu(yc)
    y_c = zc * jax.nn.sigmoid(jnp.dot(zc, w_glu.astype(f32)) + b_glu.astype(f32))
    q = q.astype(f32).reshape(b, t, H_D, DK_D) * (DK_D ** -0.5)
    k = k.astype(f32).reshape(b, t, H_D, DK_D)
    v = v.astype(f32).reshape(b, t, H_D, DV_D)
    glr = glr.astype(f32).reshape(b, t, 2, GLA_RANK)
    log_a = jax.nn.log_sigmoid(jnp.einsum('btdr,drk->btdk', glr, gla_wg2.astype(f32)) + gla_bg.astype(f32)) / GLA_TAU
    log_a = log_a.reshape(b, t, 2, H_D, DK_D)
    o_f, g_f = gla_chunked(q, k, v, log_a[:, :, 0], gla_s0[:, 0])
    o_b, g_b = gla_chunked(jnp.flip(q, 1), jnp.flip(k, 1), jnp.flip(v, 1), jnp.flip(log_a[:, :, 1], 1), gla_s0[:, 1])
    o = o_f + jnp.flip(o_b, 1)
    y_d = (rms_norm(o, gla_norm_g) * jax.nn.silu(r.astype(f32)).reshape(b, t, H_D, DV_D)).reshape(b, t, H_D * DV_D)
    y = jnp.dot(jnp.concatenate([y_c, y_d], axis=-1).astype(h.dtype), w_out)
    s_fin = jnp.stack([sf, sb], axis=1)
    return y, s_fin.real, s_fin.imag, jnp.stack([g_f, g_b], axis=1)


def setup_inputs(seed: int = 0) -> dict:
    key = jax.random.key(seed)
    ks = iter(jax.random.split(key, 64))
    f32 = jnp.float32

    def nrm(shape, scale):
        return scale * jax.random.normal(next(ks), shape, f32)

    def unif(shape, lo, hi):
        return jax.random.uniform(next(ks), shape, f32, lo, hi)

    a8 = unif((N_EVEN, 2, D_A), 0.9, 0.999)
    sig = a8 ** (1.0 / LRU_C)
    lru_lambda = jnp.log(sig) - jnp.log1p(-sig)
    dt = jnp.exp(unif((N_EVEN, 2, H_B), math.log(1e-3), math.log(1e-1)))
    dn_dt_bias = dt + jnp.log(-jnp.expm1(-dt))
    lam_im_base = math.pi * jnp.arange(S5_N, dtype=f32)
    return {
        'x_prompt': nrm((BATCH, SEQ, D_MODEL), 1.0),
        'x_sample': nrm((DEC_BATCH, DEC_SEQ, D_MODEL), 1.0),
        'state_lru': nrm((DEC_BATCH, N_EVEN, 2, D_A), 1.0),
        'state_delta': nrm((DEC_BATCH, N_EVEN, 2, H_B, DK_B, DV_B), 0.3),
        'state_s5_re': nrm((DEC_BATCH, N_ODD, 2, S5_G, S5_N), 1.0),
        'state_s5_im': nrm((DEC_BATCH, N_ODD, 2, S5_G, S5_N), 1.0),
        'state_gla': nrm((DEC_BATCH, N_ODD, 2, H_D, DK_D, DV_D), 0.3),
        'c': nrm((DEC_BATCH, D_MODEL), 1.0),
        'c_ctx': nrm((D_MODEL,), 1.0),
        'w_ada': nrm((DEPTH, D_MODEL, 6 * D_MODEL), D_MODEL ** -0.5),
        'b_ada': nrm((DEPTH, 6 * D_MODEL), 0.01),
        'norm_g': 1.0 + nrm((DEPTH, 4, D_MODEL), 0.02),
        'w_up': nrm((DEPTH, D_MODEL, D_FF), D_MODEL ** -0.5),
        'w_down': nrm((DEPTH, D_FF, D_MODEL), D_FF ** -0.5),
        'w_in_e': nrm((N_EVEN, D_MODEL, E_IN), D_MODEL ** -0.5),
        'w_out_e': nrm((N_EVEN, E_OUT, D_MODEL), E_OUT ** -0.5),
        'conv_a': nrm((N_EVEN, CONV_K, D_A), CONV_K ** -0.5),
        'lru_wa': nrm((N_EVEN, 2, LRU_BLOCKS, LRU_BW, LRU_BW), LRU_BW ** -0.5),
        'lru_ba': nrm((N_EVEN, 2, D_A), 0.01),
        'lru_wx': nrm((N_EVEN, 2, LRU_BLOCKS, LRU_BW, LRU_BW), LRU_BW ** -0.5),
        'lru_bx': nrm((N_EVEN, 2, D_A), 0.01),
        'lru_lambda': lru_lambda,
        'conv_qkv': nrm((N_EVEN, CONV_K, 2 * H_B * DK_B + H_B * DV_B), CONV_K ** -0.5),
        'dn_a_log': jnp.log(unif((N_EVEN, 2, H_B), 1.0, 16.0)),
        'dn_dt_bias': dn_dt_bias,
        'dn_norm_g': 1.0 + nrm((N_EVEN, DV_B), 0.02),
        'w_in_o': nrm((N_ODD, D_MODEL, O_IN), D_MODEL ** -0.5),
        'w_out_o': nrm((N_ODD, O_OUT, D_MODEL), O_OUT ** -0.5),
        's5_lam_re': -0.5 + nrm((N_ODD, 2, S5_G, S5_N), 0.01),
        's5_lam_im': lam_im_base + nrm((N_ODD, 2, S5_G, S5_N), 0.01),
        's5_log_step': unif((N_ODD, 2, S5_G), math.log(1e-3), math.log(1e-1)),
        's5_b_re': nrm((N_ODD, 2, S5_G, S5_N, S5_GW), (2 * S5_GW) ** -0.5),
        's5_b_im': nrm((N_ODD, 2, S5_G, S5_N, S5_GW), (2 * S5_GW) ** -0.5),
        's5_c_re': nrm((N_ODD, 2, S5_G, S5_GW, S5_N), S5_N ** -0.5),
        's5_c_im': nrm((N_ODD, 2, S5_G, S5_GW, S5_N), S5_N ** -0.5),
        's5_d': nrm((N_ODD, D_C), 0.5),
        's5_w_glu': nrm((N_ODD, D_C, D_C), D_C ** -0.5),
        's5_b_glu': nrm((N_ODD, D_C), 0.01),
        'gla_wg2': nrm((N_ODD, 2, GLA_RANK, H_D * DK_D), GLA_RANK ** -0.5),
        'gla_bg': nrm((N_ODD, 2, H_D * DK_D), 0.01),
        'gla_norm_g': 1.0 + nrm((N_ODD, DV_D), 0.02),
    }


def reference(x_prompt, x_sample, state_lru, state_delta, state_s5_re, state_s5_im, state_gla, c, c_ctx,
              w_ada, b_ada, norm_g, w_up, w_down, w_in_e, w_out_e, conv_a, lru_wa, lru_ba, lru_wx, lru_bx,
              lru_lambda, conv_qkv, dn_a_log, dn_dt_bias, dn_norm_g, w_in_o, w_out_o, s5_lam_re, s5_lam_im,
              s5_log_step, s5_b_re, s5_b_im, s5_c_re, s5_c_im, s5_d, s5_w_glu, s5_b_glu, gla_wg2, gla_bg,
              gla_norm_g):
    f32 = jnp.float32

    def run(x, cvec, st_lru, st_delta, st_re, st_im, st_gla, latent):
        fin_lru, fin_delta, fin_re, fin_im, fin_gla = [], [], [], [], []
        for l in range(DEPTH):
            mod = jnp.dot(jax.nn.silu(cvec), w_ada[l]) + b_ada[l]
            sh1, sc1, gt1, sh2, sc2, gt2 = jnp.split(mod[:, None, :], 6, axis=-1)
            h = rms_norm(x, norm_g[l, 0]) * (1.0 + sc1) + sh1
            if l % 2 == 0:
                e = l // 2
                y, f_lru, f_delta = even_mixer(h, w_in_e[e], w_out_e[e], conv_a[e], lru_wa[e], lru_ba[e],
                                               lru_wx[e], lru_bx[e], lru_lambda[e], conv_qkv[e], dn_a_log[e],
                                               dn_dt_bias[e], dn_norm_g[e], st_lru[:, e], st_delta[:, e], latent)
                fin_lru.append(f_lru)
                fin_delta.append(f_delta)
            else:
                o = l // 2
                y, f_re, f_im, f_gla = odd_mixer(h, w_in_o[o], w_out_o[o], s5_lam_re[o], s5_lam_im[o],
                                                 s5_log_step[o], s5_b_re[o], s5_b_im[o], s5_c_re[o], s5_c_im[o],
                                                 s5_d[o], s5_w_glu[o], s5_b_glu[o], gla_wg2[o], gla_bg[o],
                                                 gla_norm_g[o], st_re[:, o], st_im[:, o], st_gla[:, o])
                fin_re.append(f_re)
                fin_im.append(f_im)
                fin_gla.append(f_gla)
            x = x + gt1 * rms_norm(y, norm_g[l, 1])
            h = rms_norm(x, norm_g[l, 2]) * (1.0 + sc2) + sh2
            y = jnp.dot(jnp.square(jax.nn.relu(jnp.dot(h, w_up[l]))), w_down[l])
            x = x + gt2 * rms_norm(y, norm_g[l, 3])
        return x, fin_lru, fin_delta, fin_re, fin_im, fin_gla

    bp = x_prompt.shape[0]
    y_prompt, f_lru, f_delta, f_re, f_im, f_gla = run(
        x_prompt, c_ctx[None, :],
        jnp.zeros((bp, N_EVEN, 2, D_A), f32), jnp.zeros((bp, N_EVEN, 2, H_B, DK_B, DV_B), f32),
        jnp.zeros((bp, N_ODD, 2, S5_G, S5_N), f32), jnp.zeros((bp, N_ODD, 2, S5_G, S5_N), f32),
        jnp.zeros((bp, N_ODD, 2, H_D, DK_D, DV_D), f32), False)
    y_sample = run(x_sample, c, state_lru, state_delta, state_s5_re, state_s5_im, state_gla, True)[0]
    new_state_lru = jnp.stack(f_lru, axis=1)
    new_state_delta = jnp.stack(f_delta, axis=1)
    new_state_s5_re = jnp.stack(f_re, axis=1)
    new_state_s5_im = jnp.stack(f_im, axis=1)
    new_state_gla = jnp.stack(f_gla, axis=1)
    return (y_prompt, y_sample, new_state_lru, new_state_delta, new_state_s5_re, new_state_s5_im, new_state_gla)
```

```python
import functools
import math

import jax
import jax.numpy as jnp
from jax import lax
from jax.experimental import pallas as pl
from jax.experimental.pallas import tpu as pltpu

f32 = jnp.float32
bf16 = jnp.bfloat16

LANES = 128
SUBLANES = 8
VMEM_LIMIT_BYTES = 56 * 1024 * 1024

D_MODEL = 2048
D_FF = 4 * D_MODEL
GRID_W = 64
CHUNK = 64
EPS = 1e-6
D_A = D_MODEL // 2
LRU_BLOCKS = 16
LRU_BW = D_A // LRU_BLOCKS
LRU_C = 8.0
H_B = 8
DK_B = 128
DV_B = 128
D_C = D_MODEL // 2
S5_GW = 16
S5_G = D_C // S5_GW
S5_N = 64
H_D = 4
DK_D = 128
DV_D = 256
GLA_RANK = 16
GLA_TAU = 16.0

TB = 256
RB = TB * SUBLANES
CROWS = CHUNK * SUBLANES
SUB = 16

E_XA, E_GA, E_Q, E_K, E_V, E_GO, E_BA = 0, 8, 16, 24, 32, 40, 48
E_SLABS = 49
O_U, O_Q, O_K, O_V, O_R, O_GLR = 0, 8, 12, 16, 24, 32
O_SLABS = 33


def _cparams(sem):
    return pltpu.CompilerParams(dimension_semantics=sem, vmem_limit_bytes=VMEM_LIMIT_BYTES)


def _mm(a, b):
    return jnp.dot(a.astype(bf16), b.astype(bf16), preferred_element_type=f32)


def _mm_nt(a, b):
    return lax.dot_general(a.astype(bf16), b.astype(bf16), (((1,), (1,)), ((), ())), preferred_element_type=f32)


def _mm_tn(a, b):
    return lax.dot_general(a.astype(bf16), b.astype(bf16), (((0,), (0,)), ((), ())), preferred_element_type=f32)


def _mm_hi(a, b):
    return jnp.dot(a, b, precision=lax.Precision.HIGHEST, preferred_element_type=f32)


def _silu(x):
    return x * jax.nn.sigmoid(x)


def _ada_kernel(c_ref, w_ref, b_ref, o_ref):
    a = _silu(c_ref[...])
    o_ref[0] = _mm(a, w_ref[0]) + b_ref[0]


def _ada(cvec, w_ada, b_ada):
    depth, d, n = w_ada.shape
    tn = 1024
    return pl.pallas_call(
        _ada_kernel,
        grid=(depth, n // tn),
        in_specs=[
            pl.BlockSpec(cvec.shape, lambda l, j: (0, 0)),
            pl.BlockSpec((1, d, tn), lambda l, j: (l, 0, j)),
            pl.BlockSpec((1, 1, tn), lambda l, j: (l, 0, j)),
        ],
        out_specs=pl.BlockSpec((1, cvec.shape[0], tn), lambda l, j: (l, 0, j)),
        out_shape=jax.ShapeDtypeStruct((depth, cvec.shape[0], n), f32),
        compiler_params=_cparams(("parallel", "parallel")),
        name="ada",
    )(cvec, w_ada, b_ada.reshape(depth, 1, n))


def _nmm_kernel(x_ref, g_ref, sh_ref, sc_ref, w_ref, o_ref, h_ref, *, relu2, slab):
    @pl.when(pl.program_id(1) == 0)
    def _():
        x = x_ref[...]
        tm, d = x.shape
        y = x * lax.rsqrt(jnp.mean(x * x, axis=-1, keepdims=True) + EPS) * g_ref[...]
        y3 = y.reshape(tm // SUBLANES, SUBLANES, d)
        h = y3 * (1.0 + sc_ref[0][None]) + sh_ref[0][None]
        h_ref[...] = h.reshape(tm, d).astype(bf16)

    acc = jnp.dot(h_ref[...], w_ref[...], preferred_element_type=f32)
    if relu2:
        acc = jnp.square(jnp.maximum(acc, 0.0))
    if slab:
        for s in range(acc.shape[1] // LANES):
            o_ref[s] = acc[:, s * LANES:(s + 1) * LANES].astype(o_ref.dtype)
    else:
        o_ref[...] = acc.astype(o_ref.dtype)


def _nmm(x, g, sh, sc, w, *, rows_ctx, tm, tn, relu2=False, slab=False, out_dtype=f32):
    rows, d = x.shape
    n = w.shape[1]
    grp = lambda i, j: ((i * tm >= rows_ctx).astype(jnp.int32), 0, 0)
    if slab:
        out_shape = jax.ShapeDtypeStruct((n // LANES, rows, LANES), out_dtype)
        out_spec = pl.BlockSpec((tn // LANES, tm, LANES), lambda i, j: (j, i, 0))
    else:
        out_shape = jax.ShapeDtypeStruct((rows, n), out_dtype)
        out_spec = pl.BlockSpec((tm, tn), lambda i, j: (i, j))
    return pl.pallas_call(
        functools.partial(_nmm_kernel, relu2=relu2, slab=slab),
        grid=(rows // tm, n // tn),
        in_specs=[
            pl.BlockSpec((tm, d), lambda i, j: (i, 0)),
            pl.BlockSpec((1, d), lambda i, j: (0, 0)),
            pl.BlockSpec((1, SUBLANES, d), grp),
            pl.BlockSpec((1, SUBLANES, d), grp),
            pl.BlockSpec((d, tn), lambda i, j: (0, j)),
        ],
        out_specs=out_spec,
        out_shape=out_shape,
        scratch_shapes=[pltpu.VMEM((tm, d), bf16)],
        compiler_params=_cparams(("parallel", "arbitrary")),
        name="nmm",
    )(x, g.reshape(1, d), sh, sc, w)


def _mmres_kernel(a_ref, w_ref, x_ref, g_ref, gt_ref, o_ref, acc_ref):
    k = pl.program_id(1)

    @pl.when(k == 0)
    def _():
        acc_ref[...] = jnp.zeros_like(acc_ref)

    acc_ref[...] += jnp.dot(a_ref[...], w_ref[...], preferred_element_type=f32)

    @pl.when(k == pl.num_programs(1) - 1)
    def _():
        y = acc_ref[...]
        tm, d = y.shape
        yn = y * lax.rsqrt(jnp.mean(y * y, axis=-1, keepdims=True) + EPS) * g_ref[...]
        o = x_ref[...].reshape(tm // SUBLANES, SUBLANES, d) + gt_ref[0][None] * yn.reshape(tm // SUBLANES, SUBLANES, d)
        o_ref[...] = o.reshape(tm, d)


def _mmres(a, w, x, g, gt, *, rows_ctx, tm, tk):
    rows, kdim = a.shape
    d = w.shape[1]
    grp = lambda i, k: ((i * tm >= rows_ctx).astype(jnp.int32), 0, 0)
    return pl.pallas_call(
        _mmres_kernel,
        grid=(rows // tm, kdim // tk),
        in_specs=[
            pl.BlockSpec((tm, tk), lambda i, k: (i, k)),
            pl.BlockSpec((tk, d), lambda i, k: (k, 0)),
            pl.BlockSpec((tm, d), lambda i, k: (i, 0)),
            pl.BlockSpec((1, d), lambda i, k: (0, 0)),
            pl.BlockSpec((1, SUBLANES, d), grp),
        ],
        out_specs=pl.BlockSpec((tm, d), lambda i, k: (i, 0)),
        out_shape=jax.ShapeDtypeStruct((rows, d), f32),
        scratch_shapes=[pltpu.VMEM((tm, d), f32)],
        compiler_params=_cparams(("parallel", "arbitrary")),
        name="mmres",
    )(a, w, x, g.reshape(1, d), gt)


def _step_blk(s, gc, nd, rev):
    if not rev:
        return s
    return jnp.where(s < gc, s, 2 * gc + nd - 1 - s)


def _conv4(x, w4, is_ctx):
    rows = x.shape[0]
    t = lax.broadcasted_iota(jnp.int32, x.shape, 0) // SUBLANES
    tl = jnp.where(is_ctx, t, t % GRID_W)
    last = jnp.where(is_ctx, TB - 1, GRID_W - 1)
    y = x * w4[2:3, :]
    xm2 = pltpu.roll(x, 2 * SUBLANES, 0)
    y = y + jnp.where(tl >= 2, xm2, 0.0) * w4[0:1, :]
    xm1 = pltpu.roll(x, SUBLANES, 0)
    y = y + jnp.where(tl >= 1, xm1, 0.0) * w4[1:2, :]
    xp1 = pltpu.roll(x, rows - SUBLANES, 0)
    y = y + jnp.where(tl < last, xp1, 0.0) * w4[3:4, :]
    return y


def _tile_cumsum(src_ref, dst_ref, rev):
    ntile = src_ref.shape[0] // SUBLANES

    def body(i, run):
        t = (ntile - 1 - i) if rev else i
        r = pl.multiple_of(t * SUBLANES, SUBLANES)
        run = jnp.where(i % CHUNK == 0, 0.0, run) + src_ref[pl.ds(r, SUBLANES), :]
        dst_ref[pl.ds(r, SUBLANES), :] = run
        return run

    lax.fori_loop(0, ntile, body, jnp.zeros((SUBLANES, src_ref.shape[1]), f32))


def _pick_lane(x, lane):
    onehot = lax.broadcasted_iota(jnp.int32, x.shape, 1) == lane
    col = jnp.sum(jnp.where(onehot, x, 0.0), axis=-1, keepdims=True)
    return jnp.broadcast_to(col, x.shape)


def _lru_gate_kernel(xa_ref, cw_ref, w_ref, bias_ref, sp_ref, a_ref, b_ref, *, gc):
    is_ctx = pl.program_id(0) < gc
    u = _conv4(xa_ref[0], cw_ref[0], is_ctx)
    for d in range(2):
        r = jax.nn.sigmoid(_mm(u, w_ref[d, 0, 0]) + bias_ref[d, 0, 0])
        i = jax.nn.sigmoid(_mm(u, w_ref[d, 1, 0]) + bias_ref[d, 1, 0])
        log_a = -LRU_C * r * sp_ref[d, 0]
        a = jnp.exp(log_a)
        b = jnp.sqrt(-jnp.tanh(log_a) * (a * a + 1.0)) * (i * u)
        a_ref[d] = a
        b_ref[d] = b


def _lru_gates(z, conv_a, wbd, bias, sp, *, gc, nblk):
    rows = z.shape[1]
    ns = D_A // LANES
    out = jax.ShapeDtypeStruct((2, rows, D_A), f32)
    return pl.pallas_call(
        functools.partial(_lru_gate_kernel, gc=gc),
        grid=(nblk, ns),
        in_specs=[
            pl.BlockSpec((1, RB, LANES), lambda i, s: (E_XA + s, i, 0)),
            pl.BlockSpec((1, 4, LANES), lambda i, s: (s, 0, 0)),
            pl.BlockSpec((2, 2, 1, LANES, LANES), lambda i, s: (0, 0, s, 0, 0)),
            pl.BlockSpec((2, 2, 1, 1, LANES), lambda i, s: (0, 0, s, 0, 0)),
            pl.BlockSpec((2, 1, 1, LANES), lambda i, s: (0, s, 0, 0)),
        ],
        out_specs=[pl.BlockSpec((2, RB, LANES), lambda i, s: (0, i, s))] * 2,
        out_shape=[out, out],
        compiler_params=_cparams(("parallel", "parallel")),
        name="lru_gates",
    )(z, conv_a, wbd, bias, sp)


def _lru_scan_kernel(a_ref, b_ref, h0_ref, h_ref, fin_ref, carry_ref, *, gc, rev):
    s = pl.program_id(1)

    @pl.when(s < gc)
    def _():
        carry_ref[...] = jnp.zeros_like(carry_ref)

    @pl.when(s == gc)
    def _():
        carry_ref[...] = h0_ref[...]

    def body(i, h):
        t = (TB - 1 - i) if rev else i
        r = pl.multiple_of(t * SUBLANES, SUBLANES)
        h = a_ref[0, pl.ds(r, SUBLANES), :] * h + b_ref[0, pl.ds(r, SUBLANES), :]
        h_ref[pl.ds(r, SUBLANES), :] = h
        return h

    h = lax.fori_loop(0, TB, body, carry_ref[...], unroll=8)
    carry_ref[...] = h

    @pl.when(s < gc)
    def _():
        fin_ref[0] = h


def _lru_scan(a, b, h0, *, d, gc, nd, rev):
    rows = a.shape[1]
    nblk = gc + nd
    wc = 512
    blk = lambda c, s: _step_blk(s, gc, nd, rev)
    return pl.pallas_call(
        functools.partial(_lru_scan_kernel, gc=gc, rev=rev),
        grid=(D_A // wc, nblk),
        in_specs=[
            pl.BlockSpec((1, RB, wc), lambda c, s: (d, blk(c, s), c)),
            pl.BlockSpec((1, RB, wc), lambda c, s: (d, blk(c, s), c)),
            pl.BlockSpec((SUBLANES, wc), lambda c, s: (0, c)),
        ],
        out_specs=[
            pl.BlockSpec((RB, wc), lambda c, s: (blk(c, s), c)),
            pl.BlockSpec((1, SUBLANES, wc), lambda c, s: (jnp.minimum(s, gc - 1), 0, c)),
        ],
        out_shape=[
            jax.ShapeDtypeStruct((rows, D_A), f32),
            jax.ShapeDtypeStruct((gc, SUBLANES, D_A), f32),
        ],
        scratch_shapes=[pltpu.VMEM((SUBLANES, wc), f32)],
        compiler_params=_cparams(("parallel", "arbitrary")),
        name="lru_scan_bwd" if rev else "lru_scan_fwd",
    )(a, b, h0)


def _lru_out_kernel(hf_ref, hb_ref, ga_ref, o_ref):
    o_ref[...] = ((hf_ref[...] + hb_ref[...]) * jax.nn.gelu(ga_ref[0])).astype(o_ref.dtype)


def _lru_out(hf, hb, z, *, nblk):
    rows = hf.shape[0]
    return pl.pallas_call(
        _lru_out_kernel,
        grid=(nblk, D_A // LANES),
        in_specs=[
            pl.BlockSpec((RB, LANES), lambda i, s: (i, s)),
            pl.BlockSpec((RB, LANES), lambda i, s: (i, s)),
            pl.BlockSpec((1, RB, LANES), lambda i, s: (E_GA + s, i, 0)),
        ],
        out_specs=pl.BlockSpec((RB, LANES), lambda i, s: (i, s)),
        out_shape=jax.ShapeDtypeStruct((rows, D_A), bf16),
        compiler_params=_cparams(("parallel", "parallel")),
        name="lru_out",
    )(hf, hb, z)


def _delta_kernel(q_ref, k_ref, v_ref, ba_ref, cw_ref, nega_ref, dtb_ref, s0_ref, o_ref, fin_ref,
                  qs_ref, ks_ref, vs_ref, bs_ref, la_ref, gs_ref, st_ref, *, gc, rev, d):
    h = pl.program_id(0)
    s = pl.program_id(1)
    is_ctx = s < gc

    @pl.when(is_ctx)
    def _():
        st_ref[...] = jnp.zeros_like(st_ref)

    @pl.when(s == gc)
    def _():
        st_ref[...] = s0_ref[0]

    def l2n(x):
        return x * lax.rsqrt(jnp.sum(x * x, axis=-1, keepdims=True) + EPS)

    qs_ref[...] = l2n(_silu(_conv4(q_ref[0], cw_ref[0, 0], is_ctx))) * (DK_B ** -0.5)
    ks_ref[...] = l2n(_silu(_conv4(k_ref[0], cw_ref[1, 0], is_ctx)))
    vs_ref[...] = _silu(_conv4(v_ref[0], cw_ref[2, 0], is_ctx))
    ba = ba_ref[0]
    bs_ref[...] = _pick_lane(jax.nn.sigmoid(ba), d * H_B + h)
    la_ref[...] = _pick_lane(nega_ref[...] * jax.nn.softplus(ba + dtb_ref[...]), 2 * H_B + d * H_B + h)
    _tile_cumsum(la_ref, gs_ref, rev)

    ii = lax.broadcasted_iota(jnp.int32, (CHUNK, CHUNK), 0)
    jj = lax.broadcasted_iota(jnp.int32, (CHUNK, CHUNK), 1)
    incl = (jj >= ii) if rev else (jj <= ii)
    strict = (jj > ii) if rev else (jj < ii)
    eye = (ii == jj).astype(f32)
    last = 0 if rev else CHUNK - 1

    def chunk_body(ci, carry):
        c = (TB // CHUNK - 1 - ci) if rev else ci
        for b in range(SUBLANES):
            rows = pl.ds(c * CROWS + b, CHUNK, stride=SUBLANES)
            q = qs_ref[rows, :]
            k = ks_ref[rows, :]
            v = vs_ref[rows, :]
            beta = bs_ref[rows, :]
            g = gs_ref[rows, :]
            gsq = g[:, :CHUNK]
            g_row = jnp.sum(gsq * eye, axis=0, keepdims=True)
            diff = gsq - g_row
            dec_incl = jnp.where(incl, jnp.exp(jnp.where(incl, diff, 0.0)), 0.0)
            dec_strict = jnp.where(strict, dec_incl, 0.0)
            kb = k * beta
            low = _mm_nt(kb, k) * dec_strict
            t_inv = eye - low
            m = _mm_hi(low, low)
            for it in range(5):
                t_inv = t_inv + _mm_hi(t_inv, m)
                if it < 4:
                    m = _mm_hi(m, m)
            eg = jnp.exp(g)
            uw = _mm_hi(t_inv, jnp.concatenate([v * beta, kb * eg], axis=1))
            u = uw[:, :DV_B]
            w = uw[:, DV_B:]
            a_qk = _mm_nt(q, k) * dec_incl
            g_last = g[last:last + 1, :]
            q_g = q * eg
            k_g = k * jnp.exp(g_last - g)
            st = st_ref[b]
            v_new = u - _mm(w, st)
            o = _mm(q_g, st) + _mm(a_qk, v_new)
            st_ref[b] = st * jnp.exp(g_last) + _mm_tn(k_g, v_new)
            o_ref[0, rows, :] = o
        return carry

    lax.fori_loop(0, TB // CHUNK, chunk_body, 0)

    @pl.when(is_ctx)
    def _():
        fin_ref[0, 0] = st_ref[...]


def _delta(z, conv_qkv, nega, dtb, s0, *, d, gc, nd, rev):
    rows = z.shape[1]
    nblk = gc + nd
    blk = lambda h, s: _step_blk(s, gc, nd, rev)
    slab = lambda off: pl.BlockSpec((1, RB, LANES), lambda h, s: (off + h, blk(h, s), 0))
    scr = pltpu.VMEM((RB, LANES), f32)
    return pl.pallas_call(
        functools.partial(_delta_kernel, gc=gc, rev=rev, d=d),
        grid=(H_B, nblk),
        in_specs=[
            slab(E_Q), slab(E_K), slab(E_V),
            pl.BlockSpec((1, RB, LANES), lambda h, s: (E_BA, blk(h, s), 0)),
            pl.BlockSpec((3, 1, 4, LANES), lambda h, s: (0, h, 0, 0)),
            pl.BlockSpec((1, LANES), lambda h, s: (0, 0)),
            pl.BlockSpec((1, LANES), lambda h, s: (0, 0)),
            pl.BlockSpec((1, SUBLANES, DK_B, DV_B), lambda h, s: (h, 0, 0, 0)),
        ],
        out_specs=[
            pl.BlockSpec((1, RB, LANES), lambda h, s: (h, blk(h, s), 0)),
            pl.BlockSpec((1, 1, SUBLANES, DK_B, DV_B), lambda h, s: (jnp.minimum(s, gc - 1), h, 0, 0, 0)),
        ],
        out_shape=[
            jax.ShapeDtypeStruct((H_B, rows, LANES), f32),
            jax.ShapeDtypeStruct((gc, H_B, SUBLANES, DK_B, DV_B), f32),
        ],
        scratch_shapes=[scr, scr, scr, scr, scr, scr, pltpu.VMEM((SUBLANES, DK_B, DV_B), f32)],
        compiler_params=_cparams(("parallel", "arbitrary")),
        name="delta_bwd" if rev else "delta_fwd",
    )(z, z, z, z, conv_qkv, nega, dtb, s0)


def _delta_out_kernel(of_ref, ob_ref, go_ref, g_ref, o_ref):
    o = of_ref[0] + ob_ref[0]
    y = o * lax.rsqrt(jnp.mean(o * o, axis=-1, keepdims=True) + EPS) * g_ref[...]
    o_ref[...] = (y * _silu(go_ref[0])).astype(o_ref.dtype)


def _delta_out(of, ob, z, g, *, nblk):
    rows = of.shape[1]
    return pl.pallas_call(
        _delta_out_kernel,
        grid=(nblk, H_B),
        in_specs=[
            pl.BlockSpec((1, RB, LANES), lambda i, h: (h, i, 0)),
            pl.BlockSpec((1, RB, LANES), lambda i, h: (h, i, 0)),
            pl.BlockSpec((1, RB, LANES), lambda i, h: (E_GO + h, i, 0)),
            pl.BlockSpec((1, LANES), lambda i, h: (0, 0)),
        ],
        out_specs=pl.BlockSpec((RB, LANES), lambda i, h: (i, h)),
        out_shape=jax.ShapeDtypeStruct((rows, H_B * DV_B), bf16),
        compiler_params=_cparams(("parallel", "parallel")),
        name="delta_out",
    )(of, ob, z, g.reshape(1, LANES))


def _s5_kernel(u_ref, bt_ref, ct_ref, lam_ref, s0_ref, y_ref, fin_ref, sbuf_ref, carry_ref, *, gc, rev):
    s = pl.program_id(1)
    half = sbuf_ref.shape[1] // 2

    @pl.when(s < gc)
    def _():
        carry_ref[...] = jnp.zeros_like(carry_ref)

    @pl.when(s == gc)
    def _():
        carry_ref[...] = s0_ref[0]

    sbuf_ref[...] = _mm(u_ref[0], bt_ref[0])
    lr = jnp.broadcast_to(lam_ref[0, 0:1, :], (SUBLANES, half))
    li = jnp.broadcast_to(lam_ref[0, 1:2, :], (SUBLANES, half))

    def body(i, hc):
        hr, hi = hc
        t = (TB - 1 - i) if rev else i
        r = pl.multiple_of(t * SUBLANES, SUBLANES)
        nr = lr * hr - li * hi + sbuf_ref[pl.ds(r, SUBLANES), :half]
        ni = lr * hi + li * hr + sbuf_ref[pl.ds(r, SUBLANES), half:]
        sbuf_ref[pl.ds(r, SUBLANES), :half] = nr
        sbuf_ref[pl.ds(r, SUBLANES), half:] = ni
        return nr, ni

    hr, hi = lax.fori_loop(0, TB, body, (carry_ref[:, :half], carry_ref[:, half:]), unroll=4)
    carry_ref[:, :half] = hr
    carry_ref[:, half:] = hi
    y_ref[0] = _mm(sbuf_ref[...], ct_ref[0])

    @pl.when(s < gc)
    def _():
        fin_ref[0, 0] = carry_ref[...]


def _s5(z, bt, ct, lam, s0, *, gc, nd, rev):
    rows = z.shape[1]
    nblk = gc + nd
    ngt = D_C // LANES
    sw = 2 * (LANES // S5_GW) * S5_N
    blk = lambda g, s: _step_blk(s, gc, nd, rev)
    return pl.pallas_call(
        functools.partial(_s5_kernel, gc=gc, rev=rev),
        grid=(ngt, nblk),
        in_specs=[
            pl.BlockSpec((1, RB, LANES), lambda g, s: (O_U + g, blk(g, s), 0)),
            pl.BlockSpec((1, LANES, sw), lambda g, s: (g, 0, 0)),
            pl.BlockSpec((1, sw, LANES), lambda g, s: (g, 0, 0)),
            pl.BlockSpec((1, 2, sw // 2), lambda g, s: (g, 0, 0)),
            pl.BlockSpec((1, SUBLANES, sw), lambda g, s: (g, 0, 0)),
        ],
        out_specs=[
            pl.BlockSpec((1, RB, LANES), lambda g, s: (g, blk(g, s), 0)),
            pl.BlockSpec((1, 1, SUBLANES, sw), lambda g, s: (jnp.minimum(s, gc - 1), g, 0, 0)),
        ],
        out_shape=[
            jax.ShapeDtypeStruct((ngt, rows, LANES), f32),
            jax.ShapeDtypeStruct((gc, ngt, SUBLANES, sw), f32),
        ],
        scratch_shapes=[pltpu.VMEM((RB, sw), f32), pltpu.VMEM((SUBLANES, sw), f32)],
        compiler_params=_cparams(("parallel", "arbitrary")),
        name="s5_bwd" if rev else "s5_fwd",
    )(z, bt, ct, lam, s0)


def _s5_out_kernel(yf_ref, yb_ref, u_ref, d_ref, w_ref, b_ref, o_ref):
    ns = yf_ref.shape[0]
    yc = jnp.concatenate([yf_ref[s] + yb_ref[s] + d_ref[s] * u_ref[s] for s in range(ns)], axis=1)
    zc = jax.nn.gelu(yc)
    gate = jax.nn.sigmoid(jnp.dot(zc.astype(bf16), w_ref[...], preferred_element_type=f32) + b_ref[...])
    o_ref[...] = (zc * gate).astype(o_ref.dtype)


def _s5_out(yf, yb, z, dskip, w_glu, b_glu, *, tm):
    ns, rows, _ = yf.shape
    return pl.pallas_call(
        _s5_out_kernel,
        grid=(rows // tm,),
        in_specs=[
            pl.BlockSpec((ns, tm, LANES), lambda i: (0, i, 0)),
            pl.BlockSpec((ns, tm, LANES), lambda i: (0, i, 0)),
            pl.BlockSpec((ns, tm, LANES), lambda i: (O_U // ns, i, 0)),
            pl.BlockSpec((ns, 1, LANES), lambda i: (0, 0, 0)),
            pl.BlockSpec((D_C, D_C), lambda i: (0, 0)),
            pl.BlockSpec((1, D_C), lambda i: (0, 0)),
        ],
        out_specs=pl.BlockSpec((tm, D_C), lambda i: (i, 0)),
        out_shape=jax.ShapeDtypeStruct((rows, D_C), bf16),
        compiler_params=_cparams(("parallel",)),
        name="s5_out",
    )(yf, yb, z, dskip, w_glu, b_glu)


def _gla_kernel(q_ref, k_ref, v_ref, glr_ref, wg_ref, bg_ref, s0_ref, o_ref, fin_ref,
                la_ref, gs_ref, st_ref, *, gc, rev):
    s = pl.program_id(1)
    is_ctx = s < gc

    @pl.when(is_ctx)
    def _():
        st_ref[...] = jnp.zeros_like(st_ref)

    @pl.when(s == gc)
    def _():
        st_ref[...] = s0_ref[0]

    la_ref[...] = jax.nn.log_sigmoid(_mm(glr_ref[0], wg_ref[0]) + bg_ref[0]) * (1.0 / GLA_TAU)
    _tile_cumsum(la_ref, gs_ref, rev)

    ii = lax.broadcasted_iota(jnp.int32, (SUB, CHUNK), 0)
    jj = lax.broadcasted_iota(jnp.int32, (SUB, CHUNK), 1)
    last = 0 if rev else CHUNK - 1

    def chunk_body(ci, carry):
        c = (TB // CHUNK - 1 - ci) if rev else ci
        for b in range(SUBLANES):
            rows = pl.ds(c * CROWS + b, CHUNK, stride=SUBLANES)
            q = q_ref[0, rows, :] * (DK_D ** -0.5)
            k = k_ref[0, rows, :]
            v = jnp.concatenate([v_ref[0, rows, :], v_ref[1, rows, :]], axis=1)
            g = gs_ref[rows, :]
            a_rows = []
            for blk in range(CHUNK // SUB):
                lo = blk * SUB
                first = lo + SUB - 1 if rev else lo
                c_ref = g[first:first + 1, :]
                qt = q[lo:lo + SUB, :] * jnp.exp(g[lo:lo + SUB, :] - c_ref)
                kt = k * jnp.exp(c_ref - g)
                a_blk = _mm_nt(qt, kt)
                causal = (jj >= ii + lo) if rev else (jj <= ii + lo)
                a_rows.append(jnp.where(causal, a_blk, 0.0))
            a_qk = jnp.concatenate(a_rows, axis=0)
            g_last = g[last:last + 1, :]
            q_g = q * jnp.exp(g)
            k_g = k * jnp.exp(g_last - g)
            st = st_ref[b]
            o = _mm_nt(q_g, st) + _mm(a_qk, v)
            st_ref[b] = st * jnp.exp(g_last) + _mm_tn(v, k_g)
            o_ref[0, rows, :] = o[:, :LANES]
            o_ref[1, rows, :] = o[:, LANES:]
        return carry

    lax.fori_loop(0, TB // CHUNK, chunk_body, 0)

    @pl.when(is_ctx)
    def _():
        fin_ref[0, 0] = st_ref[...]


def _gla(z, wg, bg, s0, *, gc, nd, rev):
    rows = z.shape[1]
    nblk = gc + nd
    nv = DV_D // LANES
    blk = lambda h, s: _step_blk(s, gc, nd, rev)
    scr = pltpu.VMEM((RB, LANES), f32)
    return pl.pallas_call(
        functools.partial(_gla_kernel, gc=gc, rev=rev),
        grid=(H_D, nblk),
        in_specs=[
            pl.BlockSpec((1, RB, LANES), lambda h, s: (O_Q + h, blk(h, s), 0)),
            pl.BlockSpec((1, RB, LANES), lambda h, s: (O_K + h, blk(h, s), 0)),
            pl.BlockSpec((nv, RB, LANES), lambda h, s: (O_V // nv + h, blk(h, s), 0)),
            pl.BlockSpec((1, RB, LANES), lambda h, s: (O_GLR, blk(h, s), 0)),
            pl.BlockSpec((1, LANES, LANES), lambda h, s: (h, 0, 0)),
            pl.BlockSpec((1, 1, LANES), lambda h, s: (h, 0, 0)),
            pl.BlockSpec((1, SUBLANES, DV_D, DK_D), lambda h, s: (h, 0, 0, 0)),
        ],
        out_specs=[
            pl.BlockSpec((nv, RB, LANES), lambda h, s: (h, blk(h, s), 0)),
            pl.BlockSpec((1, 1, SUBLANES, DV_D, DK_D), lambda h, s: (jnp.minimum(s, gc - 1), h, 0, 0, 0)),
        ],
        out_shape=[
            jax.ShapeDtypeStruct((H_D * nv, rows, LANES), f32),
            jax.ShapeDtypeStruct((gc, H_D, SUBLANES, DV_D, DK_D), f32),
        ],
        scratch_shapes=[scr, scr, pltpu.VMEM((SUBLANES, DV_D, DK_D), f32)],
        compiler_params=_cparams(("parallel", "arbitrary")),
        name="gla_bwd" if rev else "gla_fwd",
    )(z, z, z, z, wg, bg, s0)


def _gla_out_kernel(of_ref, ob_ref, r_ref, g_ref, o_ref):
    nv = of_ref.shape[0]
    o = [of_ref[j] + ob_ref[j] for j in range(nv)]
    ms = sum(jnp.sum(x * x, axis=-1, keepdims=True) for x in o) * (1.0 / DV_D)
    inv = lax.rsqrt(ms + EPS)
    for j in range(nv):
        y = o[j] * inv * g_ref[j] * _silu(r_ref[j])
        o_ref[:, j * LANES:(j + 1) * LANES] = y.astype(o_ref.dtype)


def _gla_out(of, ob, z, g, *, nblk):
    rows = of.shape[1]
    nv = DV_D // LANES
    return pl.pallas_call(
        _gla_out_kernel,
        grid=(nblk, H_D),
        in_specs=[
            pl.BlockSpec((nv, RB, LANES), lambda i, h: (h, i, 0)),
            pl.BlockSpec((nv, RB, LANES), lambda i, h: (h, i, 0)),
            pl.BlockSpec((nv, RB, LANES), lambda i, h: (O_R // nv + h, i, 0)),
            pl.BlockSpec((nv, 1, LANES), lambda i, h: (0, 0, 0)),
        ],
        out_specs=pl.BlockSpec((RB, DV_D), lambda i, h: (i, h)),
        out_shape=jax.ShapeDtypeStruct((rows, H_D * DV_D), bf16),
        compiler_params=_cparams(("parallel", "parallel")),
        name="gla_out",
    )(of, ob, z, g.reshape(nv, 1, LANES))


def _to_rows(x):
    b, t, d = x.shape
    return x.reshape(b // SUBLANES, SUBLANES, t, d).transpose(0, 2, 1, 3).reshape(b * t, d)


def _from_rows(r, b, t):
    d = r.shape[-1]
    return r.reshape(b // SUBLANES, t, SUBLANES, d).transpose(0, 2, 1, 3).reshape(b, t, d)


def _pad_cols(w, n):
    return jnp.pad(w, ((0, 0), (0, n - w.shape[1])))


def _unit_states(fin, b):
    fin = jnp.moveaxis(fin, b, 1)
    return fin.reshape((fin.shape[0] * SUBLANES,) + fin.shape[2:])


def kernel(x_prompt, x_sample, state_lru, state_delta, state_s5_re, state_s5_im, state_gla, c, c_ctx, w_ada, b_ada, norm_g, w_up, w_down, w_in_e, w_out_e, conv_a, lru_wa, lru_ba, lru_wx, lru_bx, lru_lambda, conv_qkv, dn_a_log, dn_dt_bias, dn_norm_g, w_in_o, w_out_o, s5_lam_re, s5_lam_im, s5_log_step, s5_b_re, s5_b_im, s5_c_re, s5_c_im, s5_d, s5_w_glu, s5_b_glu, gla_wg2, gla_bg, gla_norm_g):
    bp, tp, d = x_prompt.shape
    bs, ts, _ = x_sample.shape
    assert tp == TB and bp % SUBLANES == 0 and bs == SUBLANES and ts % TB == 0 and d == D_MODEL
    depth = w_ada.shape[0]
    gc = bp // SUBLANES
    nd = ts // TB
    nblk = gc + nd
    rows_ctx = bp * tp
    seq = dict(gc=gc, nd=nd)

    x = jnp.concatenate([_to_rows(x_prompt), _to_rows(x_sample)], axis=0)

    cvec = jnp.concatenate([c, jnp.broadcast_to(c_ctx[None], (SUBLANES, d))], axis=0)
    mod = _ada(cvec, w_ada, b_ada)
    mod = jnp.stack([mod[:, SUBLANES:], mod[:, :SUBLANES]], axis=1)
    mods = [[mod[l, :, :, i * d:(i + 1) * d] for i in range(6)] for l in range(depth)]

    fin_lru, fin_delta, fin_re, fin_im, fin_gla = [], [], [], [], []
    for l in range(depth):
        sh1, sc1, gt1, sh2, sc2, gt2 = mods[l]
        if l % 2 == 0:
            e = l // 2
            w_in = _pad_cols(w_in_e[e], E_SLABS * LANES).astype(bf16)
            z = _nmm(x, norm_g[l, 0], sh1, sc1, w_in, rows_ctx=rows_ctx, tm=512, tn=7 * LANES, slab=True)
            ns = D_A // LANES
            wg = jnp.stack([lru_wa[e], lru_wx[e]], axis=1).reshape(2, 2, ns, 2, LRU_BW, LRU_BW)
            wbd = jnp.zeros((2, 2, ns, LANES, LANES), f32)
            wbd = wbd.at[:, :, :, :LRU_BW, :LRU_BW].set(wg[:, :, :, 0]).at[:, :, :, LRU_BW:, LRU_BW:].set(wg[:, :, :, 1])
            bias = jnp.stack([lru_ba[e], lru_bx[e]], axis=1).reshape(2, 2, ns, 1, LANES)
            sp = jax.nn.softplus(-lru_lambda[e]).reshape(2, ns, 1, LANES)
            cwa = conv_a[e].reshape(4, ns, LANES).transpose(1, 0, 2)
            a, b = _lru_gates(z, cwa, wbd.astype(bf16), bias, sp, gc=gc, nblk=nblk)
            h_f, f_f = _lru_scan(a, b, state_lru[:, e, 0], d=0, rev=False, **seq)
            h_b, f_b = _lru_scan(a, b, state_lru[:, e, 1], d=1, rev=True, **seq)
            y_a = _lru_out(h_f, h_b, z, nblk=nblk)
            fin_lru.append(jnp.stack([_unit_states(f_f, 1), _unit_states(f_b, 1)], axis=1))
            cq = conv_qkv[e].reshape(4, 3, H_B, LANES).transpose(1, 2, 0, 3)
            lane_pad = lambda v: jnp.pad(v.reshape(1, 2 * H_B), ((0, 0), (2 * H_B, LANES - 4 * H_B)))
            nega = lane_pad(-jnp.exp(dn_a_log[e]))
            dtb = lane_pad(dn_dt_bias[e])
            s0 = state_delta[:, e].transpose(1, 2, 0, 3, 4)
            o_f, s_f = _delta(z, cq, nega, dtb, s0[0], d=0, rev=False, **seq)
            o_b, s_b = _delta(z, cq, nega, dtb, s0[1], d=1, rev=True, **seq)
            y_b = _delta_out(o_f, o_b, z, dn_norm_g[e], nblk=nblk)
            fin_delta.append(jnp.stack([_unit_states(s_f, 2), _unit_states(s_b, 2)], axis=1))
            y_mix = jnp.concatenate([y_a, y_b], axis=1)
            w_out = w_out_e[e].astype(bf16)
        else:
            o = l // 2
            w_in = _pad_cols(w_in_o[o], O_SLABS * LANES).astype(bf16)
            z = _nmm(x, norm_g[l, 0], sh1, sc1, w_in, rows_ctx=rows_ctx, tm=512, tn=11 * LANES, slab=True)
            ngt = D_C // LANES
            gps = LANES // S5_GW
            lam = lax.complex(s5_lam_re[o], s5_lam_im[o])
            lam_bar = jnp.exp(lam * jnp.exp(s5_log_step[o])[..., None])
            b_bar = ((lam_bar - 1.0) / lam)[..., None] * lax.complex(s5_b_re[o], s5_b_im[o])
            eye_g = jnp.eye(gps, dtype=f32)

            def bmat(p):
                p = p.reshape(2, ngt, gps, S5_N, S5_GW)
                return jnp.einsum('dtgnw,gh->dtgwhn', p, eye_g).reshape(2, ngt, LANES, gps * S5_N)

            def cmat(p):
                p = p.reshape(2, ngt, gps, S5_GW, S5_N)
                return jnp.einsum('dtgwn,gh->dtgnhw', p, eye_g).reshape(2, ngt, gps * S5_N, LANES)

            bt = jnp.concatenate([bmat(b_bar.real), bmat(b_bar.imag)], axis=-1).astype(bf16)
            ct = jnp.concatenate([cmat(s5_c_re[o]), -cmat(s5_c_im[o])], axis=-2).astype(bf16)
            lamv = jnp.stack([lam_bar.real, lam_bar.imag], axis=2).reshape(2, ngt, gps, 2, S5_N)
            lamv = lamv.transpose(0, 1, 3, 2, 4).reshape(2, ngt, 2, gps * S5_N)

            def s5_state(sre, sim):
                f = lambda p: p.reshape(SUBLANES, ngt, gps * S5_N).transpose(1, 0, 2)
                return jnp.concatenate([f(sre), f(sim)], axis=-1)

            y_f, c_f = _s5(z, bt[0], ct[0], lamv[0], s5_state(state_s5_re[:, o, 0], state_s5_im[:, o, 0]), rev=False, **seq)
            y_bk, c_b = _s5(z, bt[1], ct[1], lamv[1], s5_state(state_s5_re[:, o, 1], state_s5_im[:, o, 1]), rev=True, **seq)
            y_c = _s5_out(y_f, y_bk, z, s5_d[o].reshape(ngt, 1, LANES), s5_w_glu[o].astype(bf16),
                          s5_b_glu[o].reshape(1, D_C), tm=512)
            half = gps * S5_N

            def s5_fin(cf):
                cf = cf.transpose(0, 2, 1, 3).reshape(gc * SUBLANES, ngt, 2, gps, S5_N)
                return (cf[:, :, 0].reshape(gc * SUBLANES, S5_G, S5_N), cf[:, :, 1].reshape(gc * SUBLANES, S5_G, S5_N))

            (rf, imf), (rb, imb) = s5_fin(c_f), s5_fin(c_b)
            fin_re.append(jnp.stack([rf, rb], axis=1))
            fin_im.append(jnp.stack([imf, imb], axis=1))
            wgp = jnp.zeros((2, H_D, LANES, LANES), f32)
            wg2 = gla_wg2[o].reshape(2, GLA_RANK, H_D, DK_D).transpose(0, 2, 1, 3)
            wgp = wgp.at[0, :, :GLA_RANK].set(wg2[0]).at[1, :, GLA_RANK:2 * GLA_RANK].set(wg2[1]).astype(bf16)
            bgp = gla_bg[o].reshape(2, H_D, 1, DK_D)
            g0 = state_gla[:, o].transpose(1, 2, 0, 4, 3)
            o_f, g_f = _gla(z, wgp[0], bgp[0], g0[0], rev=False, **seq)
            o_b, g_b = _gla(z, wgp[1], bgp[1], g0[1], rev=True, **seq)
            y_d = _gla_out(o_f, o_b, z, gla_norm_g[o], nblk=nblk)
            gfin = lambda gf: jnp.swapaxes(_unit_states(gf, 2), -1, -2)
            fin_gla.append(jnp.stack([gfin(g_f), gfin(g_b)], axis=1))
            y_mix = jnp.concatenate([y_c, y_d], axis=1)
            w_out = w_out_o[o].astype(bf16)

        x = _mmres(y_mix, w_out, x, norm_g[l, 1], gt1, rows_ctx=rows_ctx, tm=512, tk=w_out.shape[0])
        hmid = _nmm(x, norm_g[l, 2], sh2, sc2, w_up[l].astype(bf16), rows_ctx=rows_ctx, tm=512, tn=1024,
                    relu2=True, out_dtype=bf16)
        x = _mmres(hmid, w_down[l].astype(bf16), x, norm_g[l, 3], gt2, rows_ctx=rows_ctx, tm=512, tk=2048)

    y_prompt = _from_rows(x[:rows_ctx], bp, tp)
    y_sample = _from_rows(x[rows_ctx:], bs, ts)
    return (y_prompt, y_sample, jnp.stack(fin_lru, axis=1), jnp.stack(fin_delta, axis=1),
            jnp.stack(fin_re, axis=1), jnp.stack(fin_im, axis=1), jnp.stack(fin_gla, axis=1))
```

```python
import functools
import math

import jax
import jax.numpy as jnp
from jax import lax
from jax.experimental import pallas as pl
from jax.experimental.pallas import tpu as pltpu

f32 = jnp.float32
bf16 = jnp.bfloat16

LANES = 128
SUBLANES = 8
VMEM_LIMIT_BYTES = 56 * 1024 * 1024

D_MODEL = 2048
D_FF = 4 * D_MODEL
GRID_W = 64
CHUNK = 64
EPS = 1e-6
D_A = D_MODEL // 2
LRU_BLOCKS = 16
LRU_BW = D_A // LRU_BLOCKS
LRU_C = 8.0
H_B = 8
DK_B = 128
DV_B = 128
D_C = D_MODEL // 2
S5_GW = 16
S5_G = D_C // S5_GW
S5_N = 64
H_D = 4
DK_D = 128
DV_D = 256
GLA_RANK = 16
GLA_TAU = 16.0

TB = 256
RB = TB * SUBLANES
CROWS = CHUNK * SUBLANES
SUB = 16

E_XA, E_GA, E_Q, E_K, E_V, E_GO, E_BA = 0, 8, 16, 24, 32, 40, 48
E_SLABS = 50
O_U, O_Q, O_K, O_V, O_R, O_GLR = 0, 8, 12, 16, 24, 32
O_SLABS = 33


def _cparams(sem):
    return pltpu.CompilerParams(dimension_semantics=sem, vmem_limit_bytes=VMEM_LIMIT_BYTES)


def _mm(a, b):
    return jnp.dot(a.astype(bf16), b.astype(bf16), preferred_element_type=f32)


def _mm_nt(a, b):
    return lax.dot_general(a.astype(bf16), b.astype(bf16), (((1,), (1,)), ((), ())), preferred_element_type=f32)


def _mm_tn(a, b):
    return lax.dot_general(a.astype(bf16), b.astype(bf16), (((0,), (0,)), ((), ())), preferred_element_type=f32)


def _split(x):
    hi = x.astype(bf16)
    return hi, (x - hi.astype(f32)).astype(bf16)


def _mm3(a, b):
    dot = functools.partial(jnp.dot, preferred_element_type=f32)
    return dot(a[0], b[0]) + (dot(a[0], b[1]) + dot(a[1], b[0]))


def _sigmoid(x):
    return 0.5 * jnp.tanh(0.5 * x) + 0.5


def _silu(x):
    t = 0.5 * x
    return t + t * jnp.tanh(t)


def _ada_kernel(c_ref, w_ref, b_ref, o_ref):
    a = _silu(c_ref[...])
    o_ref[0] = _mm(a, w_ref[0]) + b_ref[0]


def _ada(cvec, w_ada, b_ada):
    depth, d, n = w_ada.shape
    tn = 1024
    return pl.pallas_call(
        _ada_kernel,
        grid=(depth, n // tn),
        in_specs=[
            pl.BlockSpec(cvec.shape, lambda l, j: (0, 0)),
            pl.BlockSpec((1, d, tn), lambda l, j: (l, 0, j)),
            pl.BlockSpec((1, 1, tn), lambda l, j: (l, 0, j)),
        ],
        out_specs=pl.BlockSpec((1, cvec.shape[0], tn), lambda l, j: (l, 0, j)),
        out_shape=jax.ShapeDtypeStruct((depth, cvec.shape[0], n), f32),
        compiler_params=_cparams(("parallel", "parallel")),
        name="ada",
    )(cvec, w_ada, b_ada.reshape(depth, 1, n))


def _nmm_kernel(x_ref, g_ref, sh_ref, sc_ref, w_ref, o_ref, h_ref, *, relu2, slab):
    @pl.when(pl.program_id(1) == 0)
    def _():
        x = x_ref[...]
        tm, d = x.shape
        y = x * lax.rsqrt(jnp.mean(x * x, axis=-1, keepdims=True) + EPS)
        gain = g_ref[...] * (1.0 + sc_ref[0])
        h = y.reshape(tm // SUBLANES, SUBLANES, d) * gain[None] + sh_ref[0][None]
        h_ref[...] = h.reshape(tm, d).astype(bf16)

    acc = jnp.dot(h_ref[...], w_ref[...], preferred_element_type=f32)
    if relu2:
        acc = jnp.square(jnp.maximum(acc, 0.0))
    if slab:
        for s in range(acc.shape[1] // LANES):
            o_ref[s] = acc[:, s * LANES:(s + 1) * LANES].astype(o_ref.dtype)
    else:
        o_ref[...] = acc.astype(o_ref.dtype)


def _nmm(x, g, sh, sc, w, *, rows_ctx, tm, tn, relu2=False, slab=False, out_dtype=f32):
    rows, d = x.shape
    n = w.shape[1]
    grp = lambda i, j: ((i * tm >= rows_ctx).astype(jnp.int32), 0, 0)
    if slab:
        out_shape = jax.ShapeDtypeStruct((n // LANES, rows, LANES), out_dtype)
        out_spec = pl.BlockSpec((tn // LANES, tm, LANES), lambda i, j: (j, i, 0))
    else:
        out_shape = jax.ShapeDtypeStruct((rows, n), out_dtype)
        out_spec = pl.BlockSpec((tm, tn), lambda i, j: (i, j))
    return pl.pallas_call(
        functools.partial(_nmm_kernel, relu2=relu2, slab=slab),
        grid=(rows // tm, n // tn),
        in_specs=[
            pl.BlockSpec((tm, d), lambda i, j: (i, 0)),
            pl.BlockSpec((1, d), lambda i, j: (0, 0)),
            pl.BlockSpec((1, SUBLANES, d), grp),
            pl.BlockSpec((1, SUBLANES, d), grp),
            pl.BlockSpec((d, tn), lambda i, j: (0, j)),
        ],
        out_specs=out_spec,
        out_shape=out_shape,
        scratch_shapes=[pltpu.VMEM((tm, d), bf16)],
        compiler_params=_cparams(("parallel", "arbitrary")),
        name="nmm",
    )(x, g.reshape(1, d), sh, sc, w)


def _mmres_kernel(a_ref, w_ref, x_ref, g_ref, gt_ref, o_ref, acc_ref):
    k = pl.program_id(1)

    @pl.when(k == 0)
    def _():
        acc_ref[...] = jnp.zeros_like(acc_ref)

    acc_ref[...] += jnp.dot(a_ref[...], w_ref[...], preferred_element_type=f32)

    @pl.when(k == pl.num_programs(1) - 1)
    def _():
        y = acc_ref[...]
        tm, d = y.shape
        yn = y * lax.rsqrt(jnp.mean(y * y, axis=-1, keepdims=True) + EPS) * g_ref[...]
        o = x_ref[...].reshape(tm // SUBLANES, SUBLANES, d) + gt_ref[0][None] * yn.reshape(tm // SUBLANES, SUBLANES, d)
        o_ref[...] = o.reshape(tm, d)


def _mmres(a, w, x, g, gt, *, rows_ctx, tm, tk):
    rows, kdim = a.shape
    d = w.shape[1]
    grp = lambda i, k: ((i * tm >= rows_ctx).astype(jnp.int32), 0, 0)
    return pl.pallas_call(
        _mmres_kernel,
        grid=(rows // tm, kdim // tk),
        in_specs=[
            pl.BlockSpec((tm, tk), lambda i, k: (i, k)),
            pl.BlockSpec((tk, d), lambda i, k: (k, 0)),
            pl.BlockSpec((tm, d), lambda i, k: (i, 0)),
            pl.BlockSpec((1, d), lambda i, k: (0, 0)),
            pl.BlockSpec((1, SUBLANES, d), grp),
        ],
        out_specs=pl.BlockSpec((tm, d), lambda i, k: (i, 0)),
        out_shape=jax.ShapeDtypeStruct((rows, d), f32),
        scratch_shapes=[pltpu.VMEM((tm, d), f32)],
        compiler_params=_cparams(("parallel", "arbitrary")),
        name="mmres",
    )(a, w, x, g.reshape(1, d), gt)


def _step_blk(s, gc, nd, rev):
    if not rev:
        return s
    return jnp.where(s < gc, s, 2 * gc + nd - 1 - s)


def _conv4(x, w4, is_ctx):
    rows = x.shape[0]
    t = lax.broadcasted_iota(jnp.int32, x.shape, 0) // SUBLANES
    tl = jnp.where(is_ctx, t, t % GRID_W)
    last = jnp.where(is_ctx, TB - 1, GRID_W - 1)
    y = x * w4[2:3, :]
    xm2 = pltpu.roll(x, 2 * SUBLANES, 0)
    y = y + jnp.where(tl >= 2, xm2, 0.0) * w4[0:1, :]
    xm1 = pltpu.roll(x, SUBLANES, 0)
    y = y + jnp.where(tl >= 1, xm1, 0.0) * w4[1:2, :]
    xp1 = pltpu.roll(x, rows - SUBLANES, 0)
    y = y + jnp.where(tl < last, xp1, 0.0) * w4[3:4, :]
    return y


def _tile_cumsum(src_ref, dst_ref, rev):
    ntile = src_ref.shape[0] // SUBLANES

    def body(i, run):
        t = (ntile - 1 - i) if rev else i
        r = pl.multiple_of(t * SUBLANES, SUBLANES)
        run = jnp.where(i % CHUNK == 0, 0.0, run) + src_ref[pl.ds(r, SUBLANES), :]
        dst_ref[pl.ds(r, SUBLANES), :] = run
        return run

    lax.fori_loop(0, ntile, body, jnp.zeros((SUBLANES, src_ref.shape[1]), f32), unroll=8)


def _pick_lane(x, lane):
    onehot = lax.broadcasted_iota(jnp.int32, x.shape, 1) == lane
    col = jnp.sum(jnp.where(onehot, x, 0.0), axis=-1, keepdims=True)
    return jnp.broadcast_to(col, x.shape)


def _lru_gate_kernel(xa_ref, cw_ref, w_ref, bias_ref, sp_ref, a_ref, b_ref, *, gc):
    is_ctx = pl.program_id(0) < gc
    u = _conv4(xa_ref[0], cw_ref[0], is_ctx)
    for d in range(2):
        r = _sigmoid(_mm(u, w_ref[d, 0, 0]) + bias_ref[d, 0, 0])
        i = _sigmoid(_mm(u, w_ref[d, 1, 0]) + bias_ref[d, 1, 0])
        log_a = -LRU_C * r * sp_ref[d, 0]
        a = jnp.exp(log_a)
        b = jnp.sqrt(-jnp.tanh(log_a) * (a * a + 1.0)) * (i * u)
        a_ref[d] = a
        b_ref[d] = b


def _lru_gates(z, conv_a, wbd, bias, sp, *, gc, nblk):
    rows = z.shape[1]
    ns = D_A // LANES
    out = jax.ShapeDtypeStruct((2, rows, D_A), f32)
    return pl.pallas_call(
        functools.partial(_lru_gate_kernel, gc=gc),
        grid=(nblk, ns),
        in_specs=[
            pl.BlockSpec((1, RB, LANES), lambda i, s: (E_XA + s, i, 0)),
            pl.BlockSpec((1, 4, LANES), lambda i, s: (s, 0, 0)),
            pl.BlockSpec((2, 2, 1, LANES, LANES), lambda i, s: (0, 0, s, 0, 0)),
            pl.BlockSpec((2, 2, 1, 1, LANES), lambda i, s: (0, 0, s, 0, 0)),
            pl.BlockSpec((2, 1, 1, LANES), lambda i, s: (0, s, 0, 0)),
        ],
        out_specs=[pl.BlockSpec((2, RB, LANES), lambda i, s: (0, i, s))] * 2,
        out_shape=[out, out],
        compiler_params=_cparams(("parallel", "parallel")),
        name="lru_gates",
    )(z, conv_a, wbd, bias, sp)


def _lru_scan_kernel(a_ref, b_ref, h0_ref, h_ref, fin_ref, carry_ref, *, gc, rev):
    s = pl.program_id(1)

    @pl.when(s < gc)
    def _():
        carry_ref[...] = jnp.zeros_like(carry_ref)

    @pl.when(s == gc)
    def _():
        carry_ref[...] = h0_ref[...]

    def body(i, h):
        t = (TB - 1 - i) if rev else i
        r = pl.multiple_of(t * SUBLANES, SUBLANES)
        h = a_ref[0, pl.ds(r, SUBLANES), :] * h + b_ref[0, pl.ds(r, SUBLANES), :]
        h_ref[pl.ds(r, SUBLANES), :] = h
        return h

    h = lax.fori_loop(0, TB, body, carry_ref[...], unroll=8)
    carry_ref[...] = h

    @pl.when(s < gc)
    def _():
        fin_ref[0] = h


def _lru_scan(a, b, h0, *, d, gc, nd, rev):
    rows = a.shape[1]
    nblk = gc + nd
    wc = 512
    blk = lambda c, s: _step_blk(s, gc, nd, rev)
    return pl.pallas_call(
        functools.partial(_lru_scan_kernel, gc=gc, rev=rev),
        grid=(D_A // wc, nblk),
        in_specs=[
            pl.BlockSpec((1, RB, wc), lambda c, s: (d, blk(c, s), c)),
            pl.BlockSpec((1, RB, wc), lambda c, s: (d, blk(c, s), c)),
            pl.BlockSpec((SUBLANES, wc), lambda c, s: (0, c)),
        ],
        out_specs=[
            pl.BlockSpec((RB, wc), lambda c, s: (blk(c, s), c)),
            pl.BlockSpec((1, SUBLANES, wc), lambda c, s: (jnp.minimum(s, gc - 1), 0, c)),
        ],
        out_shape=[
            jax.ShapeDtypeStruct((rows, D_A), f32),
            jax.ShapeDtypeStruct((gc, SUBLANES, D_A), f32),
        ],
        scratch_shapes=[pltpu.VMEM((SUBLANES, wc), f32)],
        compiler_params=_cparams(("parallel", "arbitrary")),
        name="lru_scan_bwd" if rev else "lru_scan_fwd",
    )(a, b, h0)


def _lru_out_kernel(hf_ref, hb_ref, ga_ref, o_ref):
    o_ref[...] = ((hf_ref[...] + hb_ref[...]) * jax.nn.gelu(ga_ref[0])).astype(o_ref.dtype)


def _lru_out(hf, hb, z, *, nblk):
    rows = hf.shape[0]
    return pl.pallas_call(
        _lru_out_kernel,
        grid=(nblk, D_A // LANES),
        in_specs=[
            pl.BlockSpec((RB, LANES), lambda i, s: (i, s)),
            pl.BlockSpec((RB, LANES), lambda i, s: (i, s)),
            pl.BlockSpec((1, RB, LANES), lambda i, s: (E_GA + s, i, 0)),
        ],
        out_specs=pl.BlockSpec((RB, LANES), lambda i, s: (i, s)),
        out_shape=jax.ShapeDtypeStruct((rows, D_A), bf16),
        compiler_params=_cparams(("parallel", "parallel")),
        name="lru_out",
    )(hf, hb, z)


def _delta_kernel(q_ref, k_ref, v_ref, ba_ref, cw_ref, nega_ref, dtb_ref, s0_ref, o_ref, fin_ref,
                  qs_ref, ks_ref, vs_ref, bs_ref, la_ref, gs_ref, st_ref, *, gc, rev, d):
    h = pl.program_id(0)
    s = pl.program_id(1)
    is_ctx = s < gc

    @pl.when(is_ctx)
    def _():
        st_ref[...] = jnp.zeros_like(st_ref)

    @pl.when(s == gc)
    def _():
        st_ref[...] = s0_ref[0]

    def l2n(x):
        return x * lax.rsqrt(jnp.sum(x * x, axis=-1, keepdims=True) + EPS)

    qs_ref[...] = l2n(_silu(_conv4(q_ref[0], cw_ref[0, 0], is_ctx))) * (DK_B ** -0.5)
    ks_ref[...] = l2n(_silu(_conv4(k_ref[0], cw_ref[1, 0], is_ctx)))
    vs_ref[...] = _silu(_conv4(v_ref[0], cw_ref[2, 0], is_ctx))
    ba = ba_ref[0]
    bs_ref[...] = _pick_lane(_sigmoid(ba), d * H_B + h)
    la_ref[...] = _pick_lane(nega_ref[...] * jax.nn.softplus(ba + dtb_ref[...]), 2 * H_B + d * H_B + h)
    _tile_cumsum(la_ref, gs_ref, rev)

    ii = lax.broadcasted_iota(jnp.int32, (CHUNK, CHUNK), 0)
    jj = lax.broadcasted_iota(jnp.int32, (CHUNK, CHUNK), 1)
    incl = (jj >= ii) if rev else (jj <= ii)
    strict = (jj > ii) if rev else (jj < ii)
    eye = (ii == jj).astype(f32)
    last = 0 if rev else CHUNK - 1
    diag8 = (ii // 8) == (jj // 8)
    merge_masks = [((ii // (2 * sz)) == (jj // (2 * sz))) & ((ii // sz) != (jj // sz)) for sz in (8, 16, 32)]

    def chunk_body(ci, carry):
        c = (TB // CHUNK - 1 - ci) if rev else ci
        units = range(SUBLANES)
        rows = [pl.ds(c * CROWS + b, CHUNK, stride=SUBLANES) for b in units]
        q = [qs_ref[r, :] for r in rows]
        k = [ks_ref[r, :] for r in rows]
        v = [vs_ref[r, :] for r in rows]
        beta = [bs_ref[r, :] for r in rows]
        g = [gs_ref[r, :] for r in rows]
        kb = [k[b] * beta[b] for b in units]
        dec_incl, low = [], []
        for b in units:
            gsq = g[b][:, :CHUNK]
            g_row = jnp.sum(gsq * eye, axis=0, keepdims=True)
            diff = gsq - g_row
            dec = jnp.where(incl, jnp.exp(jnp.where(incl, diff, 0.0)), 0.0)
            dec_incl.append(dec)
            low.append(_mm_nt(kb[b], k[b]) * jnp.where(strict, dec, 0.0))
        ld = [jnp.where(diag8, low[b], 0.0) for b in units]
        ls = [_split(ld[b]) for b in units]
        d2 = [_mm3(ls[b], ls[b]) for b in units]
        d2s = [_split(d2[b]) for b in units]
        t_inv = [eye - ld[b] for b in units]
        ts = [_split(t_inv[b]) for b in units]
        prod, d4 = [], []
        for b in units:
            prod.append(_mm3(ts[b], d2s[b]))
            d4.append(_mm3(d2s[b], d2s[b]))
        t_inv = [t_inv[b] + prod[b] for b in units]
        t_inv = [t_inv[b] + _mm3(_split(t_inv[b]), _split(d4[b])) for b in units]
        for lvl in merge_masks:
            ts = [_split(t_inv[b]) for b in units]
            tl = [_mm3(ts[b], _split(jnp.where(lvl, low[b], 0.0))) for b in units]
            t_inv = [t_inv[b] - _mm3(_split(tl[b]), ts[b]) for b in units]
        eg = [jnp.exp(g[b]) for b in units]
        ts = [_split(t_inv[b]) for b in units]
        uw = [_mm3(ts[b], _split(jnp.concatenate([v[b] * beta[b], kb[b] * eg[b]], axis=1))) for b in units]
        a_qk = [_mm_nt(q[b], k[b]) * dec_incl[b] for b in units]
        g_last = [g[b][last:last + 1, :] for b in units]
        st = [st_ref[b] for b in units]
        v_new = [uw[b][:, :DV_B] - _mm(uw[b][:, DV_B:], st[b]) for b in units]
        o = [_mm(q[b] * eg[b], st[b]) + _mm(a_qk[b], v_new[b]) for b in units]
        for b in units:
            k_g = k[b] * jnp.exp(g_last[b] - g[b])
            st_ref[b] = st[b] * jnp.exp(g_last[b]) + _mm_tn(k_g, v_new[b])
            o_ref[0, rows[b], :] = o[b]
        return carry

    lax.fori_loop(0, TB // CHUNK, chunk_body, 0)

    @pl.when(is_ctx)
    def _():
        fin_ref[0, 0] = st_ref[...]


def _delta(z, conv_qkv, nega, dtb, s0, *, d, gc, nd, rev):
    rows = z.shape[1]
    nblk = gc + nd
    blk = lambda h, s: _step_blk(s, gc, nd, rev)
    slab = lambda off: pl.BlockSpec((1, RB, LANES), lambda h, s: (off + h, blk(h, s), 0))
    scr = pltpu.VMEM((RB, LANES), f32)
    return pl.pallas_call(
        functools.partial(_delta_kernel, gc=gc, rev=rev, d=d),
        grid=(H_B, nblk),
        in_specs=[
            slab(E_Q), slab(E_K), slab(E_V),
            pl.BlockSpec((1, RB, LANES), lambda h, s: (E_BA, blk(h, s), 0)),
            pl.BlockSpec((3, 1, 4, LANES), lambda h, s: (0, h, 0, 0)),
            pl.BlockSpec((1, LANES), lambda h, s: (0, 0)),
            pl.BlockSpec((1, LANES), lambda h, s: (0, 0)),
            pl.BlockSpec((1, SUBLANES, DK_B, DV_B), lambda h, s: (h, 0, 0, 0)),
        ],
        out_specs=[
            pl.BlockSpec((1, RB, LANES), lambda h, s: (h, blk(h, s), 0)),
            pl.BlockSpec((1, 1, SUBLANES, DK_B, DV_B), lambda h, s: (jnp.minimum(s, gc - 1), h, 0, 0, 0)),
        ],
        out_shape=[
            jax.ShapeDtypeStruct((H_B, rows, LANES), f32),
            jax.ShapeDtypeStruct((gc, H_B, SUBLANES, DK_B, DV_B), f32),
        ],
        scratch_shapes=[scr, scr, scr, scr, scr, scr, pltpu.VMEM((SUBLANES, DK_B, DV_B), f32)],
        compiler_params=_cparams(("parallel", "arbitrary")),
        name="delta_bwd" if rev else "delta_fwd",
    )(z, z, z, z, conv_qkv, nega, dtb, s0)


def _delta_out_kernel(of_ref, ob_ref, go_ref, g_ref, o_ref):
    o = of_ref[0] + ob_ref[0]
    y = o * lax.rsqrt(jnp.mean(o * o, axis=-1, keepdims=True) + EPS) * g_ref[...]
    o_ref[...] = (y * _silu(go_ref[0])).astype(o_ref.dtype)


def _delta_out(of, ob, z, g, *, nblk):
    rows = of.shape[1]
    return pl.pallas_call(
        _delta_out_kernel,
        grid=(nblk, H_B),
        in_specs=[
            pl.BlockSpec((1, RB, LANES), lambda i, h: (h, i, 0)),
            pl.BlockSpec((1, RB, LANES), lambda i, h: (h, i, 0)),
            pl.BlockSpec((1, RB, LANES), lambda i, h: (E_GO + h, i, 0)),
            pl.BlockSpec((1, LANES), lambda i, h: (0, 0)),
        ],
        out_specs=pl.BlockSpec((RB, LANES), lambda i, h: (i, h)),
        out_shape=jax.ShapeDtypeStruct((rows, H_B * DV_B), bf16),
        compiler_params=_cparams(("parallel", "parallel")),
        name="delta_out",
    )(of, ob, z, g.reshape(1, LANES))


def _s5_kernel(u_ref, bt_ref, ct_ref, lam_ref, s0_ref, y_ref, fin_ref, sbuf_ref, carry_ref, *, gc, rev):
    s = pl.program_id(1)
    half = sbuf_ref.shape[1] // 2

    @pl.when(s < gc)
    def _():
        carry_ref[...] = jnp.zeros_like(carry_ref)

    @pl.when(s == gc)
    def _():
        carry_ref[...] = s0_ref[0]

    sbuf_ref[...] = _mm(u_ref[0], bt_ref[0])
    lr = jnp.broadcast_to(lam_ref[0, 0:1, :], (SUBLANES, half))
    li = jnp.broadcast_to(lam_ref[0, 1:2, :], (SUBLANES, half))

    def body(i, hc):
        hr, hi = hc
        t = (TB - 1 - i) if rev else i
        r = pl.multiple_of(t * SUBLANES, SUBLANES)
        nr = lr * hr - li * hi + sbuf_ref[pl.ds(r, SUBLANES), :half]
        ni = lr * hi + li * hr + sbuf_ref[pl.ds(r, SUBLANES), half:]
        sbuf_ref[pl.ds(r, SUBLANES), :half] = nr
        sbuf_ref[pl.ds(r, SUBLANES), half:] = ni
        return nr, ni

    hr, hi = lax.fori_loop(0, TB, body, (carry_ref[:, :half], carry_ref[:, half:]), unroll=4)
    carry_ref[:, :half] = hr
    carry_ref[:, half:] = hi
    y_ref[0] = _mm(sbuf_ref[...], ct_ref[0])

    @pl.when(s < gc)
    def _():
        fin_ref[0, 0] = carry_ref[...]


def _s5(z, bt, ct, lam, s0, *, gc, nd, rev):
    rows = z.shape[1]
    nblk = gc + nd
    ngt = D_C // LANES
    sw = 2 * (LANES // S5_GW) * S5_N
    blk = lambda g, s: _step_blk(s, gc, nd, rev)
    return pl.pallas_call(
        functools.partial(_s5_kernel, gc=gc, rev=rev),
        grid=(ngt, nblk),
        in_specs=[
            pl.BlockSpec((1, RB, LANES), lambda g, s: (O_U + g, blk(g, s), 0)),
            pl.BlockSpec((1, LANES, sw), lambda g, s: (g, 0, 0)),
            pl.BlockSpec((1, sw, LANES), lambda g, s: (g, 0, 0)),
            pl.BlockSpec((1, 2, sw // 2), lambda g, s: (g, 0, 0)),
            pl.BlockSpec((1, SUBLANES, sw), lambda g, s: (g, 0, 0)),
        ],
        out_specs=[
            pl.BlockSpec((1, RB, LANES), lambda g, s: (g, blk(g, s), 0)),
            pl.BlockSpec((1, 1, SUBLANES, sw), lambda g, s: (jnp.minimum(s, gc - 1), g, 0, 0)),
        ],
        out_shape=[
            jax.ShapeDtypeStruct((ngt, rows, LANES), f32),
            jax.ShapeDtypeStruct((gc, ngt, SUBLANES, sw), f32),
        ],
        scratch_shapes=[pltpu.VMEM((RB, sw), f32), pltpu.VMEM((SUBLANES, sw), f32)],
        compiler_params=_cparams(("parallel", "arbitrary")),
        name="s5_bwd" if rev else "s5_fwd",
    )(z, bt, ct, lam, s0)


def _s5_out_kernel(yf_ref, yb_ref, u_ref, d_ref, w_ref, b_ref, o_ref):
    ns = yf_ref.shape[0]
    yc = jnp.concatenate([yf_ref[s] + yb_ref[s] + d_ref[s] * u_ref[s] for s in range(ns)], axis=1)
    zc = jax.nn.gelu(yc)
    gate = _sigmoid(jnp.dot(zc.astype(bf16), w_ref[...], preferred_element_type=f32) + b_ref[...])
    o_ref[...] = (zc * gate).astype(o_ref.dtype)


def _s5_out(yf, yb, z, dskip, w_glu, b_glu, *, tm):
    ns, rows, _ = yf.shape
    return pl.pallas_call(
        _s5_out_kernel,
        grid=(rows // tm,),
        in_specs=[
            pl.BlockSpec((ns, tm, LANES), lambda i: (0, i, 0)),
            pl.BlockSpec((ns, tm, LANES), lambda i: (0, i, 0)),
            pl.BlockSpec((ns, tm, LANES), lambda i: (O_U // ns, i, 0)),
            pl.BlockSpec((ns, 1, LANES), lambda i: (0, 0, 0)),
            pl.BlockSpec((D_C, D_C), lambda i: (0, 0)),
            pl.BlockSpec((1, D_C), lambda i: (0, 0)),
        ],
        out_specs=pl.BlockSpec((tm, D_C), lambda i: (i, 0)),
        out_shape=jax.ShapeDtypeStruct((rows, D_C), bf16),
        compiler_params=_cparams(("parallel",)),
        name="s5_out",
    )(yf, yb, z, dskip, w_glu, b_glu)


def _gla_kernel(q_ref, k_ref, v_ref, glr_ref, wg_ref, bg_ref, s0_ref, o_ref, fin_ref,
                la_ref, gs_ref, st_ref, *, gc, rev):
    s = pl.program_id(1)
    is_ctx = s < gc

    @pl.when(is_ctx)
    def _():
        st_ref[...] = jnp.zeros_like(st_ref)

    @pl.when(s == gc)
    def _():
        st_ref[...] = s0_ref[0]

    la_ref[...] = jax.nn.log_sigmoid(_mm(glr_ref[0], wg_ref[0]) + bg_ref[0]) * (1.0 / GLA_TAU)
    _tile_cumsum(la_ref, gs_ref, rev)

    ii = lax.broadcasted_iota(jnp.int32, (SUB, CHUNK), 0)
    jj = lax.broadcasted_iota(jnp.int32, (SUB, CHUNK), 1)
    last = 0 if rev else CHUNK - 1

    def chunk_body(ci, carry):
        c = (TB // CHUNK - 1 - ci) if rev else ci
        for b in range(SUBLANES):
            rows = pl.ds(c * CROWS + b, CHUNK, stride=SUBLANES)
            q = q_ref[0, rows, :] * (DK_D ** -0.5)
            k = k_ref[0, rows, :]
            v = jnp.concatenate([v_ref[0, rows, :], v_ref[1, rows, :]], axis=1)
            g = gs_ref[rows, :]
            a_rows = []
            for blk in range(CHUNK // SUB):
                lo = blk * SUB
                first = lo + SUB - 1 if rev else lo
                c_ref = g[first:first + 1, :]
                qt = q[lo:lo + SUB, :] * jnp.exp(g[lo:lo + SUB, :] - c_ref)
                kt = k * jnp.exp(c_ref - g)
                a_blk = _mm_nt(qt, kt)
                causal = (jj >= ii + lo) if rev else (jj <= ii + lo)
                a_rows.append(jnp.where(causal, a_blk, 0.0))
            a_qk = jnp.concatenate(a_rows, axis=0)
            g_last = g[last:last + 1, :]
            q_g = q * jnp.exp(g)
            k_g = k * jnp.exp(g_last - g)
            st = st_ref[b]
            o = _mm_nt(q_g, st) + _mm(a_qk, v)
            st_ref[b] = st * jnp.exp(g_last) + _mm_tn(v, k_g)
            o_ref[0, rows, :] = o[:, :LANES]
            o_ref[1, rows, :] = o[:, LANES:]
        return carry

    lax.fori_loop(0, TB // CHUNK, chunk_body, 0)

    @pl.when(is_ctx)
    def _():
        fin_ref[0, 0] = st_ref[...]


def _gla(z, wg, bg, s0, *, gc, nd, rev):
    rows = z.shape[1]
    nblk = gc + nd
    nv = DV_D // LANES
    blk = lambda h, s: _step_blk(s, gc, nd, rev)
    scr = pltpu.VMEM((RB, LANES), f32)
    return pl.pallas_call(
        functools.partial(_gla_kernel, gc=gc, rev=rev),
        grid=(H_D, nblk),
        in_specs=[
            pl.BlockSpec((1, RB, LANES), lambda h, s: (O_Q + h, blk(h, s), 0)),
            pl.BlockSpec((1, RB, LANES), lambda h, s: (O_K + h, blk(h, s), 0)),
            pl.BlockSpec((nv, RB, LANES), lambda h, s: (O_V // nv + h, blk(h, s), 0)),
            pl.BlockSpec((1, RB, LANES), lambda h, s: (O_GLR, blk(h, s), 0)),
            pl.BlockSpec((1, LANES, LANES), lambda h, s: (h, 0, 0)),
            pl.BlockSpec((1, 1, LANES), lambda h, s: (h, 0, 0)),
            pl.BlockSpec((1, SUBLANES, DV_D, DK_D), lambda h, s: (h, 0, 0, 0)),
        ],
        out_specs=[
            pl.BlockSpec((nv, RB, LANES), lambda h, s: (h, blk(h, s), 0)),
            pl.BlockSpec((1, 1, SUBLANES, DV_D, DK_D), lambda h, s: (jnp.minimum(s, gc - 1), h, 0, 0, 0)),
        ],
        out_shape=[
            jax.ShapeDtypeStruct((H_D * nv, rows, LANES), f32),
            jax.ShapeDtypeStruct((gc, H_D, SUBLANES, DV_D, DK_D), f32),
        ],
        scratch_shapes=[scr, scr, pltpu.VMEM((SUBLANES, DV_D, DK_D), f32)],
        compiler_params=_cparams(("parallel", "arbitrary")),
        name="gla_bwd" if rev else "gla_fwd",
    )(z, z, z, z, wg, bg, s0)


def _gla_out_kernel(of_ref, ob_ref, r_ref, g_ref, o_ref):
    nv = of_ref.shape[0]
    o = [of_ref[j] + ob_ref[j] for j in range(nv)]
    ms = sum(jnp.sum(x * x, axis=-1, keepdims=True) for x in o) * (1.0 / DV_D)
    inv = lax.rsqrt(ms + EPS)
    for j in range(nv):
        y = o[j] * inv * g_ref[j] * _silu(r_ref[j])
        o_ref[:, j * LANES:(j + 1) * LANES] = y.astype(o_ref.dtype)


def _gla_out(of, ob, z, g, *, nblk):
    rows = of.shape[1]
    nv = DV_D // LANES
    return pl.pallas_call(
        _gla_out_kernel,
        grid=(nblk, H_D),
        in_specs=[
            pl.BlockSpec((nv, RB, LANES), lambda i, h: (h, i, 0)),
            pl.BlockSpec((nv, RB, LANES), lambda i, h: (h, i, 0)),
            pl.BlockSpec((nv, RB, LANES), lambda i, h: (O_R // nv + h, i, 0)),
            pl.BlockSpec((nv, 1, LANES), lambda i, h: (0, 0, 0)),
        ],
        out_specs=pl.BlockSpec((RB, DV_D), lambda i, h: (i, h)),
        out_shape=jax.ShapeDtypeStruct((rows, H_D * DV_D), bf16),
        compiler_params=_cparams(("parallel", "parallel")),
        name="gla_out",
    )(of, ob, z, g.reshape(nv, 1, LANES))


def _to_rows(x):
    b, t, d = x.shape
    return x.reshape(b // SUBLANES, SUBLANES, t, d).transpose(0, 2, 1, 3).reshape(b * t, d)


def _from_rows(r, b, t):
    d = r.shape[-1]
    return r.reshape(b // SUBLANES, t, SUBLANES, d).transpose(0, 2, 1, 3).reshape(b, t, d)


def _pad_cols(w, n):
    return jnp.pad(w, ((0, 0), (0, n - w.shape[1])))


def _unit_states(fin, b):
    fin = jnp.moveaxis(fin, b, 1)
    return fin.reshape((fin.shape[0] * SUBLANES,) + fin.shape[2:])


def kernel(x_prompt, x_sample, state_lru, state_delta, state_s5_re, state_s5_im, state_gla, c, c_ctx, w_ada, b_ada, norm_g, w_up, w_down, w_in_e, w_out_e, conv_a, lru_wa, lru_ba, lru_wx, lru_bx, lru_lambda, conv_qkv, dn_a_log, dn_dt_bias, dn_norm_g, w_in_o, w_out_o, s5_lam_re, s5_lam_im, s5_log_step, s5_b_re, s5_b_im, s5_c_re, s5_c_im, s5_d, s5_w_glu, s5_b_glu, gla_wg2, gla_bg, gla_norm_g):
    bp, tp, d = x_prompt.shape
    bs, ts, _ = x_sample.shape
    assert tp == TB and bp % SUBLANES == 0 and bs == SUBLANES and ts % TB == 0 and d == D_MODEL
    depth = w_ada.shape[0]
    gc = bp // SUBLANES
    nd = ts // TB
    nblk = gc + nd
    rows_ctx = bp * tp
    seq = dict(gc=gc, nd=nd)

    x = jnp.concatenate([_to_rows(x_prompt), _to_rows(x_sample)], axis=0)

    cvec = jnp.concatenate([c, jnp.broadcast_to(c_ctx[None], (SUBLANES, d))], axis=0)
    mod = _ada(cvec, w_ada, b_ada)
    mod = jnp.stack([mod[:, SUBLANES:], mod[:, :SUBLANES]], axis=1)
    mods = [[mod[l, :, :, i * d:(i + 1) * d] for i in range(6)] for l in range(depth)]

    fin_lru, fin_delta, fin_re, fin_im, fin_gla = [], [], [], [], []
    for l in range(depth):
        sh1, sc1, gt1, sh2, sc2, gt2 = mods[l]
        if l % 2 == 0:
            e = l // 2
            w_in = _pad_cols(w_in_e[e], E_SLABS * LANES).astype(bf16)
            z = _nmm(x, norm_g[l, 0], sh1, sc1, w_in, rows_ctx=rows_ctx, tm=512, tn=10 * LANES, slab=True)
            ns = D_A // LANES
            wg = jnp.stack([lru_wa[e], lru_wx[e]], axis=1).reshape(2, 2, ns, 2, LRU_BW, LRU_BW)
            wbd = jnp.zeros((2, 2, ns, LANES, LANES), f32)
            wbd = wbd.at[:, :, :, :LRU_BW, :LRU_BW].set(wg[:, :, :, 0]).at[:, :, :, LRU_BW:, LRU_BW:].set(wg[:, :, :, 1])
            bias = jnp.stack([lru_ba[e], lru_bx[e]], axis=1).reshape(2, 2, ns, 1, LANES)
            sp = jax.nn.softplus(-lru_lambda[e]).reshape(2, ns, 1, LANES)
            cwa = conv_a[e].reshape(4, ns, LANES).transpose(1, 0, 2)
            a, b = _lru_gates(z, cwa, wbd.astype(bf16), bias, sp, gc=gc, nblk=nblk)
            h_f, f_f = _lru_scan(a, b, state_lru[:, e, 0], d=0, rev=False, **seq)
            h_b, f_b = _lru_scan(a, b, state_lru[:, e, 1], d=1, rev=True, **seq)
            y_a = _lru_out(h_f, h_b, z, nblk=nblk)
            fin_lru.append(jnp.stack([_unit_states(f_f, 1), _unit_states(f_b, 1)], axis=1))
            cq = conv_qkv[e].reshape(4, 3, H_B, LANES).transpose(1, 2, 0, 3)
            lane_pad = lambda v: jnp.pad(v.reshape(1, 2 * H_B), ((0, 0), (2 * H_B, LANES - 4 * H_B)))
            nega = lane_pad(-jnp.exp(dn_a_log[e]))
            dtb = lane_pad(dn_dt_bias[e])
            s0 = state_delta[:, e].transpose(1, 2, 0, 3, 4)
            o_f, s_f = _delta(z, cq, nega, dtb, s0[0], d=0, rev=False, **seq)
            o_b, s_b = _delta(z, cq, nega, dtb, s0[1], d=1, rev=True, **seq)
            y_b = _delta_out(o_f, o_b, z, dn_norm_g[e], nblk=nblk)
            fin_delta.append(jnp.stack([_unit_states(s_f, 2), _unit_states(s_b, 2)], axis=1))
            y_mix = jnp.concatenate([y_a, y_b], axis=1)
            w_out = w_out_e[e].astype(bf16)
        else:
            o = l // 2
            w_in = _pad_cols(w_in_o[o], O_SLABS * LANES).astype(bf16)
            z = _nmm(x, norm_g[l, 0], sh1, sc1, w_in, rows_ctx=rows_ctx, tm=512, tn=11 * LANES, slab=True)
            ngt = D_C // LANES
            gps = LANES // S5_GW
            lam = lax.complex(s5_lam_re[o], s5_lam_im[o])
            lam_bar = jnp.exp(lam * jnp.exp(s5_log_step[o])[..., None])
            b_bar = ((lam_bar - 1.0) / lam)[..., None] * lax.complex(s5_b_re[o], s5_b_im[o])
            eye_g = jnp.eye(gps, dtype=f32)

            def bmat(p):
                p = p.reshape(2, ngt, gps, S5_N, S5_GW)
                return jnp.einsum('dtgnw,gh->dtgwhn', p, eye_g).reshape(2, ngt, LANES, gps * S5_N)

            def cmat(p):
                p = p.reshape(2, ngt, gps, S5_GW, S5_N)
                return jnp.einsum('dtgwn,gh->dtgnhw', p, eye_g).reshape(2, ngt, gps * S5_N, LANES)

            bt = jnp.concatenate([bmat(b_bar.real), bmat(b_bar.imag)], axis=-1).astype(bf16)
            ct = jnp.concatenate([cmat(s5_c_re[o]), -cmat(s5_c_im[o])], axis=-2).astype(bf16)
            lamv = jnp.stack([lam_bar.real, lam_bar.imag], axis=2).reshape(2, ngt, gps, 2, S5_N)
            lamv = lamv.transpose(0, 1, 3, 2, 4).reshape(2, ngt, 2, gps * S5_N)

            def s5_state(sre, sim):
                f = lambda p: p.reshape(SUBLANES, ngt, gps * S5_N).transpose(1, 0, 2)
                return jnp.concatenate([f(sre), f(sim)], axis=-1)

            y_f, c_f = _s5(z, bt[0], ct[0], lamv[0], s5_state(state_s5_re[:, o, 0], state_s5_im[:, o, 0]), rev=False, **seq)
            y_bk, c_b = _s5(z, bt[1], ct[1], lamv[1], s5_state(state_s5_re[:, o, 1], state_s5_im[:, o, 1]), rev=True, **seq)
            y_c = _s5_out(y_f, y_bk, z, s5_d[o].reshape(ngt, 1, LANES), s5_w_glu[o].astype(bf16),
                          s5_b_glu[o].reshape(1, D_C), tm=512)
            half = gps * S5_N

            def s5_fin(cf):
                cf = cf.transpose(0, 2, 1, 3).reshape(gc * SUBLANES, ngt, 2, gps, S5_N)
                return (cf[:, :, 0].reshape(gc * SUBLANES, S5_G, S5_N), cf[:, :, 1].reshape(gc * SUBLANES, S5_G, S5_N))

            (rf, imf), (rb, imb) = s5_fin(c_f), s5_fin(c_b)
            fin_re.append(jnp.stack([rf, rb], axis=1))
            fin_im.append(jnp.stack([imf, imb], axis=1))
            wgp = jnp.zeros((2, H_D, LANES, LANES), f32)
            wg2 = gla_wg2[o].reshape(2, GLA_RANK, H_D, DK_D).transpose(0, 2, 1, 3)
            wgp = wgp.at[0, :, :GLA_RANK].set(wg2[0]).at[1, :, GLA_RANK:2 * GLA_RANK].set(wg2[1]).astype(bf16)
            bgp = gla_bg[o].reshape(2, H_D, 1, DK_D)
            g0 = state_gla[:, o].transpose(1, 2, 0, 4, 3)
            o_f, g_f = _gla(z, wgp[0], bgp[0], g0[0], rev=False, **seq)
            o_b, g_b = _gla(z, wgp[1], bgp[1], g0[1], rev=True, **seq)
            y_d = _gla_out(o_f, o_b, z, gla_norm_g[o], nblk=nblk)
            gfin = lambda gf: jnp.swapaxes(_unit_states(gf, 2), -1, -2)
            fin_gla.append(jnp.stack([gfin(g_f), gfin(g_b)], axis=1))
            y_mix = jnp.concatenate([y_c, y_d], axis=1)
            w_out = w_out_o[o].astype(bf16)

        x = _mmres(y_mix, w_out, x, norm_g[l, 1], gt1, rows_ctx=rows_ctx, tm=512, tk=w_out.shape[0])
        hmid = _nmm(x, norm_g[l, 2], sh2, sc2, w_up[l].astype(bf16), rows_ctx=rows_ctx, tm=512, tn=1024,
                    relu2=True, out_dtype=bf16)
        x = _mmres(hmid, w_down[l].astype(bf16), x, norm_g[l, 3], gt2, rows_ctx=rows_ctx, tm=512, tk=2048)

    y_prompt = _from_rows(x[:rows_ctx], bp, tp)
    y_sample = _from_rows(x[rows_ctx:], bs, ts)
    return (y_prompt, y_sample, jnp.stack(fin_lru, axis=1), jnp.stack(fin_delta, axis=1),
            jnp.stack(fin_re, axis=1), jnp.stack(fin_im, axis=1), jnp.stack(fin_gla, axis=1))
```

```python
import functools
import math

import jax
import jax.numpy as jnp
from jax import lax
from jax.experimental import pallas as pl
from jax.experimental.pallas import tpu as pltpu

f32 = jnp.float32
bf16 = jnp.bfloat16

LANES = 128
SUBLANES = 8
VMEM_LIMIT_BYTES = 56 * 1024 * 1024

D_MODEL = 2048
D_FF = 4 * D_MODEL
GRID_W = 64
CHUNK = 64
EPS = 1e-6
D_A = D_MODEL // 2
LRU_BLOCKS = 16
LRU_BW = D_A // LRU_BLOCKS
LRU_C = 8.0
H_B = 8
DK_B = 128
DV_B = 128
D_C = D_MODEL // 2
S5_GW = 16
S5_G = D_C // S5_GW
S5_N = 64
H_D = 4
DK_D = 128
DV_D = 256
GLA_RANK = 16
GLA_TAU = 16.0

TB = 256
RB = TB * SUBLANES
CROWS = CHUNK * SUBLANES
SUB = 16

E_XA, E_GA, E_Q, E_K, E_V, E_GO, E_BA = 0, 8, 16, 24, 32, 40, 48
E_SLABS = 50
O_U, O_Q, O_K, O_V, O_R, O_GLR = 0, 8, 12, 16, 24, 32
O_SLABS = 33


def _cparams(sem):
    return pltpu.CompilerParams(dimension_semantics=sem, vmem_limit_bytes=VMEM_LIMIT_BYTES)


def _mm(a, b):
    return jnp.dot(a.astype(bf16), b.astype(bf16), preferred_element_type=f32)


def _mm_nt(a, b):
    return lax.dot_general(a.astype(bf16), b.astype(bf16), (((1,), (1,)), ((), ())), preferred_element_type=f32)


def _mm_tn(a, b):
    return lax.dot_general(a.astype(bf16), b.astype(bf16), (((0,), (0,)), ((), ())), preferred_element_type=f32)


def _split(x):
    hi = x.astype(bf16)
    return hi, (x - hi.astype(f32)).astype(bf16)


def _mm3(a, b):
    lhs = jnp.concatenate([a[0], a[1], a[0]], axis=1)
    rhs = jnp.concatenate([b[0], b[0], b[1]], axis=0)
    return jnp.dot(lhs, rhs, preferred_element_type=f32)


def _sigmoid(x):
    return 0.5 * jnp.tanh(0.5 * x) + 0.5


def _silu(x):
    t = 0.5 * x
    return t + t * jnp.tanh(t)


def _ada_kernel(c_ref, w_ref, b_ref, o_ref):
    a = _silu(c_ref[...])
    o_ref[0] = _mm(a, w_ref[0]) + b_ref[0]


def _ada(cvec, w_ada, b_ada):
    depth, d, n = w_ada.shape
    tn = 1024
    return pl.pallas_call(
        _ada_kernel,
        grid=(depth, n // tn),
        in_specs=[
            pl.BlockSpec(cvec.shape, lambda l, j: (0, 0)),
            pl.BlockSpec((1, d, tn), lambda l, j: (l, 0, j)),
            pl.BlockSpec((1, 1, tn), lambda l, j: (l, 0, j)),
        ],
        out_specs=pl.BlockSpec((1, cvec.shape[0], tn), lambda l, j: (l, 0, j)),
        out_shape=jax.ShapeDtypeStruct((depth, cvec.shape[0], n), f32),
        compiler_params=_cparams(("parallel", "parallel")),
        name="ada",
    )(cvec, w_ada, b_ada.reshape(depth, 1, n))


def _nmm_kernel(x_ref, g_ref, sh_ref, sc_ref, w_ref, o_ref, h_ref, *, relu2, slab):
    @pl.when(pl.program_id(1) == 0)
    def _():
        x = x_ref[...]
        tm, d = x.shape
        y = x * lax.rsqrt(jnp.mean(x * x, axis=-1, keepdims=True) + EPS)
        gain = g_ref[...] * (1.0 + sc_ref[0])
        h = y.reshape(tm // SUBLANES, SUBLANES, d) * gain[None] + sh_ref[0][None]
        h_ref[...] = h.reshape(tm, d).astype(bf16)

    acc = jnp.dot(h_ref[...], w_ref[...], preferred_element_type=f32)
    if relu2:
        acc = jnp.square(jnp.maximum(acc, 0.0))
    if slab:
        for s in range(acc.shape[1] // LANES):
            o_ref[s] = acc[:, s * LANES:(s + 1) * LANES].astype(o_ref.dtype)
    else:
        o_ref[...] = acc.astype(o_ref.dtype)


def _nmm(x, g, sh, sc, w, *, rows_ctx, tm, tn, relu2=False, slab=False, out_dtype=f32):
    rows, d = x.shape
    n = w.shape[1]
    grp = lambda i, j: ((i * tm >= rows_ctx).astype(jnp.int32), 0, 0)
    if slab:
        out_shape = jax.ShapeDtypeStruct((n // LANES, rows, LANES), out_dtype)
        out_spec = pl.BlockSpec((tn // LANES, tm, LANES), lambda i, j: (j, i, 0))
    else:
        out_shape = jax.ShapeDtypeStruct((rows, n), out_dtype)
        out_spec = pl.BlockSpec((tm, tn), lambda i, j: (i, j))
    return pl.pallas_call(
        functools.partial(_nmm_kernel, relu2=relu2, slab=slab),
        grid=(rows // tm, n // tn),
        in_specs=[
            pl.BlockSpec((tm, d), lambda i, j: (i, 0)),
            pl.BlockSpec((1, d), lambda i, j: (0, 0)),
            pl.BlockSpec((1, SUBLANES, d), grp),
            pl.BlockSpec((1, SUBLANES, d), grp),
            pl.BlockSpec((d, tn), lambda i, j: (0, j)),
        ],
        out_specs=out_spec,
        out_shape=out_shape,
        scratch_shapes=[pltpu.VMEM((tm, d), bf16)],
        compiler_params=_cparams(("parallel", "arbitrary")),
        name="nmm",
    )(x, g.reshape(1, d), sh, sc, w)


def _mmres_kernel(a_ref, w_ref, x_ref, g_ref, gt_ref, o_ref, acc_ref):
    k = pl.program_id(1)

    @pl.when(k == 0)
    def _():
        acc_ref[...] = jnp.zeros_like(acc_ref)

    acc_ref[...] += jnp.dot(a_ref[...], w_ref[...], preferred_element_type=f32)

    @pl.when(k == pl.num_programs(1) - 1)
    def _():
        _gated_residual(acc_ref[...], x_ref, g_ref, gt_ref, o_ref)


def _gated_residual(y, x_ref, g_ref, gt_ref, o_ref):
    tm, d = y.shape
    yn = y * lax.rsqrt(jnp.mean(y * y, axis=-1, keepdims=True) + EPS) * g_ref[...]
    o = x_ref[...].reshape(tm // SUBLANES, SUBLANES, d) + gt_ref[0][None] * yn.reshape(tm // SUBLANES, SUBLANES, d)
    o_ref[...] = o.reshape(tm, d)


def _headnorm_gate(o_slabs, gain, gate_slabs):
    width = len(o_slabs) * LANES
    ms = sum(jnp.sum(o * o, axis=-1, keepdims=True) for o in o_slabs) * (1.0 / width)
    inv = lax.rsqrt(ms + EPS)
    return [(o * inv * gain[j] * _silu(r)).astype(bf16) for j, (o, r) in enumerate(zip(o_slabs, gate_slabs))]


def _mixout_even_kernel(hf_ref, hb_ref, ga_ref, of_ref, ob_ref, go_ref, dg_ref, w1_ref, w2_ref,
                        x_ref, g_ref, gt_ref, o_ref):
    ns = ga_ref.shape[0]
    ga = jnp.concatenate([ga_ref[s] for s in range(ns)], axis=1)
    y_a = ((hf_ref[...] + hb_ref[...]) * jax.nn.gelu(ga)).astype(bf16)
    y = jnp.dot(y_a, w1_ref[...], preferred_element_type=f32)
    y_b = [_headnorm_gate([of_ref[h] + ob_ref[h]], [dg_ref[...]], [go_ref[h]])[0] for h in range(of_ref.shape[0])]
    y = y + jnp.dot(jnp.concatenate(y_b, axis=1), w2_ref[...], preferred_element_type=f32)
    _gated_residual(y, x_ref, g_ref, gt_ref, o_ref)


def _mixout_odd_kernel(yc_ref, of_ref, ob_ref, r_ref, dg_ref, w1_ref, w2_ref, x_ref, g_ref, gt_ref, o_ref):
    nv = DV_D // LANES
    y = jnp.dot(yc_ref[...], w1_ref[...], preferred_element_type=f32)
    y_d = []
    for h in range(of_ref.shape[0] // nv):
        sl = range(h * nv, (h + 1) * nv)
        y_d += _headnorm_gate([of_ref[j] + ob_ref[j] for j in sl], [dg_ref[j] for j in range(nv)], [r_ref[j] for j in sl])
    y = y + jnp.dot(jnp.concatenate(y_d, axis=1), w2_ref[...], preferred_element_type=f32)
    _gated_residual(y, x_ref, g_ref, gt_ref, o_ref)


def _mixout_specs(w, x, g, gt, rows_ctx, tm):
    kh = w.shape[0] // 2
    d = w.shape[1]
    grp = lambda i: ((i * tm >= rows_ctx).astype(jnp.int32), 0, 0)
    specs = [
        pl.BlockSpec((kh, d), lambda i: (0, 0), pipeline_mode=pl.Buffered(1)),
        pl.BlockSpec((kh, d), lambda i: (1, 0), pipeline_mode=pl.Buffered(1)),
        pl.BlockSpec((tm, d), lambda i: (i, 0)),
        pl.BlockSpec((1, d), lambda i: (0, 0)),
        pl.BlockSpec((1, SUBLANES, d), grp),
    ]
    return specs, (w, w, x, g.reshape(1, d), gt)


def _mixout_even(hf, hb, of, ob, z, dn_g, w, x, g, gt, *, rows_ctx, tm):
    rows, d = x.shape
    ns = D_A // LANES
    slabs = lambda off: pl.BlockSpec((ns, tm, LANES), lambda i: (off // ns, i, 0))
    tail_specs, tail_args = _mixout_specs(w, x, g, gt, rows_ctx, tm)
    return pl.pallas_call(
        _mixout_even_kernel,
        grid=(rows // tm,),
        in_specs=[
            pl.BlockSpec((tm, D_A), lambda i: (i, 0)),
            pl.BlockSpec((tm, D_A), lambda i: (i, 0)),
            slabs(E_GA), slabs(0), slabs(0), slabs(E_GO),
            pl.BlockSpec((1, LANES), lambda i: (0, 0)),
        ] + tail_specs,
        out_specs=pl.BlockSpec((tm, d), lambda i: (i, 0)),
        out_shape=jax.ShapeDtypeStruct((rows, d), f32),
        compiler_params=_cparams(("parallel",)),
        name="mixout_even",
    )(hf, hb, z, of, ob, z, dn_g.reshape(1, LANES), *tail_args)


def _mixout_odd(y_c, of, ob, z, gla_g, w, x, g, gt, *, rows_ctx, tm):
    rows, d = x.shape
    ns = H_D * DV_D // LANES
    nv = DV_D // LANES
    slabs = lambda off: pl.BlockSpec((ns, tm, LANES), lambda i: (off // ns, i, 0))
    tail_specs, tail_args = _mixout_specs(w, x, g, gt, rows_ctx, tm)
    return pl.pallas_call(
        _mixout_odd_kernel,
        grid=(rows // tm,),
        in_specs=[
            pl.BlockSpec((tm, D_C), lambda i: (i, 0)),
            slabs(0), slabs(0), slabs(O_R),
            pl.BlockSpec((nv, 1, LANES), lambda i: (0, 0, 0)),
        ] + tail_specs,
        out_specs=pl.BlockSpec((tm, d), lambda i: (i, 0)),
        out_shape=jax.ShapeDtypeStruct((rows, d), f32),
        compiler_params=_cparams(("parallel",)),
        name="mixout_odd",
    )(y_c, of, ob, z, gla_g.reshape(nv, 1, LANES), *tail_args)


def _mmres(a, w, x, g, gt, *, rows_ctx, tm, tk):
    rows, kdim = a.shape
    d = w.shape[1]
    grp = lambda i, k: ((i * tm >= rows_ctx).astype(jnp.int32), 0, 0)
    return pl.pallas_call(
        _mmres_kernel,
        grid=(rows // tm, kdim // tk),
        in_specs=[
            pl.BlockSpec((tm, tk), lambda i, k: (i, k)),
            pl.BlockSpec((tk, d), lambda i, k: (k, 0)),
            pl.BlockSpec((tm, d), lambda i, k: (i, 0)),
            pl.BlockSpec((1, d), lambda i, k: (0, 0)),
            pl.BlockSpec((1, SUBLANES, d), grp),
        ],
        out_specs=pl.BlockSpec((tm, d), lambda i, k: (i, 0)),
        out_shape=jax.ShapeDtypeStruct((rows, d), f32),
        scratch_shapes=[pltpu.VMEM((tm, d), f32)],
        compiler_params=_cparams(("parallel", "arbitrary")),
        name="mmres",
    )(a, w, x, g.reshape(1, d), gt)


def _step_blk(s, gc, nd, rev):
    if not rev:
        return s
    return jnp.where(s < gc, s, 2 * gc + nd - 1 - s)


def _conv4(x, w4, is_ctx):
    rows = x.shape[0]
    t = lax.broadcasted_iota(jnp.int32, x.shape, 0) // SUBLANES
    tl = jnp.where(is_ctx, t, t % GRID_W)
    last = jnp.where(is_ctx, TB - 1, GRID_W - 1)
    y = x * w4[2:3, :]
    xm2 = pltpu.roll(x, 2 * SUBLANES, 0)
    y = y + jnp.where(tl >= 2, xm2, 0.0) * w4[0:1, :]
    xm1 = pltpu.roll(x, SUBLANES, 0)
    y = y + jnp.where(tl >= 1, xm1, 0.0) * w4[1:2, :]
    xp1 = pltpu.roll(x, rows - SUBLANES, 0)
    y = y + jnp.where(tl < last, xp1, 0.0) * w4[3:4, :]
    return y


def _tile_cumsum(src_ref, dst_ref, rev):
    ntile = src_ref.shape[0] // SUBLANES

    def body(i, run):
        t = (ntile - 1 - i) if rev else i
        r = pl.multiple_of(t * SUBLANES, SUBLANES)
        run = jnp.where(i % CHUNK == 0, 0.0, run) + src_ref[pl.ds(r, SUBLANES), :]
        dst_ref[pl.ds(r, SUBLANES), :] = run
        return run

    lax.fori_loop(0, ntile, body, jnp.zeros((SUBLANES, src_ref.shape[1]), f32), unroll=8)


def _pick_lane(x, lane):
    onehot = lax.broadcasted_iota(jnp.int32, x.shape, 1) == lane
    col = jnp.sum(jnp.where(onehot, x, 0.0), axis=-1, keepdims=True)
    return jnp.broadcast_to(col, x.shape)


def _lru_gate_kernel(xa_ref, cw_ref, w_ref, bias_ref, sp_ref, a_ref, b_ref, *, gc):
    is_ctx = pl.program_id(0) < gc
    u = _conv4(xa_ref[0], cw_ref[0], is_ctx)
    for d in range(2):
        r = _sigmoid(_mm(u, w_ref[d, 0, 0]) + bias_ref[d, 0, 0])
        i = _sigmoid(_mm(u, w_ref[d, 1, 0]) + bias_ref[d, 1, 0])
        log_a = -LRU_C * r * sp_ref[d, 0]
        a = jnp.exp(log_a)
        b = jnp.sqrt(-jnp.tanh(log_a) * (a * a + 1.0)) * (i * u)
        a_ref[d] = a
        b_ref[d] = b


def _lru_gates(z, conv_a, wbd, bias, sp, *, gc, nblk):
    rows = z.shape[1]
    ns = D_A // LANES
    out = jax.ShapeDtypeStruct((2, rows, D_A), f32)
    return pl.pallas_call(
        functools.partial(_lru_gate_kernel, gc=gc),
        grid=(nblk, ns),
        in_specs=[
            pl.BlockSpec((1, RB, LANES), lambda i, s: (E_XA + s, i, 0)),
            pl.BlockSpec((1, 4, LANES), lambda i, s: (s, 0, 0)),
            pl.BlockSpec((2, 2, 1, LANES, LANES), lambda i, s: (0, 0, s, 0, 0)),
            pl.BlockSpec((2, 2, 1, 1, LANES), lambda i, s: (0, 0, s, 0, 0)),
            pl.BlockSpec((2, 1, 1, LANES), lambda i, s: (0, s, 0, 0)),
        ],
        out_specs=[pl.BlockSpec((2, RB, LANES), lambda i, s: (0, i, s))] * 2,
        out_shape=[out, out],
        compiler_params=_cparams(("parallel", "parallel")),
        name="lru_gates",
    )(z, conv_a, wbd, bias, sp)


def _lru_scan_kernel(a_ref, b_ref, h0_ref, h_ref, fin_ref, carry_ref, *, gc, rev):
    s = pl.program_id(1)

    @pl.when(s < gc)
    def _():
        carry_ref[...] = jnp.zeros_like(carry_ref)

    @pl.when(s == gc)
    def _():
        carry_ref[...] = h0_ref[...]

    def body(i, h):
        t = (TB - 1 - i) if rev else i
        r = pl.multiple_of(t * SUBLANES, SUBLANES)
        h = a_ref[0, pl.ds(r, SUBLANES), :] * h + b_ref[0, pl.ds(r, SUBLANES), :]
        h_ref[pl.ds(r, SUBLANES), :] = h
        return h

    h = lax.fori_loop(0, TB, body, carry_ref[...], unroll=8)
    carry_ref[...] = h

    @pl.when(s < gc)
    def _():
        fin_ref[0] = h


def _lru_scan(a, b, h0, *, d, gc, nd, rev):
    rows = a.shape[1]
    nblk = gc + nd
    wc = 512
    blk = lambda c, s: _step_blk(s, gc, nd, rev)
    return pl.pallas_call(
        functools.partial(_lru_scan_kernel, gc=gc, rev=rev),
        grid=(D_A // wc, nblk),
        in_specs=[
            pl.BlockSpec((1, RB, wc), lambda c, s: (d, blk(c, s), c)),
            pl.BlockSpec((1, RB, wc), lambda c, s: (d, blk(c, s), c)),
            pl.BlockSpec((SUBLANES, wc), lambda c, s: (0, c)),
        ],
        out_specs=[
            pl.BlockSpec((RB, wc), lambda c, s: (blk(c, s), c)),
            pl.BlockSpec((1, SUBLANES, wc), lambda c, s: (jnp.minimum(s, gc - 1), 0, c)),
        ],
        out_shape=[
            jax.ShapeDtypeStruct((rows, D_A), f32),
            jax.ShapeDtypeStruct((gc, SUBLANES, D_A), f32),
        ],
        scratch_shapes=[pltpu.VMEM((SUBLANES, wc), f32)],
        compiler_params=_cparams(("parallel", "arbitrary")),
        name="lru_scan_bwd" if rev else "lru_scan_fwd",
    )(a, b, h0)


def _delta_kernel(q_ref, k_ref, v_ref, ba_ref, cw_ref, nega_ref, dtb_ref, s0_ref, o_ref, fin_ref,
                  qs_ref, ks_ref, vs_ref, bs_ref, la_ref, gs_ref, st_ref, *, gc, rev, d):
    h = pl.program_id(0)
    s = pl.program_id(1)
    is_ctx = s < gc

    @pl.when(is_ctx)
    def _():
        st_ref[...] = jnp.zeros_like(st_ref)

    @pl.when(s == gc)
    def _():
        st_ref[...] = s0_ref[0]

    def l2n(x):
        return x * lax.rsqrt(jnp.sum(x * x, axis=-1, keepdims=True) + EPS)

    qs_ref[...] = l2n(_silu(_conv4(q_ref[0], cw_ref[0, 0], is_ctx))) * (DK_B ** -0.5)
    ks_ref[...] = l2n(_silu(_conv4(k_ref[0], cw_ref[1, 0], is_ctx)))
    vs_ref[...] = _silu(_conv4(v_ref[0], cw_ref[2, 0], is_ctx))
    ba = ba_ref[0]
    bs_ref[...] = _pick_lane(_sigmoid(ba), d * H_B + h)
    la_ref[...] = _pick_lane(nega_ref[...] * jax.nn.softplus(ba + dtb_ref[...]), 2 * H_B + d * H_B + h)
    _tile_cumsum(la_ref, gs_ref, rev)

    ii = lax.broadcasted_iota(jnp.int32, (CHUNK, CHUNK), 0)
    jj = lax.broadcasted_iota(jnp.int32, (CHUNK, CHUNK), 1)
    incl = (jj >= ii) if rev else (jj <= ii)
    strict = (jj > ii) if rev else (jj < ii)
    eye = (ii == jj).astype(f32)
    last = 0 if rev else CHUNK - 1
    diag8 = (ii // 8) == (jj // 8)
    merge_masks = [((ii // (2 * sz)) == (jj // (2 * sz))) & ((ii // sz) != (jj // sz)) for sz in (8, 16, 32)]

    nch = TB // CHUNK
    order = list(range(nch - 1, -1, -1)) if rev else list(range(nch))
    rows = [pl.ds(c * CROWS + b, CHUNK, stride=SUBLANES) for c in order for b in range(SUBLANES)]
    units = range(len(rows))
    if True:
        q = [qs_ref[r, :] for r in rows]
        k = [ks_ref[r, :] for r in rows]
        v = [vs_ref[r, :] for r in rows]
        beta = [bs_ref[r, :] for r in rows]
        g = [gs_ref[r, :] for r in rows]
        kb = [k[b] * beta[b] for b in units]
        dec_incl, low = [], []
        for b in units:
            gsq = g[b][:, :CHUNK]
            g_row = jnp.sum(gsq * eye, axis=0, keepdims=True)
            diff = gsq - g_row
            dec = jnp.where(incl, jnp.exp(jnp.where(incl, diff, 0.0)), 0.0)
            dec_incl.append(dec)
            low.append(_mm_nt(kb[b], k[b]) * jnp.where(strict, dec, 0.0))
        ld = [jnp.where(diag8, low[b], 0.0) for b in units]
        ls = [_split(ld[b]) for b in units]
        d2 = [_mm3(ls[b], ls[b]) for b in units]
        d2s = [_split(d2[b]) for b in units]
        t_inv = [eye - ld[b] for b in units]
        ts = [_split(t_inv[b]) for b in units]
        prod, d4 = [], []
        for b in units:
            prod.append(_mm3(ts[b], d2s[b]))
            d4.append(_mm3(d2s[b], d2s[b]))
        t_inv = [t_inv[b] + prod[b] for b in units]
        t_inv = [t_inv[b] + _mm3(_split(t_inv[b]), _split(d4[b])) for b in units]
        for lvl in merge_masks:
            ts = [_split(t_inv[b]) for b in units]
            tl = [_mm3(ts[b], _split(jnp.where(lvl, low[b], 0.0))) for b in units]
            t_inv = [t_inv[b] - _mm3(_split(tl[b]), ts[b]) for b in units]
        eg = [jnp.exp(g[b]) for b in units]
        ts = [_split(t_inv[b]) for b in units]
        uw = [_mm3(ts[b], _split(jnp.concatenate([v[b] * beta[b], kb[b] * eg[b]], axis=1))) for b in units]
        a_qk = [_mm_nt(q[b], k[b]) * dec_incl[b] for b in units]
        g_last = [g[b][last:last + 1, :] for b in units]
        for ci in range(nch):
            seqs = range(SUBLANES)
            us = [ci * SUBLANES + b for b in seqs]
            st = [st_ref[b] for b in seqs]
            v_new = [uw[u][:, :DV_B] - _mm(uw[u][:, DV_B:], st[b]) for b, u in zip(seqs, us)]
            o = [_mm(q[u] * eg[u], st[b]) + _mm(a_qk[u], v_new[b]) for b, u in zip(seqs, us)]
            for b, u in zip(seqs, us):
                k_g = k[u] * jnp.exp(g_last[u] - g[u])
                st_ref[b] = st[b] * jnp.exp(g_last[u]) + _mm_tn(k_g, v_new[b])
                o_ref[0, rows[u], :] = o[b]

    @pl.when(is_ctx)
    def _():
        fin_ref[0, 0] = st_ref[...]


def _delta(z, conv_qkv, nega, dtb, s0, *, d, gc, nd, rev):
    rows = z.shape[1]
    nblk = gc + nd
    blk = lambda h, s: _step_blk(s, gc, nd, rev)
    slab = lambda off: pl.BlockSpec((1, RB, LANES), lambda h, s: (off + h, blk(h, s), 0))
    scr = pltpu.VMEM((RB, LANES), f32)
    return pl.pallas_call(
        functools.partial(_delta_kernel, gc=gc, rev=rev, d=d),
        grid=(H_B, nblk),
        in_specs=[
            slab(E_Q), slab(E_K), slab(E_V),
            pl.BlockSpec((1, RB, LANES), lambda h, s: (E_BA, blk(h, s), 0)),
            pl.BlockSpec((3, 1, 4, LANES), lambda h, s: (0, h, 0, 0)),
            pl.BlockSpec((1, LANES), lambda h, s: (0, 0)),
            pl.BlockSpec((1, LANES), lambda h, s: (0, 0)),
            pl.BlockSpec((1, SUBLANES, DK_B, DV_B), lambda h, s: (h, 0, 0, 0)),
        ],
        out_specs=[
            pl.BlockSpec((1, RB, LANES), lambda h, s: (h, blk(h, s), 0)),
            pl.BlockSpec((1, 1, SUBLANES, DK_B, DV_B), lambda h, s: (jnp.minimum(s, gc - 1), h, 0, 0, 0)),
        ],
        out_shape=[
            jax.ShapeDtypeStruct((H_B, rows, LANES), f32),
            jax.ShapeDtypeStruct((gc, H_B, SUBLANES, DK_B, DV_B), f32),
        ],
        scratch_shapes=[scr, scr, scr, scr, scr, scr, pltpu.VMEM((SUBLANES, DK_B, DV_B), f32)],
        compiler_params=_cparams(("parallel", "arbitrary")),
        name="delta_bwd" if rev else "delta_fwd",
    )(z, z, z, z, conv_qkv, nega, dtb, s0)


def _s5_kernel(u_ref, bt_ref, ct_ref, lam_ref, s0_ref, y_ref, fin_ref, sbuf_ref, carry_ref, *, gc, rev):
    s = pl.program_id(1)
    half = sbuf_ref.shape[1] // 2

    @pl.when(s < gc)
    def _():
        carry_ref[...] = jnp.zeros_like(carry_ref)

    @pl.when(s == gc)
    def _():
        carry_ref[...] = s0_ref[0]

    nsl = u_ref.shape[0]
    sbuf_ref[...] = _mm(jnp.concatenate([u_ref[j] for j in range(nsl)], axis=1), bt_ref[0])
    lr = jnp.broadcast_to(lam_ref[0, 0:1, :], (SUBLANES, half))
    li = jnp.broadcast_to(lam_ref[0, 1:2, :], (SUBLANES, half))

    def body(i, hc):
        hr, hi = hc
        t = (TB - 1 - i) if rev else i
        r = pl.multiple_of(t * SUBLANES, SUBLANES)
        nr = lr * hr - li * hi + sbuf_ref[pl.ds(r, SUBLANES), :half]
        ni = lr * hi + li * hr + sbuf_ref[pl.ds(r, SUBLANES), half:]
        sbuf_ref[pl.ds(r, SUBLANES), :half] = nr
        sbuf_ref[pl.ds(r, SUBLANES), half:] = ni
        return nr, ni

    hr, hi = lax.fori_loop(0, TB, body, (carry_ref[:, :half], carry_ref[:, half:]), unroll=4)
    carry_ref[:, :half] = hr
    carry_ref[:, half:] = hi
    y = _mm(sbuf_ref[...], ct_ref[0])
    for j in range(nsl):
        y_ref[j] = y[:, j * LANES:(j + 1) * LANES]

    @pl.when(s < gc)
    def _():
        fin_ref[0, 0] = carry_ref[...]


S5_COLS = LANES
S5_GPC = S5_COLS // S5_GW
S5_SW = 2 * S5_GPC * S5_N


def _s5(z, bt, ct, lam, s0, *, gc, nd, rev):
    rows = z.shape[1]
    nblk = gc + nd
    ngt = D_C // S5_COLS
    nsl = S5_COLS // LANES
    sw = S5_SW
    blk = lambda g, s: _step_blk(s, gc, nd, rev)
    return pl.pallas_call(
        functools.partial(_s5_kernel, gc=gc, rev=rev),
        grid=(ngt, nblk),
        in_specs=[
            pl.BlockSpec((nsl, RB, LANES), lambda g, s: (O_U // nsl + g, blk(g, s), 0)),
            pl.BlockSpec((1, S5_COLS, sw), lambda g, s: (g, 0, 0)),
            pl.BlockSpec((1, sw, S5_COLS), lambda g, s: (g, 0, 0)),
            pl.BlockSpec((1, 2, sw // 2), lambda g, s: (g, 0, 0)),
            pl.BlockSpec((1, SUBLANES, sw), lambda g, s: (g, 0, 0)),
        ],
        out_specs=[
            pl.BlockSpec((nsl, RB, LANES), lambda g, s: (g, blk(g, s), 0)),
            pl.BlockSpec((1, 1, SUBLANES, sw), lambda g, s: (jnp.minimum(s, gc - 1), g, 0, 0)),
        ],
        out_shape=[
            jax.ShapeDtypeStruct((D_C // LANES, rows, LANES), f32),
            jax.ShapeDtypeStruct((gc, ngt, SUBLANES, sw), f32),
        ],
        scratch_shapes=[pltpu.VMEM((RB, sw), f32), pltpu.VMEM((SUBLANES, sw), f32)],
        compiler_params=_cparams(("parallel", "arbitrary")),
        name="s5_bwd" if rev else "s5_fwd",
    )(z, bt, ct, lam, s0)


def _s5_out_kernel(yf_ref, yb_ref, u_ref, d_ref, w_ref, b_ref, o_ref):
    ns = yf_ref.shape[0]
    yc = jnp.concatenate([yf_ref[s] + yb_ref[s] + d_ref[s] * u_ref[s] for s in range(ns)], axis=1)
    zc = jax.nn.gelu(yc)
    gate = _sigmoid(jnp.dot(zc.astype(bf16), w_ref[...], preferred_element_type=f32) + b_ref[...])
    o_ref[...] = (zc * gate).astype(o_ref.dtype)


def _s5_out(yf, yb, z, dskip, w_glu, b_glu, *, tm):
    ns, rows, _ = yf.shape
    return pl.pallas_call(
        _s5_out_kernel,
        grid=(rows // tm,),
        in_specs=[
            pl.BlockSpec((ns, tm, LANES), lambda i: (0, i, 0)),
            pl.BlockSpec((ns, tm, LANES), lambda i: (0, i, 0)),
            pl.BlockSpec((ns, tm, LANES), lambda i: (O_U // ns, i, 0)),
            pl.BlockSpec((ns, 1, LANES), lambda i: (0, 0, 0)),
            pl.BlockSpec((D_C, D_C), lambda i: (0, 0)),
            pl.BlockSpec((1, D_C), lambda i: (0, 0)),
        ],
        out_specs=pl.BlockSpec((tm, D_C), lambda i: (i, 0)),
        out_shape=jax.ShapeDtypeStruct((rows, D_C), bf16),
        compiler_params=_cparams(("parallel",)),
        name="s5_out",
    )(yf, yb, z, dskip, w_glu, b_glu)


def _gla_kernel(q_ref, k_ref, v_ref, glr_ref, wg_ref, bg_ref, s0_ref, o_ref, fin_ref,
                la_ref, gs_ref, st_ref, *, gc, rev):
    s = pl.program_id(1)
    is_ctx = s < gc

    @pl.when(is_ctx)
    def _():
        st_ref[...] = jnp.zeros_like(st_ref)

    @pl.when(s == gc)
    def _():
        st_ref[...] = s0_ref[0]

    la_ref[...] = jax.nn.log_sigmoid(_mm(glr_ref[0], wg_ref[0]) + bg_ref[0]) * (1.0 / GLA_TAU)
    _tile_cumsum(la_ref, gs_ref, rev)

    ii = lax.broadcasted_iota(jnp.int32, (SUB, CHUNK), 0)
    jj = lax.broadcasted_iota(jnp.int32, (SUB, CHUNK), 1)
    last = 0 if rev else CHUNK - 1

    def chunk_body(ci, carry):
        c = (TB // CHUNK - 1 - ci) if rev else ci
        for b in range(SUBLANES):
            rows = pl.ds(c * CROWS + b, CHUNK, stride=SUBLANES)
            q = q_ref[0, rows, :] * (DK_D ** -0.5)
            k = k_ref[0, rows, :]
            v = jnp.concatenate([v_ref[0, rows, :], v_ref[1, rows, :]], axis=1)
            g = gs_ref[rows, :]
            a_rows = []
            for blk in range(CHUNK // SUB):
                lo = blk * SUB
                first = lo + SUB - 1 if rev else lo
                c_ref = g[first:first + 1, :]
                qt = q[lo:lo + SUB, :] * jnp.exp(g[lo:lo + SUB, :] - c_ref)
                kt = k * jnp.exp(c_ref - g)
                a_blk = _mm_nt(qt, kt)
                causal = (jj >= ii + lo) if rev else (jj <= ii + lo)
                a_rows.append(jnp.where(causal, a_blk, 0.0))
            a_qk = jnp.concatenate(a_rows, axis=0)
            g_last = g[last:last + 1, :]
            q_g = q * jnp.exp(g)
            k_g = k * jnp.exp(g_last - g)
            st = st_ref[b]
            o = _mm_nt(q_g, st) + _mm(a_qk, v)
            st_ref[b] = st * jnp.exp(g_last) + _mm_tn(v, k_g)
            o_ref[0, rows, :] = o[:, :LANES]
            o_ref[1, rows, :] = o[:, LANES:]
        return carry

    lax.fori_loop(0, TB // CHUNK, chunk_body, 0)

    @pl.when(is_ctx)
    def _():
        fin_ref[0, 0] = st_ref[...]


def _gla(z, wg, bg, s0, *, gc, nd, rev):
    rows = z.shape[1]
    nblk = gc + nd
    nv = DV_D // LANES
    blk = lambda h, s: _step_blk(s, gc, nd, rev)
    scr = pltpu.VMEM((RB, LANES), f32)
    return pl.pallas_call(
        functools.partial(_gla_kernel, gc=gc, rev=rev),
        grid=(H_D, nblk),
        in_specs=[
            pl.BlockSpec((1, RB, LANES), lambda h, s: (O_Q + h, blk(h, s), 0)),
            pl.BlockSpec((1, RB, LANES), lambda h, s: (O_K + h, blk(h, s), 0)),
            pl.BlockSpec((nv, RB, LANES), lambda h, s: (O_V // nv + h, blk(h, s), 0)),
            pl.BlockSpec((1, RB, LANES), lambda h, s: (O_GLR, blk(h, s), 0)),
            pl.BlockSpec((1, LANES, LANES), lambda h, s: (h, 0, 0)),
            pl.BlockSpec((1, 1, LANES), lambda h, s: (h, 0, 0)),
            pl.BlockSpec((1, SUBLANES, DV_D, DK_D), lambda h, s: (h, 0, 0, 0)),
        ],
        out_specs=[
            pl.BlockSpec((nv, RB, LANES), lambda h, s: (h, blk(h, s), 0)),
            pl.BlockSpec((1, 1, SUBLANES, DV_D, DK_D), lambda h, s: (jnp.minimum(s, gc - 1), h, 0, 0, 0)),
        ],
        out_shape=[
            jax.ShapeDtypeStruct((H_D * nv, rows, LANES), f32),
            jax.ShapeDtypeStruct((gc, H_D, SUBLANES, DV_D, DK_D), f32),
        ],
        scratch_shapes=[scr, scr, pltpu.VMEM((SUBLANES, DV_D, DK_D), f32)],
        compiler_params=_cparams(("parallel", "arbitrary")),
        name="gla_bwd" if rev else "gla_fwd",
    )(z, z, z, z, wg, bg, s0)


def _to_rows(x):
    b, t, d = x.shape
    return x.reshape(b // SUBLANES, SUBLANES, t, d).transpose(0, 2, 1, 3).reshape(b * t, d)


def _from_rows(r, b, t):
    d = r.shape[-1]
    return r.reshape(b // SUBLANES, t, SUBLANES, d).transpose(0, 2, 1, 3).reshape(b, t, d)


def _pad_cols(w, n):
    return jnp.pad(w, ((0, 0), (0, n - w.shape[1])))


def _unit_states(fin, b):
    fin = jnp.moveaxis(fin, b, 1)
    return fin.reshape((fin.shape[0] * SUBLANES,) + fin.shape[2:])


def kernel(x_prompt, x_sample, state_lru, state_delta, state_s5_re, state_s5_im, state_gla, c, c_ctx, w_ada, b_ada, norm_g, w_up, w_down, w_in_e, w_out_e, conv_a, lru_wa, lru_ba, lru_wx, lru_bx, lru_lambda, conv_qkv, dn_a_log, dn_dt_bias, dn_norm_g, w_in_o, w_out_o, s5_lam_re, s5_lam_im, s5_log_step, s5_b_re, s5_b_im, s5_c_re, s5_c_im, s5_d, s5_w_glu, s5_b_glu, gla_wg2, gla_bg, gla_norm_g):
    bp, tp, d = x_prompt.shape
    bs, ts, _ = x_sample.shape
    assert tp == TB and bp % SUBLANES == 0 and bs == SUBLANES and ts % TB == 0 and d == D_MODEL
    depth = w_ada.shape[0]
    gc = bp // SUBLANES
    nd = ts // TB
    nblk = gc + nd
    rows_ctx = bp * tp
    seq = dict(gc=gc, nd=nd)

    x = jnp.concatenate([_to_rows(x_prompt), _to_rows(x_sample)], axis=0)

    cvec = jnp.concatenate([c, jnp.broadcast_to(c_ctx[None], (SUBLANES, d))], axis=0)
    mod = _ada(cvec, w_ada, b_ada)
    mod = jnp.stack([mod[:, SUBLANES:], mod[:, :SUBLANES]], axis=1)
    mods = [[mod[l, :, :, i * d:(i + 1) * d] for i in range(6)] for l in range(depth)]

    fin_lru, fin_delta, fin_re, fin_im, fin_gla = [], [], [], [], []
    for l in range(depth):
        sh1, sc1, gt1, sh2, sc2, gt2 = mods[l]
        if l % 2 == 0:
            e = l // 2
            w_in = _pad_cols(w_in_e[e], E_SLABS * LANES).astype(bf16)
            z = _nmm(x, norm_g[l, 0], sh1, sc1, w_in, rows_ctx=rows_ctx, tm=1024, tn=10 * LANES, slab=True)
            ns = D_A // LANES
            wg = jnp.stack([lru_wa[e], lru_wx[e]], axis=1).reshape(2, 2, ns, 2, LRU_BW, LRU_BW)
            wbd = jnp.zeros((2, 2, ns, LANES, LANES), f32)
            wbd = wbd.at[:, :, :, :LRU_BW, :LRU_BW].set(wg[:, :, :, 0]).at[:, :, :, LRU_BW:, LRU_BW:].set(wg[:, :, :, 1])
            bias = jnp.stack([lru_ba[e], lru_bx[e]], axis=1).reshape(2, 2, ns, 1, LANES)
            sp = jax.nn.softplus(-lru_lambda[e]).reshape(2, ns, 1, LANES)
            cwa = conv_a[e].reshape(4, ns, LANES).transpose(1, 0, 2)
            a, b = _lru_gates(z, cwa, wbd.astype(bf16), bias, sp, gc=gc, nblk=nblk)
            h_f, f_f = _lru_scan(a, b, state_lru[:, e, 0], d=0, rev=False, **seq)
            h_b, f_b = _lru_scan(a, b, state_lru[:, e, 1], d=1, rev=True, **seq)
            fin_lru.append(jnp.stack([_unit_states(f_f, 1), _unit_states(f_b, 1)], axis=1))
            cq = conv_qkv[e].reshape(4, 3, H_B, LANES).transpose(1, 2, 0, 3)
            lane_pad = lambda v: jnp.pad(v.reshape(1, 2 * H_B), ((0, 0), (2 * H_B, LANES - 4 * H_B)))
            nega = lane_pad(-jnp.exp(dn_a_log[e]))
            dtb = lane_pad(dn_dt_bias[e])
            s0 = state_delta[:, e].transpose(1, 2, 0, 3, 4)
            o_f, s_f = _delta(z, cq, nega, dtb, s0[0], d=0, rev=False, **seq)
            o_b, s_b = _delta(z, cq, nega, dtb, s0[1], d=1, rev=True, **seq)
            fin_delta.append(jnp.stack([_unit_states(s_f, 2), _unit_states(s_b, 2)], axis=1))
            x = _mixout_even(h_f, h_b, o_f, o_b, z, dn_norm_g[e], w_out_e[e].astype(bf16), x, norm_g[l, 1], gt1,
                             rows_ctx=rows_ctx, tm=512)
        else:
            o = l // 2
            w_in = _pad_cols(w_in_o[o], O_SLABS * LANES).astype(bf16)
            z = _nmm(x, norm_g[l, 0], sh1, sc1, w_in, rows_ctx=rows_ctx, tm=1024, tn=11 * LANES, slab=True)
            ngt = D_C // S5_COLS
            gps = S5_GPC
            lam = lax.complex(s5_lam_re[o], s5_lam_im[o])
            lam_bar = jnp.exp(lam * jnp.exp(s5_log_step[o])[..., None])
            b_bar = ((lam_bar - 1.0) / lam)[..., None] * lax.complex(s5_b_re[o], s5_b_im[o])
            eye_g = jnp.eye(gps, dtype=f32)

            def bmat(p):
                p = p.reshape(2, ngt, gps, S5_N, S5_GW)
                return jnp.einsum('dtgnw,gh->dtgwhn', p, eye_g).reshape(2, ngt, S5_COLS, gps * S5_N)

            def cmat(p):
                p = p.reshape(2, ngt, gps, S5_GW, S5_N)
                return jnp.einsum('dtgwn,gh->dtgnhw', p, eye_g).reshape(2, ngt, gps * S5_N, S5_COLS)

            bt = jnp.concatenate([bmat(b_bar.real), bmat(b_bar.imag)], axis=-1).astype(bf16)
            ct = jnp.concatenate([cmat(s5_c_re[o]), -cmat(s5_c_im[o])], axis=-2).astype(bf16)
            lamv = jnp.stack([lam_bar.real, lam_bar.imag], axis=2).reshape(2, ngt, gps, 2, S5_N)
            lamv = lamv.transpose(0, 1, 3, 2, 4).reshape(2, ngt, 2, gps * S5_N)

            def s5_state(sre, sim):
                f = lambda p: p.reshape(SUBLANES, ngt, gps * S5_N).transpose(1, 0, 2)
                return jnp.concatenate([f(sre), f(sim)], axis=-1)

            y_f, c_f = _s5(z, bt[0], ct[0], lamv[0], s5_state(state_s5_re[:, o, 0], state_s5_im[:, o, 0]), rev=False, **seq)
            y_bk, c_b = _s5(z, bt[1], ct[1], lamv[1], s5_state(state_s5_re[:, o, 1], state_s5_im[:, o, 1]), rev=True, **seq)
            y_c = _s5_out(y_f, y_bk, z, s5_d[o].reshape(D_C // LANES, 1, LANES), s5_w_glu[o].astype(bf16),
                          s5_b_glu[o].reshape(1, D_C), tm=512)

            def s5_fin(cf):
                cf = cf.transpose(0, 2, 1, 3).reshape(gc * SUBLANES, ngt, 2, gps, S5_N)
                return (cf[:, :, 0].reshape(gc * SUBLANES, S5_G, S5_N), cf[:, :, 1].reshape(gc * SUBLANES, S5_G, S5_N))

            (rf, imf), (rb, imb) = s5_fin(c_f), s5_fin(c_b)
            fin_re.append(jnp.stack([rf, rb], axis=1))
            fin_im.append(jnp.stack([imf, imb], axis=1))
            wgp = jnp.zeros((2, H_D, LANES, LANES), f32)
            wg2 = gla_wg2[o].reshape(2, GLA_RANK, H_D, DK_D).transpose(0, 2, 1, 3)
            wgp = wgp.at[0, :, :GLA_RANK].set(wg2[0]).at[1, :, GLA_RANK:2 * GLA_RANK].set(wg2[1]).astype(bf16)
            bgp = gla_bg[o].reshape(2, H_D, 1, DK_D)
            g0 = state_gla[:, o].transpose(1, 2, 0, 4, 3)
            o_f, g_f = _gla(z, wgp[0], bgp[0], g0[0], rev=False, **seq)
            o_b, g_b = _gla(z, wgp[1], bgp[1], g0[1], rev=True, **seq)
            gfin = lambda gf: jnp.swapaxes(_unit_states(gf, 2), -1, -2)
            fin_gla.append(jnp.stack([gfin(g_f), gfin(g_b)], axis=1))
            x = _mixout_odd(y_c, o_f, o_b, z, gla_norm_g[o], w_out_o[o].astype(bf16), x, norm_g[l, 1], gt1,
                            rows_ctx=rows_ctx, tm=512)

        hmid = _nmm(x, norm_g[l, 2], sh2, sc2, w_up[l].astype(bf16), rows_ctx=rows_ctx, tm=1024, tn=1024,
                    relu2=True, out_dtype=bf16)
        x = _mmres(hmid, w_down[l].astype(bf16), x, norm_g[l, 3], gt2, rows_ctx=rows_ctx, tm=512, tk=2048)

    y_prompt = _from_rows(x[:rows_ctx], bp, tp)
    y_sample = _from_rows(x[rows_ctx:], bs, ts)
    return (y_prompt, y_sample, jnp.stack(fin_lru, axis=1), jnp.stack(fin_delta, axis=1),
            jnp.stack(fin_re, axis=1), jnp.stack(fin_im, axis=1), jnp.stack(fin_gla, axis=1))
```

```python
import functools
import math

import jax
import jax.numpy as jnp
from jax import lax
from jax.experimental import pallas as pl
from jax.experimental.pallas import tpu as pltpu

f32 = jnp.float32
bf16 = jnp.bfloat16

LANES = 128
SUBLANES = 8
VMEM_LIMIT_BYTES = 56 * 1024 * 1024

D_MODEL = 2048
D_FF = 4 * D_MODEL
GRID_W = 64
CHUNK = 64
EPS = 1e-6
D_A = D_MODEL // 2
LRU_BLOCKS = 16
LRU_BW = D_A // LRU_BLOCKS
LRU_C = 8.0
H_B = 8
DK_B = 128
DV_B = 128
D_C = D_MODEL // 2
S5_GW = 16
S5_G = D_C // S5_GW
S5_N = 64
H_D = 4
DK_D = 128
DV_D = 256
GLA_RANK = 16
GLA_TAU = 16.0

TB = 256
RB = TB * SUBLANES
CROWS = CHUNK * SUBLANES
SUB = 16

E_XA, E_GA, E_Q, E_K, E_V, E_GO, E_BA = 0, 8, 16, 24, 32, 40, 48
E_SLABS = 50
O_U, O_Q, O_K, O_V, O_R, O_GLR = 0, 8, 12, 16, 24, 32
O_SLABS = 33


def _cparams(sem):
    return pltpu.CompilerParams(dimension_semantics=sem, vmem_limit_bytes=VMEM_LIMIT_BYTES)


def _mm(a, b):
    return jnp.dot(a.astype(bf16), b.astype(bf16), preferred_element_type=f32)


def _mm_nt(a, b):
    return lax.dot_general(a.astype(bf16), b.astype(bf16), (((1,), (1,)), ((), ())), preferred_element_type=f32)


def _mm_tn(a, b):
    return lax.dot_general(a.astype(bf16), b.astype(bf16), (((0,), (0,)), ((), ())), preferred_element_type=f32)


def _split(x):
    hi = x.astype(bf16)
    return hi, (x - hi.astype(f32)).astype(bf16)


def _mm3(a, b):
    dot = functools.partial(jnp.dot, preferred_element_type=f32)
    return dot(a[0], b[0]) + (dot(a[0], b[1]) + dot(a[1], b[0]))


def _sigmoid(x):
    return 0.5 * jnp.tanh(0.5 * x) + 0.5


def _silu(x):
    t = 0.5 * x
    return t + t * jnp.tanh(t)


def _ada_kernel(c_ref, w_ref, b_ref, o_ref):
    a = _silu(c_ref[...])
    o_ref[0] = _mm(a, w_ref[0]) + b_ref[0]


def _ada(cvec, w_ada, b_ada):
    depth, d, n = w_ada.shape
    tn = 1024
    return pl.pallas_call(
        _ada_kernel,
        grid=(depth, n // tn),
        in_specs=[
            pl.BlockSpec(cvec.shape, lambda l, j: (0, 0)),
            pl.BlockSpec((1, d, tn), lambda l, j: (l, 0, j)),
            pl.BlockSpec((1, 1, tn), lambda l, j: (l, 0, j)),
        ],
        out_specs=pl.BlockSpec((1, cvec.shape[0], tn), lambda l, j: (l, 0, j)),
        out_shape=jax.ShapeDtypeStruct((depth, cvec.shape[0], n), f32),
        compiler_params=_cparams(("parallel", "parallel")),
        name="ada",
    )(cvec, w_ada, b_ada.reshape(depth, 1, n))


def _nmm_kernel(x_ref, g_ref, sh_ref, sc_ref, w_ref, o_ref, *, relu2, slab, tn):
    x = x_ref[...]
    tm, d = x.shape
    y = x * lax.rsqrt(jnp.mean(x * x, axis=-1, keepdims=True) + EPS)
    gain = g_ref[...] * (1.0 + sc_ref[0])
    h = (y.reshape(tm // SUBLANES, SUBLANES, d) * gain[None] + sh_ref[0][None]).reshape(tm, d).astype(bf16)
    for c in range(w_ref.shape[1] // tn):
        acc = jnp.dot(h, w_ref[:, c * tn:(c + 1) * tn], preferred_element_type=f32)
        if relu2:
            acc = jnp.square(jnp.maximum(acc, 0.0))
        if slab:
            for s in range(tn // LANES):
                o_ref[c * (tn // LANES) + s] = acc[:, s * LANES:(s + 1) * LANES].astype(o_ref.dtype)
        else:
            o_ref[:, c * tn:(c + 1) * tn] = acc.astype(o_ref.dtype)


def _nmm(x, g, sh, sc, w, *, rows_ctx, tm, tn, relu2=False, slab=False, out_dtype=f32):
    rows, d = x.shape
    n = w.shape[1]
    assert n % tn == 0
    grp = lambda i: ((i * tm >= rows_ctx).astype(jnp.int32), 0, 0)
    if slab:
        out_shape = jax.ShapeDtypeStruct((n // LANES, rows, LANES), out_dtype)
        out_spec = pl.BlockSpec((n // LANES, tm, LANES), lambda i: (0, i, 0))
    else:
        out_shape = jax.ShapeDtypeStruct((rows, n), out_dtype)
        out_spec = pl.BlockSpec((tm, n), lambda i: (i, 0))
    return pl.pallas_call(
        functools.partial(_nmm_kernel, relu2=relu2, slab=slab, tn=tn),
        grid=(rows // tm,),
        in_specs=[
            pl.BlockSpec((tm, d), lambda i: (i, 0)),
            pl.BlockSpec((1, d), lambda i: (0, 0)),
            pl.BlockSpec((1, SUBLANES, d), grp),
            pl.BlockSpec((1, SUBLANES, d), grp),
            pl.BlockSpec((d, n), lambda i: (0, 0), pipeline_mode=pl.Buffered(1)),
        ],
        out_specs=out_spec,
        out_shape=out_shape,
        compiler_params=_cparams(("parallel",)),
        name="nmm",
    )(x, g.reshape(1, d), sh, sc, w)


def _mmres_kernel(a_ref, w_ref, x_ref, g_ref, gt_ref, o_ref):
    y = jnp.dot(a_ref[...], w_ref[...], preferred_element_type=f32)
    _gated_residual(y, x_ref, g_ref, gt_ref, o_ref)


def _gated_residual(y, x_ref, g_ref, gt_ref, o_ref):
    tm, d = y.shape
    yn = y * lax.rsqrt(jnp.mean(y * y, axis=-1, keepdims=True) + EPS) * g_ref[...]
    o = x_ref[...].reshape(tm // SUBLANES, SUBLANES, d) + gt_ref[0][None] * yn.reshape(tm // SUBLANES, SUBLANES, d)
    o_ref[...] = o.reshape(tm, d)


def _headnorm_gate(o_slabs, gain, gate_slabs):
    width = len(o_slabs) * LANES
    ms = sum(jnp.sum(o * o, axis=-1, keepdims=True) for o in o_slabs) * (1.0 / width)
    inv = lax.rsqrt(ms + EPS)
    return [(o * inv * gain[j] * _silu(r)).astype(bf16) for j, (o, r) in enumerate(zip(o_slabs, gate_slabs))]


def _mixout_even_kernel(hf_ref, hb_ref, ga_ref, of_ref, ob_ref, go_ref, dg_ref, w1_ref, w2_ref,
                        x_ref, g_ref, gt_ref, o_ref):
    ns = ga_ref.shape[0]
    ga = jnp.concatenate([ga_ref[s] for s in range(ns)], axis=1)
    y_a = ((hf_ref[...] + hb_ref[...]) * jax.nn.gelu(ga)).astype(bf16)
    y = jnp.dot(y_a, w1_ref[...], preferred_element_type=f32)
    y_b = [_headnorm_gate([of_ref[h] + ob_ref[h]], [dg_ref[...]], [go_ref[h]])[0] for h in range(of_ref.shape[0])]
    y = y + jnp.dot(jnp.concatenate(y_b, axis=1), w2_ref[...], preferred_element_type=f32)
    _gated_residual(y, x_ref, g_ref, gt_ref, o_ref)


def _mixout_odd_kernel(yc_ref, of_ref, ob_ref, r_ref, dg_ref, w1_ref, w2_ref, x_ref, g_ref, gt_ref, o_ref):
    nv = DV_D // LANES
    y = jnp.dot(yc_ref[...], w1_ref[...], preferred_element_type=f32)
    y_d = []
    for h in range(of_ref.shape[0] // nv):
        sl = range(h * nv, (h + 1) * nv)
        y_d += _headnorm_gate([of_ref[j] + ob_ref[j] for j in sl], [dg_ref[j] for j in range(nv)], [r_ref[j] for j in sl])
    y = y + jnp.dot(jnp.concatenate(y_d, axis=1), w2_ref[...], preferred_element_type=f32)
    _gated_residual(y, x_ref, g_ref, gt_ref, o_ref)


def _mixout_specs(w, x, g, gt, rows_ctx, tm):
    kh = w.shape[0] // 2
    d = w.shape[1]
    grp = lambda i: ((i * tm >= rows_ctx).astype(jnp.int32), 0, 0)
    specs = [
        pl.BlockSpec((kh, d), lambda i: (0, 0), pipeline_mode=pl.Buffered(1)),
        pl.BlockSpec((kh, d), lambda i: (1, 0), pipeline_mode=pl.Buffered(1)),
        pl.BlockSpec((tm, d), lambda i: (i, 0)),
        pl.BlockSpec((1, d), lambda i: (0, 0)),
        pl.BlockSpec((1, SUBLANES, d), grp),
    ]
    return specs, (w, w, x, g.reshape(1, d), gt)


def _mixout_even(hf, hb, of, ob, z, dn_g, w, x, g, gt, *, rows_ctx, tm):
    rows, d = x.shape
    ns = D_A // LANES
    slabs = lambda off: pl.BlockSpec((ns, tm, LANES), lambda i: (off // ns, i, 0))
    tail_specs, tail_args = _mixout_specs(w, x, g, gt, rows_ctx, tm)
    return pl.pallas_call(
        _mixout_even_kernel,
        grid=(rows // tm,),
        in_specs=[
            pl.BlockSpec((tm, D_A), lambda i: (i, 0)),
            pl.BlockSpec((tm, D_A), lambda i: (i, 0)),
            slabs(E_GA), slabs(0), slabs(0), slabs(E_GO),
            pl.BlockSpec((1, LANES), lambda i: (0, 0)),
        ] + tail_specs,
        out_specs=pl.BlockSpec((tm, d), lambda i: (i, 0)),
        out_shape=jax.ShapeDtypeStruct((rows, d), f32),
        compiler_params=_cparams(("parallel",)),
        name="mixout_even",
    )(hf, hb, z, of, ob, z, dn_g.reshape(1, LANES), *tail_args)


def _mixout_odd(y_c, of, ob, z, gla_g, w, x, g, gt, *, rows_ctx, tm):
    rows, d = x.shape
    ns = H_D * DV_D // LANES
    nv = DV_D // LANES
    slabs = lambda off: pl.BlockSpec((ns, tm, LANES), lambda i: (off // ns, i, 0))
    tail_specs, tail_args = _mixout_specs(w, x, g, gt, rows_ctx, tm)
    return pl.pallas_call(
        _mixout_odd_kernel,
        grid=(rows // tm,),
        in_specs=[
            pl.BlockSpec((tm, D_C), lambda i: (i, 0)),
            slabs(0), slabs(0), slabs(O_R),
            pl.BlockSpec((nv, 1, LANES), lambda i: (0, 0, 0)),
        ] + tail_specs,
        out_specs=pl.BlockSpec((tm, d), lambda i: (i, 0)),
        out_shape=jax.ShapeDtypeStruct((rows, d), f32),
        compiler_params=_cparams(("parallel",)),
        name="mixout_odd",
    )(y_c, of, ob, z, gla_g.reshape(nv, 1, LANES), *tail_args)


def _mmres(a, w, x, g, gt, *, rows_ctx, tm):
    rows, kdim = a.shape
    d = w.shape[1]
    grp = lambda i: ((i * tm >= rows_ctx).astype(jnp.int32), 0, 0)
    return pl.pallas_call(
        _mmres_kernel,
        grid=(rows // tm,),
        in_specs=[
            pl.BlockSpec((tm, kdim), lambda i: (i, 0)),
            pl.BlockSpec((kdim, d), lambda i: (0, 0), pipeline_mode=pl.Buffered(1)),
            pl.BlockSpec((tm, d), lambda i: (i, 0)),
            pl.BlockSpec((1, d), lambda i: (0, 0)),
            pl.BlockSpec((1, SUBLANES, d), grp),
        ],
        out_specs=pl.BlockSpec((tm, d), lambda i: (i, 0)),
        out_shape=jax.ShapeDtypeStruct((rows, d), f32),
        compiler_params=_cparams(("parallel",)),
        name="mmres",
    )(a, w, x, g.reshape(1, d), gt)


def _step_blk(s, gc, nd, rev):
    if not rev:
        return s
    return jnp.where(s < gc, s, 2 * gc + nd - 1 - s)


def _conv4(x, w4, is_ctx):
    rows = x.shape[0]
    t = lax.broadcasted_iota(jnp.int32, x.shape, 0) // SUBLANES
    tl = jnp.where(is_ctx, t, t % GRID_W)
    last = jnp.where(is_ctx, TB - 1, GRID_W - 1)
    y = x * w4[2:3, :]
    xm2 = pltpu.roll(x, 2 * SUBLANES, 0)
    y = y + jnp.where(tl >= 2, xm2, 0.0) * w4[0:1, :]
    xm1 = pltpu.roll(x, SUBLANES, 0)
    y = y + jnp.where(tl >= 1, xm1, 0.0) * w4[1:2, :]
    xp1 = pltpu.roll(x, rows - SUBLANES, 0)
    y = y + jnp.where(tl < last, xp1, 0.0) * w4[3:4, :]
    return y


def _tile_cumsum(src_ref, dst_ref, rev):
    ntile = src_ref.shape[0] // SUBLANES

    def body(i, run):
        t = (ntile - 1 - i) if rev else i
        r = pl.multiple_of(t * SUBLANES, SUBLANES)
        run = jnp.where(i % CHUNK == 0, 0.0, run) + src_ref[pl.ds(r, SUBLANES), :]
        dst_ref[pl.ds(r, SUBLANES), :] = run
        return run

    lax.fori_loop(0, ntile, body, jnp.zeros((SUBLANES, src_ref.shape[1]), f32), unroll=8)


def _pick_lane(x, lane):
    onehot = lax.broadcasted_iota(jnp.int32, x.shape, 1) == lane
    col = jnp.sum(jnp.where(onehot, x, 0.0), axis=-1, keepdims=True)
    return jnp.broadcast_to(col, x.shape)


def _lru_gate_kernel(xa_ref, cw_ref, w_ref, bias_ref, sp_ref, a_ref, b_ref, *, gc):
    is_ctx = pl.program_id(0) < gc
    u = _conv4(xa_ref[0], cw_ref[0], is_ctx)
    for d in range(2):
        r = _sigmoid(_mm(u, w_ref[d, 0, 0]) + bias_ref[d, 0, 0])
        i = _sigmoid(_mm(u, w_ref[d, 1, 0]) + bias_ref[d, 1, 0])
        log_a = -LRU_C * r * sp_ref[d, 0]
        a = jnp.exp(log_a)
        b = jnp.sqrt(-jnp.tanh(log_a) * (a * a + 1.0)) * (i * u)
        a_ref[d] = a
        b_ref[d] = b


def _lru_gates(z, conv_a, wbd, bias, sp, *, gc, nblk):
    rows = z.shape[1]
    ns = D_A // LANES
    out = jax.ShapeDtypeStruct((2, rows, D_A), f32)
    return pl.pallas_call(
        functools.partial(_lru_gate_kernel, gc=gc),
        grid=(nblk, ns),
        in_specs=[
            pl.BlockSpec((1, RB, LANES), lambda i, s: (E_XA + s, i, 0)),
            pl.BlockSpec((1, 4, LANES), lambda i, s: (s, 0, 0)),
            pl.BlockSpec((2, 2, 1, LANES, LANES), lambda i, s: (0, 0, s, 0, 0)),
            pl.BlockSpec((2, 2, 1, 1, LANES), lambda i, s: (0, 0, s, 0, 0)),
            pl.BlockSpec((2, 1, 1, LANES), lambda i, s: (0, s, 0, 0)),
        ],
        out_specs=[pl.BlockSpec((2, RB, LANES), lambda i, s: (0, i, s))] * 2,
        out_shape=[out, out],
        compiler_params=_cparams(("parallel", "parallel")),
        name="lru_gates",
    )(z, conv_a, wbd, bias, sp)


def _lru_scan_kernel(a_ref, b_ref, h0_ref, h_ref, fin_ref, carry_ref, *, gc, rev):
    s = pl.program_id(1)

    @pl.when(s < gc)
    def _():
        carry_ref[...] = jnp.zeros_like(carry_ref)

    @pl.when(s == gc)
    def _():
        carry_ref[...] = h0_ref[...]

    def body(i, h):
        t = (TB - 1 - i) if rev else i
        r = pl.multiple_of(t * SUBLANES, SUBLANES)
        h = a_ref[0, pl.ds(r, SUBLANES), :] * h + b_ref[0, pl.ds(r, SUBLANES), :]
        h_ref[pl.ds(r, SUBLANES), :] = h
        return h

    h = lax.fori_loop(0, TB, body, carry_ref[...], unroll=8)
    carry_ref[...] = h

    @pl.when(s < gc)
    def _():
        fin_ref[0] = h


def _lru_scan(a, b, h0, *, d, gc, nd, rev):
    rows = a.shape[1]
    nblk = gc + nd
    wc = 512
    blk = lambda c, s: _step_blk(s, gc, nd, rev)
    return pl.pallas_call(
        functools.partial(_lru_scan_kernel, gc=gc, rev=rev),
        grid=(D_A // wc, nblk),
        in_specs=[
            pl.BlockSpec((1, RB, wc), lambda c, s: (d, blk(c, s), c)),
            pl.BlockSpec((1, RB, wc), lambda c, s: (d, blk(c, s), c)),
            pl.BlockSpec((SUBLANES, wc), lambda c, s: (0, c)),
        ],
        out_specs=[
            pl.BlockSpec((RB, wc), lambda c, s: (blk(c, s), c)),
            pl.BlockSpec((1, SUBLANES, wc), lambda c, s: (jnp.minimum(s, gc - 1), 0, c)),
        ],
        out_shape=[
            jax.ShapeDtypeStruct((rows, D_A), f32),
            jax.ShapeDtypeStruct((gc, SUBLANES, D_A), f32),
        ],
        scratch_shapes=[pltpu.VMEM((SUBLANES, wc), f32)],
        compiler_params=_cparams(("parallel", "arbitrary")),
        name="lru_scan_bwd" if rev else "lru_scan_fwd",
    )(a, b, h0)


def _delta_kernel(q_ref, k_ref, v_ref, ba_ref, cw_ref, nega_ref, dtb_ref, s0_ref, o_ref, fin_ref,
                  qs_ref, ks_ref, vs_ref, bs_ref, la_ref, gs_ref, st_ref, *, gc, rev, d):
    h = pl.program_id(0)
    s = pl.program_id(1)
    is_ctx = s < gc

    @pl.when(is_ctx)
    def _():
        st_ref[...] = jnp.zeros_like(st_ref)

    @pl.when(s == gc)
    def _():
        st_ref[...] = s0_ref[0]

    def l2n(x):
        return x * lax.rsqrt(jnp.sum(x * x, axis=-1, keepdims=True) + EPS)

    qs_ref[...] = l2n(_silu(_conv4(q_ref[0], cw_ref[0, 0], is_ctx))) * (DK_B ** -0.5)
    ks_ref[...] = l2n(_silu(_conv4(k_ref[0], cw_ref[1, 0], is_ctx)))
    vs_ref[...] = _silu(_conv4(v_ref[0], cw_ref[2, 0], is_ctx))
    ba = ba_ref[0]
    bs_ref[...] = _pick_lane(_sigmoid(ba), d * H_B + h)
    la_ref[...] = _pick_lane(nega_ref[...] * jax.nn.softplus(ba + dtb_ref[...]), 2 * H_B + d * H_B + h)
    _tile_cumsum(la_ref, gs_ref, rev)

    ii = lax.broadcasted_iota(jnp.int32, (CHUNK, CHUNK), 0)
    jj = lax.broadcasted_iota(jnp.int32, (CHUNK, CHUNK), 1)
    incl = (jj >= ii) if rev else (jj <= ii)
    strict = (jj > ii) if rev else (jj < ii)
    eye = (ii == jj).astype(f32)
    last = 0 if rev else CHUNK - 1
    diag8 = (ii // 8) == (jj // 8)
    merge_masks = [((ii // (2 * sz)) == (jj // (2 * sz))) & ((ii // sz) != (jj // sz)) for sz in (8, 16, 32)]

    def chunk_body(ci, carry):
        c = (TB // CHUNK - 1 - ci) if rev else ci
        units = range(SUBLANES)
        rows = [pl.ds(c * CROWS + b, CHUNK, stride=SUBLANES) for b in units]
        q = [qs_ref[r, :] for r in rows]
        k = [ks_ref[r, :] for r in rows]
        v = [vs_ref[r, :] for r in rows]
        beta = [bs_ref[r, :] for r in rows]
        g = [gs_ref[r, :] for r in rows]
        kb = [k[b] * beta[b] for b in units]
        dec_incl, low = [], []
        for b in units:
            gsq = g[b][:, :CHUNK]
            g_row = jnp.sum(gsq * eye, axis=0, keepdims=True)
            diff = gsq - g_row
            dec = jnp.where(incl, jnp.exp(jnp.where(incl, diff, 0.0)), 0.0)
            dec_incl.append(dec)
            low.append(_mm_nt(kb[b], k[b]) * jnp.where(strict, dec, 0.0))
        ld = [jnp.where(diag8, low[b], 0.0) for b in units]
        ls = [_split(ld[b]) for b in units]
        d2 = [_mm3(ls[b], ls[b]) for b in units]
        d2s = [_split(d2[b]) for b in units]
        t_inv = [eye - ld[b] for b in units]
        ts = [_split(t_inv[b]) for b in units]
        prod, d4 = [], []
        for b in units:
            prod.append(_mm3(ts[b], d2s[b]))
            d4.append(_mm3(d2s[b], d2s[b]))
        t_inv = [t_inv[b] + prod[b] for b in units]
        t_inv = [t_inv[b] + _mm3(_split(t_inv[b]), _split(d4[b])) for b in units]
        for lvl in merge_masks:
            ts = [_split(t_inv[b]) for b in units]
            tl = [_mm3(ts[b], _split(jnp.where(lvl, low[b], 0.0))) for b in units]
            t_inv = [t_inv[b] - _mm3(_split(tl[b]), ts[b]) for b in units]
        eg = [jnp.exp(g[b]) for b in units]
        ts = [_split(t_inv[b]) for b in units]
        uw = [_mm3(ts[b], _split(jnp.concatenate([v[b] * beta[b], kb[b] * eg[b]], axis=1))) for b in units]
        a_qk = [_mm_nt(q[b], k[b]) * dec_incl[b] for b in units]
        g_last = [g[b][last:last + 1, :] for b in units]
        st = [st_ref[b] for b in units]
        v_new = [uw[b][:, :DV_B] - _mm(uw[b][:, DV_B:], st[b]) for b in units]
        o = [_mm(q[b] * eg[b], st[b]) + _mm(a_qk[b], v_new[b]) for b in units]
        for b in units:
            k_g = k[b] * jnp.exp(g_last[b] - g[b])
            st_ref[b] = st[b] * jnp.exp(g_last[b]) + _mm_tn(k_g, v_new[b])
            o_ref[0, rows[b], :] = o[b]
        return carry

    lax.fori_loop(0, TB // CHUNK, chunk_body, 0)

    @pl.when(is_ctx)
    def _():
        fin_ref[0, 0] = st_ref[...]


def _delta(z, conv_qkv, nega, dtb, s0, *, d, gc, nd, rev):
    rows = z.shape[1]
    nblk = gc + nd
    blk = lambda h, s: _step_blk(s, gc, nd, rev)
    slab = lambda off: pl.BlockSpec((1, RB, LANES), lambda h, s: (off + h, blk(h, s), 0))
    scr = pltpu.VMEM((RB, LANES), f32)
    return pl.pallas_call(
        functools.partial(_delta_kernel, gc=gc, rev=rev, d=d),
        grid=(H_B, nblk),
        in_specs=[
            slab(E_Q), slab(E_K), slab(E_V),
            pl.BlockSpec((1, RB, LANES), lambda h, s: (E_BA, blk(h, s), 0)),
            pl.BlockSpec((3, 1, 4, LANES), lambda h, s: (0, h, 0, 0)),
            pl.BlockSpec((1, LANES), lambda h, s: (0, 0)),
            pl.BlockSpec((1, LANES), lambda h, s: (0, 0)),
            pl.BlockSpec((1, SUBLANES, DK_B, DV_B), lambda h, s: (h, 0, 0, 0)),
        ],
        out_specs=[
            pl.BlockSpec((1, RB, LANES), lambda h, s: (h, blk(h, s), 0)),
            pl.BlockSpec((1, 1, SUBLANES, DK_B, DV_B), lambda h, s: (jnp.minimum(s, gc - 1), h, 0, 0, 0)),
        ],
        out_shape=[
            jax.ShapeDtypeStruct((H_B, rows, LANES), f32),
            jax.ShapeDtypeStruct((gc, H_B, SUBLANES, DK_B, DV_B), f32),
        ],
        scratch_shapes=[scr, scr, scr, scr, scr, scr, pltpu.VMEM((SUBLANES, DK_B, DV_B), f32)],
        compiler_params=_cparams(("parallel", "arbitrary")),
        name="delta_bwd" if rev else "delta_fwd",
    )(z, z, z, z, conv_qkv, nega, dtb, s0)


def _s5_kernel(u_ref, bt_ref, ct_ref, lam_ref, s0_ref, y_ref, fin_ref, sbuf_ref, carry_ref, *, gc, rev):
    s = pl.program_id(1)
    half = sbuf_ref.shape[1] // 2

    @pl.when(s < gc)
    def _():
        carry_ref[...] = jnp.zeros_like(carry_ref)

    @pl.when(s == gc)
    def _():
        carry_ref[...] = s0_ref[0]

    nsl = u_ref.shape[0]
    sbuf_ref[...] = _mm(jnp.concatenate([u_ref[j] for j in range(nsl)], axis=1), bt_ref[0])
    lr = jnp.broadcast_to(lam_ref[0, 0:1, :], (SUBLANES, half))
    li = jnp.broadcast_to(lam_ref[0, 1:2, :], (SUBLANES, half))

    def body(i, hc):
        hr, hi = hc
        t = (TB - 1 - i) if rev else i
        r = pl.multiple_of(t * SUBLANES, SUBLANES)
        nr = lr * hr - li * hi + sbuf_ref[pl.ds(r, SUBLANES), :half]
        ni = lr * hi + li * hr + sbuf_ref[pl.ds(r, SUBLANES), half:]
        sbuf_ref[pl.ds(r, SUBLANES), :half] = nr
        sbuf_ref[pl.ds(r, SUBLANES), half:] = ni
        return nr, ni

    hr, hi = lax.fori_loop(0, TB, body, (carry_ref[:, :half], carry_ref[:, half:]), unroll=4)
    carry_ref[:, :half] = hr
    carry_ref[:, half:] = hi
    y = _mm(sbuf_ref[...], ct_ref[0])
    for j in range(nsl):
        y_ref[j] = y[:, j * LANES:(j + 1) * LANES]

    @pl.when(s < gc)
    def _():
        fin_ref[0, 0] = carry_ref[...]


S5_COLS = LANES
S5_GPC = S5_COLS // S5_GW
S5_SW = 2 * S5_GPC * S5_N


def _s5(z, bt, ct, lam, s0, *, gc, nd, rev):
    rows = z.shape[1]
    nblk = gc + nd
    ngt = D_C // S5_COLS
    nsl = S5_COLS // LANES
    sw = S5_SW
    blk = lambda g, s: _step_blk(s, gc, nd, rev)
    return pl.pallas_call(
        functools.partial(_s5_kernel, gc=gc, rev=rev),
        grid=(ngt, nblk),
        in_specs=[
            pl.BlockSpec((nsl, RB, LANES), lambda g, s: (O_U // nsl + g, blk(g, s), 0)),
            pl.BlockSpec((1, S5_COLS, sw), lambda g, s: (g, 0, 0)),
            pl.BlockSpec((1, sw, S5_COLS), lambda g, s: (g, 0, 0)),
            pl.BlockSpec((1, 2, sw // 2), lambda g, s: (g, 0, 0)),
            pl.BlockSpec((1, SUBLANES, sw), lambda g, s: (g, 0, 0)),
        ],
        out_specs=[
            pl.BlockSpec((nsl, RB, LANES), lambda g, s: (g, blk(g, s), 0)),
            pl.BlockSpec((1, 1, SUBLANES, sw), lambda g, s: (jnp.minimum(s, gc - 1), g, 0, 0)),
        ],
        out_shape=[
            jax.ShapeDtypeStruct((D_C // LANES, rows, LANES), f32),
            jax.ShapeDtypeStruct((gc, ngt, SUBLANES, sw), f32),
        ],
        scratch_shapes=[pltpu.VMEM((RB, sw), f32), pltpu.VMEM((SUBLANES, sw), f32)],
        compiler_params=_cparams(("parallel", "arbitrary")),
        name="s5_bwd" if rev else "s5_fwd",
    )(z, bt, ct, lam, s0)


def _s5_out_kernel(yf_ref, yb_ref, u_ref, d_ref, w_ref, b_ref, o_ref):
    ns = yf_ref.shape[0]
    yc = jnp.concatenate([yf_ref[s] + yb_ref[s] + d_ref[s] * u_ref[s] for s in range(ns)], axis=1)
    zc = jax.nn.gelu(yc)
    gate = _sigmoid(jnp.dot(zc.astype(bf16), w_ref[...], preferred_element_type=f32) + b_ref[...])
    o_ref[...] = (zc * gate).astype(o_ref.dtype)


def _s5_out(yf, yb, z, dskip, w_glu, b_glu, *, tm):
    ns, rows, _ = yf.shape
    return pl.pallas_call(
        _s5_out_kernel,
        grid=(rows // tm,),
        in_specs=[
            pl.BlockSpec((ns, tm, LANES), lambda i: (0, i, 0)),
            pl.BlockSpec((ns, tm, LANES), lambda i: (0, i, 0)),
            pl.BlockSpec((ns, tm, LANES), lambda i: (O_U // ns, i, 0)),
            pl.BlockSpec((ns, 1, LANES), lambda i: (0, 0, 0)),
            pl.BlockSpec((D_C, D_C), lambda i: (0, 0)),
            pl.BlockSpec((1, D_C), lambda i: (0, 0)),
        ],
        out_specs=pl.BlockSpec((tm, D_C), lambda i: (i, 0)),
        out_shape=jax.ShapeDtypeStruct((rows, D_C), bf16),
        compiler_params=_cparams(("parallel",)),
        name="s5_out",
    )(yf, yb, z, dskip, w_glu, b_glu)


def _gla_kernel(q_ref, k_ref, v_ref, glr_ref, wg_ref, bg_ref, s0_ref, o_ref, fin_ref,
                la_ref, gs_ref, st_ref, *, gc, rev):
    s = pl.program_id(1)
    is_ctx = s < gc

    @pl.when(is_ctx)
    def _():
        st_ref[...] = jnp.zeros_like(st_ref)

    @pl.when(s == gc)
    def _():
        st_ref[...] = s0_ref[0]

    la_ref[...] = jax.nn.log_sigmoid(_mm(glr_ref[0], wg_ref[0]) + bg_ref[0]) * (1.0 / GLA_TAU)
    _tile_cumsum(la_ref, gs_ref, rev)

    ii = lax.broadcasted_iota(jnp.int32, (SUB, CHUNK), 0)
    jj = lax.broadcasted_iota(jnp.int32, (SUB, CHUNK), 1)
    last = 0 if rev else CHUNK - 1

    def chunk_body(ci, carry):
        c = (TB // CHUNK - 1 - ci) if rev else ci
        units = range(SUBLANES)
        rows = [pl.ds(c * CROWS + b, CHUNK, stride=SUBLANES) for b in units]
        q = [q_ref[0, r, :] * (DK_D ** -0.5) for r in rows]
        k = [k_ref[0, r, :] for r in rows]
        v = [jnp.concatenate([v_ref[0, r, :], v_ref[1, r, :]], axis=1) for r in rows]
        g = [gs_ref[r, :] for r in rows]
        a_rows = [[] for _ in units]
        for blk in range(CHUNK // SUB):
            lo = blk * SUB
            first = lo + SUB - 1 if rev else lo
            causal = (jj >= ii + lo) if rev else (jj <= ii + lo)
            for b in units:
                g_first = g[b][first:first + 1, :]
                qt = q[b][lo:lo + SUB, :] * jnp.exp(g[b][lo:lo + SUB, :] - g_first)
                kt = k[b] * jnp.exp(g_first - g[b])
                a_rows[b].append(jnp.where(causal, _mm_nt(qt, kt), 0.0))
        a_qk = [jnp.concatenate(a_rows[b], axis=0) for b in units]
        g_last = [g[b][last:last + 1, :] for b in units]
        st = [st_ref[b] for b in units]
        o = [_mm_nt(q[b] * jnp.exp(g[b]), st[b]) + _mm(a_qk[b], v[b]) for b in units]
        for b in units:
            k_g = k[b] * jnp.exp(g_last[b] - g[b])
            st_ref[b] = st[b] * jnp.exp(g_last[b]) + _mm_tn(v[b], k_g)
            o_ref[0, rows[b], :] = o[b][:, :LANES]
            o_ref[1, rows[b], :] = o[b][:, LANES:]
        return carry

    lax.fori_loop(0, TB // CHUNK, chunk_body, 0)

    @pl.when(is_ctx)
    def _():
        fin_ref[0, 0] = st_ref[...]


def _gla(z, wg, bg, s0, *, gc, nd, rev):
    rows = z.shape[1]
    nblk = gc + nd
    nv = DV_D // LANES
    blk = lambda h, s: _step_blk(s, gc, nd, rev)
    scr = pltpu.VMEM((RB, LANES), f32)
    return pl.pallas_call(
        functools.partial(_gla_kernel, gc=gc, rev=rev),
        grid=(H_D, nblk),
        in_specs=[
            pl.BlockSpec((1, RB, LANES), lambda h, s: (O_Q + h, blk(h, s), 0)),
            pl.BlockSpec((1, RB, LANES), lambda h, s: (O_K + h, blk(h, s), 0)),
            pl.BlockSpec((nv, RB, LANES), lambda h, s: (O_V // nv + h, blk(h, s), 0)),
            pl.BlockSpec((1, RB, LANES), lambda h, s: (O_GLR, blk(h, s), 0)),
            pl.BlockSpec((1, LANES, LANES), lambda h, s: (h, 0, 0)),
            pl.BlockSpec((1, 1, LANES), lambda h, s: (h, 0, 0)),
            pl.BlockSpec((1, SUBLANES, DV_D, DK_D), lambda h, s: (h, 0, 0, 0)),
        ],
        out_specs=[
            pl.BlockSpec((nv, RB, LANES), lambda h, s: (h, blk(h, s), 0)),
            pl.BlockSpec((1, 1, SUBLANES, DV_D, DK_D), lambda h, s: (jnp.minimum(s, gc - 1), h, 0, 0, 0)),
        ],
        out_shape=[
            jax.ShapeDtypeStruct((H_D * nv, rows, LANES), f32),
            jax.ShapeDtypeStruct((gc, H_D, SUBLANES, DV_D, DK_D), f32),
        ],
        scratch_shapes=[scr, scr, pltpu.VMEM((SUBLANES, DV_D, DK_D), f32)],
        compiler_params=_cparams(("parallel", "arbitrary")),
        name="gla_bwd" if rev else "gla_fwd",
    )(z, z, z, z, wg, bg, s0)


def _to_rows(x):
    b, t, d = x.shape
    return x.reshape(b // SUBLANES, SUBLANES, t, d).transpose(0, 2, 1, 3).reshape(b * t, d)


def _from_rows(r, b, t):
    d = r.shape[-1]
    return r.reshape(b // SUBLANES, t, SUBLANES, d).transpose(0, 2, 1, 3).reshape(b, t, d)


def _pad_cols(w, n):
    return jnp.pad(w, ((0, 0), (0, n - w.shape[1])))


def _unit_states(fin, b):
    fin = jnp.moveaxis(fin, b, 1)
    return fin.reshape((fin.shape[0] * SUBLANES,) + fin.shape[2:])


def kernel(x_prompt, x_sample, state_lru, state_delta, state_s5_re, state_s5_im, state_gla, c, c_ctx, w_ada, b_ada, norm_g, w_up, w_down, w_in_e, w_out_e, conv_a, lru_wa, lru_ba, lru_wx, lru_bx, lru_lambda, conv_qkv, dn_a_log, dn_dt_bias, dn_norm_g, w_in_o, w_out_o, s5_lam_re, s5_lam_im, s5_log_step, s5_b_re, s5_b_im, s5_c_re, s5_c_im, s5_d, s5_w_glu, s5_b_glu, gla_wg2, gla_bg, gla_norm_g):
    bp, tp, d = x_prompt.shape
    bs, ts, _ = x_sample.shape
    assert tp == TB and bp % SUBLANES == 0 and bs == SUBLANES and ts % TB == 0 and d == D_MODEL
    depth = w_ada.shape[0]
    gc = bp // SUBLANES
    nd = ts // TB
    nblk = gc + nd
    rows_ctx = bp * tp
    seq = dict(gc=gc, nd=nd)

    x = jnp.concatenate([_to_rows(x_prompt), _to_rows(x_sample)], axis=0)

    cvec = jnp.concatenate([c, jnp.broadcast_to(c_ctx[None], (SUBLANES, d))], axis=0)
    mod = _ada(cvec, w_ada, b_ada)
    mod = jnp.stack([mod[:, SUBLANES:], mod[:, :SUBLANES]], axis=1)
    mods = [[mod[l, :, :, i * d:(i + 1) * d] for i in range(6)] for l in range(depth)]

    fin_lru, fin_delta, fin_re, fin_im, fin_gla = [], [], [], [], []
    for l in range(depth):
        sh1, sc1, gt1, sh2, sc2, gt2 = mods[l]
        if l % 2 == 0:
            e = l // 2
            w_in = _pad_cols(w_in_e[e], E_SLABS * LANES).astype(bf16)
            z = _nmm(x, norm_g[l, 0], sh1, sc1, w_in, rows_ctx=rows_ctx, tm=256, tn=10 * LANES, slab=True)
            ns = D_A // LANES
            wg = jnp.stack([lru_wa[e], lru_wx[e]], axis=1).reshape(2, 2, ns, 2, LRU_BW, LRU_BW)
            wbd = jnp.zeros((2, 2, ns, LANES, LANES), f32)
            wbd = wbd.at[:, :, :, :LRU_BW, :LRU_BW].set(wg[:, :, :, 0]).at[:, :, :, LRU_BW:, LRU_BW:].set(wg[:, :, :, 1])
            bias = jnp.stack([lru_ba[e], lru_bx[e]], axis=1).reshape(2, 2, ns, 1, LANES)
            sp = jax.nn.softplus(-lru_lambda[e]).reshape(2, ns, 1, LANES)
            cwa = conv_a[e].reshape(4, ns, LANES).transpose(1, 0, 2)
            a, b = _lru_gates(z, cwa, wbd.astype(bf16), bias, sp, gc=gc, nblk=nblk)
            h_f, f_f = _lru_scan(a, b, state_lru[:, e, 0], d=0, rev=False, **seq)
            h_b, f_b = _lru_scan(a, b, state_lru[:, e, 1], d=1, rev=True, **seq)
            fin_lru.append(jnp.stack([_unit_states(f_f, 1), _unit_states(f_b, 1)], axis=1))
            cq = conv_qkv[e].reshape(4, 3, H_B, LANES).transpose(1, 2, 0, 3)
            lane_pad = lambda v: jnp.pad(v.reshape(1, 2 * H_B), ((0, 0), (2 * H_B, LANES - 4 * H_B)))
            nega = lane_pad(-jnp.exp(dn_a_log[e]))
            dtb = lane_pad(dn_dt_bias[e])
            s0 = state_delta[:, e].transpose(1, 2, 0, 3, 4)
            o_f, s_f = _delta(z, cq, nega, dtb, s0[0], d=0, rev=False, **seq)
            o_b, s_b = _delta(z, cq, nega, dtb, s0[1], d=1, rev=True, **seq)
            fin_delta.append(jnp.stack([_unit_states(s_f, 2), _unit_states(s_b, 2)], axis=1))
            x = _mixout_even(h_f, h_b, o_f, o_b, z, dn_norm_g[e], w_out_e[e].astype(bf16), x, norm_g[l, 1], gt1,
                             rows_ctx=rows_ctx, tm=512)
        else:
            o = l // 2
            w_in = _pad_cols(w_in_o[o], O_SLABS * LANES).astype(bf16)
            z = _nmm(x, norm_g[l, 0], sh1, sc1, w_in, rows_ctx=rows_ctx, tm=256, tn=11 * LANES, slab=True)
            ngt = D_C // S5_COLS
            gps = S5_GPC
            lam = lax.complex(s5_lam_re[o], s5_lam_im[o])
            lam_bar = jnp.exp(lam * jnp.exp(s5_log_step[o])[..., None])
            b_bar = ((lam_bar - 1.0) / lam)[..., None] * lax.complex(s5_b_re[o], s5_b_im[o])
            eye_g = jnp.eye(gps, dtype=f32)

            def bmat(p):
                p = p.reshape(2, ngt, gps, S5_N, S5_GW)
                return jnp.einsum('dtgnw,gh->dtgwhn', p, eye_g).reshape(2, ngt, S5_COLS, gps * S5_N)

            def cmat(p):
                p = p.reshape(2, ngt, gps, S5_GW, S5_N)
                return jnp.einsum('dtgwn,gh->dtgnhw', p, eye_g).reshape(2, ngt, gps * S5_N, S5_COLS)

            bt = jnp.concatenate([bmat(b_bar.real), bmat(b_bar.imag)], axis=-1).astype(bf16)
            ct = jnp.concatenate([cmat(s5_c_re[o]), -cmat(s5_c_im[o])], axis=-2).astype(bf16)
            lamv = jnp.stack([lam_bar.real, lam_bar.imag], axis=2).reshape(2, ngt, gps, 2, S5_N)
            lamv = lamv.transpose(0, 1, 3, 2, 4).reshape(2, ngt, 2, gps * S5_N)

            def s5_state(sre, sim):
                f = lambda p: p.reshape(SUBLANES, ngt, gps * S5_N).transpose(1, 0, 2)
                return jnp.concatenate([f(sre), f(sim)], axis=-1)

            y_f, c_f = _s5(z, bt[0], ct[0], lamv[0], s5_state(state_s5_re[:, o, 0], state_s5_im[:, o, 0]), rev=False, **seq)
            y_bk, c_b = _s5(z, bt[1], ct[1], lamv[1], s5_state(state_s5_re[:, o, 1], state_s5_im[:, o, 1]), rev=True, **seq)
            y_c = _s5_out(y_f, y_bk, z, s5_d[o].reshape(D_C // LANES, 1, LANES), s5_w_glu[o].astype(bf16),
                          s5_b_glu[o].reshape(1, D_C), tm=512)

            def s5_fin(cf):
                cf = cf.transpose(0, 2, 1, 3).reshape(gc * SUBLANES, ngt, 2, gps, S5_N)
                return (cf[:, :, 0].reshape(gc * SUBLANES, S5_G, S5_N), cf[:, :, 1].reshape(gc * SUBLANES, S5_G, S5_N))

            (rf, imf), (rb, imb) = s5_fin(c_f), s5_fin(c_b)
            fin_re.append(jnp.stack([rf, rb], axis=1))
            fin_im.append(jnp.stack([imf, imb], axis=1))
            wgp = jnp.zeros((2, H_D, LANES, LANES), f32)
            wg2 = gla_wg2[o].reshape(2, GLA_RANK, H_D, DK_D).transpose(0, 2, 1, 3)
            wgp = wgp.at[0, :, :GLA_RANK].set(wg2[0]).at[1, :, GLA_RANK:2 * GLA_RANK].set(wg2[1]).astype(bf16)
            bgp = gla_bg[o].reshape(2, H_D, 1, DK_D)
            g0 = state_gla[:, o].transpose(1, 2, 0, 4, 3)
            o_f, g_f = _gla(z, wgp[0], bgp[0], g0[0], rev=False, **seq)
            o_b, g_b = _gla(z, wgp[1], bgp[1], g0[1], rev=True, **seq)
            gfin = lambda gf: jnp.swapaxes(_unit_states(gf, 2), -1, -2)
            fin_gla.append(jnp.stack([gfin(g_f), gfin(g_b)], axis=1))
            x = _mixout_odd(y_c, o_f, o_b, z, gla_norm_g[o], w_out_o[o].astype(bf16), x, norm_g[l, 1], gt1,
                            rows_ctx=rows_ctx, tm=512)

        hmid = _nmm(x, norm_g[l, 2], sh2, sc2, w_up[l].astype(bf16), rows_ctx=rows_ctx, tm=256, tn=2048,
                    relu2=True, out_dtype=bf16)
        x = _mmres(hmid, w_down[l].astype(bf16), x, norm_g[l, 3], gt2, rows_ctx=rows_ctx, tm=256)

    y_prompt = _from_rows(x[:rows_ctx], bp, tp)
    y_sample = _from_rows(x[rows_ctx:], bs, ts)
    return (y_prompt, y_sample, jnp.stack(fin_lru, axis=1), jnp.stack(fin_delta, axis=1),
            jnp.stack(fin_re, axis=1), jnp.stack(fin_im, axis=1), jnp.stack(fin_gla, axis=1))
```

```python
import functools
import math

import jax
import jax.numpy as jnp
from jax import lax
from jax.experimental import pallas as pl
from jax.experimental.pallas import tpu as pltpu

f32 = jnp.float32
bf16 = jnp.bfloat16

LANES = 128
SUBLANES = 8
VMEM_LIMIT_BYTES = 56 * 1024 * 1024

D_MODEL = 2048
D_FF = 4 * D_MODEL
GRID_W = 64
CHUNK = 64
EPS = 1e-6
D_A = D_MODEL // 2
LRU_BLOCKS = 16
LRU_BW = D_A // LRU_BLOCKS
LRU_C = 8.0
H_B = 8
DK_B = 128
DV_B = 128
D_C = D_MODEL // 2
S5_GW = 16
S5_G = D_C // S5_GW
S5_N = 64
H_D = 4
DK_D = 128
DV_D = 256
GLA_RANK = 16
GLA_TAU = 16.0

TB = 256
RB = TB * SUBLANES
CROWS = CHUNK * SUBLANES
SUB = 16

E_XA, E_GA, E_Q, E_K, E_V, E_GO, E_BA = 0, 8, 16, 24, 32, 40, 48
E_SLABS = 50
O_U, O_Q, O_K, O_V, O_R, O_GLR = 0, 8, 12, 16, 24, 32
O_SLABS = 33


def _cparams(sem):
    return pltpu.CompilerParams(dimension_semantics=sem, vmem_limit_bytes=VMEM_LIMIT_BYTES)


def _mm(a, b):
    return jnp.dot(a.astype(bf16), b.astype(bf16), preferred_element_type=f32)


def _mm_nt(a, b):
    return lax.dot_general(a.astype(bf16), b.astype(bf16), (((1,), (1,)), ((), ())), preferred_element_type=f32)


def _mm_tn(a, b):
    return lax.dot_general(a.astype(bf16), b.astype(bf16), (((0,), (0,)), ((), ())), preferred_element_type=f32)


def _split(x):
    hi = x.astype(bf16)
    return hi, (x - hi.astype(f32)).astype(bf16)


def _mm3(a, b):
    dot = functools.partial(jnp.dot, preferred_element_type=f32)
    return dot(a[0], b[0]) + (dot(a[0], b[1]) + dot(a[1], b[0]))


def _sigmoid(x):
    return 0.5 * jnp.tanh(0.5 * x) + 0.5


def _silu(x):
    t = 0.5 * x
    return t + t * jnp.tanh(t)


def _ada_kernel(c_ref, w_ref, b_ref, o_ref):
    a = _silu(c_ref[...])
    o_ref[0] = _mm(a, w_ref[0]) + b_ref[0]


def _ada(cvec, w_ada, b_ada):
    depth, d, n = w_ada.shape
    tn = 1024
    return pl.pallas_call(
        _ada_kernel,
        grid=(depth, n // tn),
        in_specs=[
            pl.BlockSpec(cvec.shape, lambda l, j: (0, 0)),
            pl.BlockSpec((1, d, tn), lambda l, j: (l, 0, j)),
            pl.BlockSpec((1, 1, tn), lambda l, j: (l, 0, j)),
        ],
        out_specs=pl.BlockSpec((1, cvec.shape[0], tn), lambda l, j: (l, 0, j)),
        out_shape=jax.ShapeDtypeStruct((depth, cvec.shape[0], n), f32),
        compiler_params=_cparams(("parallel", "parallel")),
        name="ada",
    )(cvec, w_ada, b_ada.reshape(depth, 1, n))


def _nmm_kernel(x_ref, g_ref, sh_ref, sc_ref, w_ref, o_ref, *, relu2, slab, tn):
    x = x_ref[...]
    tm, d = x.shape
    y = x * lax.rsqrt(jnp.mean(x * x, axis=-1, keepdims=True) + EPS)
    gain = g_ref[...] * (1.0 + sc_ref[0])
    h = (y.reshape(tm // SUBLANES, SUBLANES, d) * gain[None] + sh_ref[0][None]).reshape(tm, d).astype(bf16)
    for c in range(w_ref.shape[1] // tn):
        acc = jnp.dot(h, w_ref[:, c * tn:(c + 1) * tn], preferred_element_type=f32)
        if relu2:
            acc = jnp.square(jnp.maximum(acc, 0.0))
        if slab:
            for s in range(tn // LANES):
                o_ref[c * (tn // LANES) + s] = acc[:, s * LANES:(s + 1) * LANES].astype(o_ref.dtype)
        else:
            o_ref[:, c * tn:(c + 1) * tn] = acc.astype(o_ref.dtype)


def _nmm(x, g, sh, sc, w, *, rows_ctx, tm, tn, relu2=False, slab=False, out_dtype=f32):
    rows, d = x.shape
    n = w.shape[1]
    assert n % tn == 0
    grp = lambda i: ((i * tm >= rows_ctx).astype(jnp.int32), 0, 0)
    if slab:
        out_shape = jax.ShapeDtypeStruct((n // LANES, rows, LANES), out_dtype)
        out_spec = pl.BlockSpec((n // LANES, tm, LANES), lambda i: (0, i, 0))
    else:
        out_shape = jax.ShapeDtypeStruct((rows, n), out_dtype)
        out_spec = pl.BlockSpec((tm, n), lambda i: (i, 0))
    return pl.pallas_call(
        functools.partial(_nmm_kernel, relu2=relu2, slab=slab, tn=tn),
        grid=(rows // tm,),
        in_specs=[
            pl.BlockSpec((tm, d), lambda i: (i, 0)),
            pl.BlockSpec((1, d), lambda i: (0, 0)),
            pl.BlockSpec((1, SUBLANES, d), grp),
            pl.BlockSpec((1, SUBLANES, d), grp),
            pl.BlockSpec((d, n), lambda i: (0, 0), pipeline_mode=pl.Buffered(1)),
        ],
        out_specs=out_spec,
        out_shape=out_shape,
        compiler_params=_cparams(("parallel",)),
        name="nmm",
    )(x, g.reshape(1, d), sh, sc, w)


def _mmres_kernel(a_ref, w_ref, x_ref, g_ref, gt_ref, o_ref):
    y = jnp.dot(a_ref[...], w_ref[...], preferred_element_type=f32)
    _gated_residual(y, x_ref, g_ref, gt_ref, o_ref)


def _gated_residual(y, x_ref, g_ref, gt_ref, o_ref):
    tm, d = y.shape
    yn = y * lax.rsqrt(jnp.mean(y * y, axis=-1, keepdims=True) + EPS) * g_ref[...]
    o = x_ref[...].reshape(tm // SUBLANES, SUBLANES, d) + gt_ref[0][None] * yn.reshape(tm // SUBLANES, SUBLANES, d)
    o_ref[...] = o.reshape(tm, d)


def _headnorm_gate(o_slabs, gain, gate_slabs):
    width = len(o_slabs) * LANES
    ms = sum(jnp.sum(o * o, axis=-1, keepdims=True) for o in o_slabs) * (1.0 / width)
    inv = lax.rsqrt(ms + EPS)
    return [(o * inv * gain[j] * _silu(r)).astype(bf16) for j, (o, r) in enumerate(zip(o_slabs, gate_slabs))]


def _mixout_even_kernel(hf_ref, hb_ref, ga_ref, of_ref, ob_ref, go_ref, dg_ref, w1_ref, w2_ref,
                        x_ref, g_ref, gt_ref, o_ref):
    ns = ga_ref.shape[0]
    ga = jnp.concatenate([ga_ref[s] for s in range(ns)], axis=1)
    y_a = ((hf_ref[...] + hb_ref[...]) * jax.nn.gelu(ga)).astype(bf16)
    y = jnp.dot(y_a, w1_ref[...], preferred_element_type=f32)
    y_b = [_headnorm_gate([of_ref[h] + ob_ref[h]], [dg_ref[...]], [go_ref[h]])[0] for h in range(of_ref.shape[0])]
    y = y + jnp.dot(jnp.concatenate(y_b, axis=1), w2_ref[...], preferred_element_type=f32)
    _gated_residual(y, x_ref, g_ref, gt_ref, o_ref)


def _mixout_odd_kernel(yc_ref, of_ref, ob_ref, r_ref, dg_ref, w1_ref, w2_ref, x_ref, g_ref, gt_ref, o_ref):
    nv = DV_D // LANES
    y = jnp.dot(yc_ref[...], w1_ref[...], preferred_element_type=f32)
    y_d = []
    for h in range(of_ref.shape[0] // nv):
        sl = range(h * nv, (h + 1) * nv)
        y_d += _headnorm_gate([of_ref[j] + ob_ref[j] for j in sl], [dg_ref[j] for j in range(nv)], [r_ref[j] for j in sl])
    y = y + jnp.dot(jnp.concatenate(y_d, axis=1), w2_ref[...], preferred_element_type=f32)
    _gated_residual(y, x_ref, g_ref, gt_ref, o_ref)


def _mixout_specs(w, x, g, gt, rows_ctx, tm):
    kh = w.shape[0] // 2
    d = w.shape[1]
    grp = lambda i: ((i * tm >= rows_ctx).astype(jnp.int32), 0, 0)
    specs = [
        pl.BlockSpec((kh, d), lambda i: (0, 0), pipeline_mode=pl.Buffered(1)),
        pl.BlockSpec((kh, d), lambda i: (1, 0), pipeline_mode=pl.Buffered(1)),
        pl.BlockSpec((tm, d), lambda i: (i, 0)),
        pl.BlockSpec((1, d), lambda i: (0, 0)),
        pl.BlockSpec((1, SUBLANES, d), grp),
    ]
    return specs, (w, w, x, g.reshape(1, d), gt)


def _mixout_even(hf, hb, of, ob, z, dn_g, w, x, g, gt, *, rows_ctx, tm):
    rows, d = x.shape
    ns = D_A // LANES
    slabs = lambda off: pl.BlockSpec((ns, tm, LANES), lambda i: (off // ns, i, 0))
    tail_specs, tail_args = _mixout_specs(w, x, g, gt, rows_ctx, tm)
    return pl.pallas_call(
        _mixout_even_kernel,
        grid=(rows // tm,),
        in_specs=[
            pl.BlockSpec((tm, D_A), lambda i: (i, 0)),
            pl.BlockSpec((tm, D_A), lambda i: (i, 0)),
            slabs(E_GA), slabs(0), slabs(0), slabs(E_GO),
            pl.BlockSpec((1, LANES), lambda i: (0, 0)),
        ] + tail_specs,
        out_specs=pl.BlockSpec((tm, d), lambda i: (i, 0)),
        out_shape=jax.ShapeDtypeStruct((rows, d), f32),
        compiler_params=_cparams(("parallel",)),
        name="mixout_even",
    )(hf, hb, z, of, ob, z, dn_g.reshape(1, LANES), *tail_args)


def _mixout_odd(y_c, of, ob, z, gla_g, w, x, g, gt, *, rows_ctx, tm):
    rows, d = x.shape
    ns = H_D * DV_D // LANES
    nv = DV_D // LANES
    slabs = lambda off: pl.BlockSpec((ns, tm, LANES), lambda i: (off // ns, i, 0))
    tail_specs, tail_args = _mixout_specs(w, x, g, gt, rows_ctx, tm)
    return pl.pallas_call(
        _mixout_odd_kernel,
        grid=(rows // tm,),
        in_specs=[
            pl.BlockSpec((tm, D_C), lambda i: (i, 0)),
            slabs(0), slabs(0), slabs(O_R),
            pl.BlockSpec((nv, 1, LANES), lambda i: (0, 0, 0)),
        ] + tail_specs,
        out_specs=pl.BlockSpec((tm, d), lambda i: (i, 0)),
        out_shape=jax.ShapeDtypeStruct((rows, d), f32),
        compiler_params=_cparams(("parallel",)),
        name="mixout_odd",
    )(y_c, of, ob, z, gla_g.reshape(nv, 1, LANES), *tail_args)


def _mmres(a, w, x, g, gt, *, rows_ctx, tm):
    rows, kdim = a.shape
    d = w.shape[1]
    grp = lambda i: ((i * tm >= rows_ctx).astype(jnp.int32), 0, 0)
    return pl.pallas_call(
        _mmres_kernel,
        grid=(rows // tm,),
        in_specs=[
            pl.BlockSpec((tm, kdim), lambda i: (i, 0)),
            pl.BlockSpec((kdim, d), lambda i: (0, 0), pipeline_mode=pl.Buffered(1)),
            pl.BlockSpec((tm, d), lambda i: (i, 0)),
            pl.BlockSpec((1, d), lambda i: (0, 0)),
            pl.BlockSpec((1, SUBLANES, d), grp),
        ],
        out_specs=pl.BlockSpec((tm, d), lambda i: (i, 0)),
        out_shape=jax.ShapeDtypeStruct((rows, d), f32),
        compiler_params=_cparams(("parallel",)),
        name="mmres",
    )(a, w, x, g.reshape(1, d), gt)


def _step_blk(s, gc, nd, rev):
    if not rev:
        return s
    return jnp.where(s < gc, s, 2 * gc + nd - 1 - s)


def _conv4(x, w4, is_ctx):
    rows = x.shape[0]
    t = lax.broadcasted_iota(jnp.int32, x.shape, 0) // SUBLANES
    tl = jnp.where(is_ctx, t, t % GRID_W)
    last = jnp.where(is_ctx, TB - 1, GRID_W - 1)
    y = x * w4[2:3, :]
    xm2 = pltpu.roll(x, 2 * SUBLANES, 0)
    y = y + jnp.where(tl >= 2, xm2, 0.0) * w4[0:1, :]
    xm1 = pltpu.roll(x, SUBLANES, 0)
    y = y + jnp.where(tl >= 1, xm1, 0.0) * w4[1:2, :]
    xp1 = pltpu.roll(x, rows - SUBLANES, 0)
    y = y + jnp.where(tl < last, xp1, 0.0) * w4[3:4, :]
    return y


def _tile_cumsum(src_ref, dst_ref, rev):
    ntile = src_ref.shape[0] // SUBLANES
    run = None
    for i in range(ntile):
        r = ((ntile - 1 - i) if rev else i) * SUBLANES
        x = src_ref[r:r + SUBLANES, :]
        run = x if i % CHUNK == 0 else run + x
        dst_ref[r:r + SUBLANES, :] = run


def _pick_lane(x, lane):
    onehot = lax.broadcasted_iota(jnp.int32, x.shape, 1) == lane
    col = jnp.sum(jnp.where(onehot, x, 0.0), axis=-1, keepdims=True)
    return jnp.broadcast_to(col, x.shape)


def _lru_gate_kernel(xa_ref, cw_ref, w_ref, bias_ref, sp_ref, a_ref, b_ref, *, gc):
    is_ctx = pl.program_id(0) < gc
    u = _conv4(xa_ref[0], cw_ref[0], is_ctx)
    for d in range(2):
        r = _sigmoid(_mm(u, w_ref[d, 0, 0]) + bias_ref[d, 0, 0])
        i = _sigmoid(_mm(u, w_ref[d, 1, 0]) + bias_ref[d, 1, 0])
        log_a = -LRU_C * r * sp_ref[d, 0]
        a = jnp.exp(log_a)
        b = jnp.sqrt(-jnp.tanh(log_a) * (a * a + 1.0)) * (i * u)
        a_ref[d] = a
        b_ref[d] = b


def _lru_gates(z, conv_a, wbd, bias, sp, *, gc, nblk):
    rows = z.shape[1]
    ns = D_A // LANES
    out = jax.ShapeDtypeStruct((2, rows, D_A), f32)
    return pl.pallas_call(
        functools.partial(_lru_gate_kernel, gc=gc),
        grid=(nblk, ns),
        in_specs=[
            pl.BlockSpec((1, RB, LANES), lambda i, s: (E_XA + s, i, 0)),
            pl.BlockSpec((1, 4, LANES), lambda i, s: (s, 0, 0)),
            pl.BlockSpec((2, 2, 1, LANES, LANES), lambda i, s: (0, 0, s, 0, 0)),
            pl.BlockSpec((2, 2, 1, 1, LANES), lambda i, s: (0, 0, s, 0, 0)),
            pl.BlockSpec((2, 1, 1, LANES), lambda i, s: (0, s, 0, 0)),
        ],
        out_specs=[pl.BlockSpec((2, RB, LANES), lambda i, s: (0, i, s))] * 2,
        out_shape=[out, out],
        compiler_params=_cparams(("parallel", "parallel")),
        name="lru_gates",
    )(z, conv_a, wbd, bias, sp)


def _lru_scan_kernel(a_ref, b_ref, h0_ref, h_ref, fin_ref, carry_ref, *, gc, rev):
    s = pl.program_id(1)

    @pl.when(s < gc)
    def _():
        carry_ref[...] = jnp.zeros_like(carry_ref)

    @pl.when(s == gc)
    def _():
        carry_ref[...] = h0_ref[...]

    def body(i, h):
        t = (TB - 1 - i) if rev else i
        r = pl.multiple_of(t * SUBLANES, SUBLANES)
        h = a_ref[0, pl.ds(r, SUBLANES), :] * h + b_ref[0, pl.ds(r, SUBLANES), :]
        h_ref[pl.ds(r, SUBLANES), :] = h
        return h

    h = lax.fori_loop(0, TB, body, carry_ref[...], unroll=8)
    carry_ref[...] = h

    @pl.when(s < gc)
    def _():
        fin_ref[0] = h


def _lru_scan(a, b, h0, *, d, gc, nd, rev):
    rows = a.shape[1]
    nblk = gc + nd
    wc = 512
    blk = lambda c, s: _step_blk(s, gc, nd, rev)
    return pl.pallas_call(
        functools.partial(_lru_scan_kernel, gc=gc, rev=rev),
        grid=(D_A // wc, nblk),
        in_specs=[
            pl.BlockSpec((1, RB, wc), lambda c, s: (d, blk(c, s), c)),
            pl.BlockSpec((1, RB, wc), lambda c, s: (d, blk(c, s), c)),
            pl.BlockSpec((SUBLANES, wc), lambda c, s: (0, c)),
        ],
        out_specs=[
            pl.BlockSpec((RB, wc), lambda c, s: (blk(c, s), c)),
            pl.BlockSpec((1, SUBLANES, wc), lambda c, s: (jnp.minimum(s, gc - 1), 0, c)),
        ],
        out_shape=[
            jax.ShapeDtypeStruct((rows, D_A), f32),
            jax.ShapeDtypeStruct((gc, SUBLANES, D_A), f32),
        ],
        scratch_shapes=[pltpu.VMEM((SUBLANES, wc), f32)],
        compiler_params=_cparams(("parallel", "arbitrary")),
        name="lru_scan_bwd" if rev else "lru_scan_fwd",
    )(a, b, h0)


def _delta_kernel(q_ref, k_ref, v_ref, ba_ref, cw_ref, nega_ref, dtb_ref, s0_ref, o_ref, fin_ref,
                  qs_ref, ks_ref, vs_ref, bs_ref, la_ref, gs_ref, st_ref, *, gc, rev, d):
    h = pl.program_id(0)
    s = pl.program_id(1)
    is_ctx = s < gc

    @pl.when(is_ctx)
    def _():
        st_ref[...] = jnp.zeros_like(st_ref)

    @pl.when(s == gc)
    def _():
        st_ref[...] = s0_ref[0]

    def l2n(x):
        return x * lax.rsqrt(jnp.sum(x * x, axis=-1, keepdims=True) + EPS)

    qs_ref[...] = l2n(_silu(_conv4(q_ref[0], cw_ref[0, 0], is_ctx))) * (DK_B ** -0.5)
    ks_ref[...] = l2n(_silu(_conv4(k_ref[0], cw_ref[1, 0], is_ctx)))
    vs_ref[...] = _silu(_conv4(v_ref[0], cw_ref[2, 0], is_ctx))
    ba = ba_ref[0]
    bs_ref[...] = _pick_lane(_sigmoid(ba), d * H_B + h)
    la_ref[...] = _pick_lane(nega_ref[...] * jax.nn.softplus(ba + dtb_ref[...]), 2 * H_B + d * H_B + h)
    _tile_cumsum(la_ref, gs_ref, rev)

    ii = lax.broadcasted_iota(jnp.int32, (CHUNK, CHUNK), 0)
    jj = lax.broadcasted_iota(jnp.int32, (CHUNK, CHUNK), 1)
    incl = (jj >= ii) if rev else (jj <= ii)
    strict = (jj > ii) if rev else (jj < ii)
    eye = (ii == jj).astype(f32)
    last = 0 if rev else CHUNK - 1
    diag8 = (ii // 8) == (jj // 8)
    merge_masks = [((ii // (2 * sz)) == (jj // (2 * sz))) & ((ii // sz) != (jj // sz)) for sz in (8, 16, 32)]

    def chunk_body(ci, carry):
        c = (TB // CHUNK - 1 - ci) if rev else ci
        units = range(SUBLANES)
        rows = [pl.ds(c * CROWS + b, CHUNK, stride=SUBLANES) for b in units]
        q = [qs_ref[r, :] for r in rows]
        k = [ks_ref[r, :] for r in rows]
        v = [vs_ref[r, :] for r in rows]
        beta = [bs_ref[r, :] for r in rows]
        g = [gs_ref[r, :] for r in rows]
        kb = [k[b] * beta[b] for b in units]
        dec_incl, low = [], []
        for b in units:
            gsq = g[b][:, :CHUNK]
            g_row = jnp.sum(gsq * eye, axis=0, keepdims=True)
            diff = gsq - g_row
            dec = jnp.where(incl, jnp.exp(jnp.where(incl, diff, 0.0)), 0.0)
            dec_incl.append(dec)
            low.append(_mm_nt(kb[b], k[b]) * jnp.where(strict, dec, 0.0))
        ld = [jnp.where(diag8, low[b], 0.0) for b in units]
        ls = [_split(ld[b]) for b in units]
        d2 = [_mm3(ls[b], ls[b]) for b in units]
        d2s = [_split(d2[b]) for b in units]
        t_inv = [eye - ld[b] for b in units]
        ts = [_split(t_inv[b]) for b in units]
        prod, d4 = [], []
        for b in units:
            prod.append(_mm3(ts[b], d2s[b]))
            d4.append(_mm3(d2s[b], d2s[b]))
        t_inv = [t_inv[b] + prod[b] for b in units]
        t_inv = [t_inv[b] + _mm3(_split(t_inv[b]), _split(d4[b])) for b in units]
        for lvl in merge_masks:
            ts = [_split(t_inv[b]) for b in units]
            tl = [_mm3(ts[b], _split(jnp.where(lvl, low[b], 0.0))) for b in units]
            t_inv = [t_inv[b] - _mm3(_split(tl[b]), ts[b]) for b in units]
        eg = [jnp.exp(g[b]) for b in units]
        ts = [_split(t_inv[b]) for b in units]
        uw = [_mm3(ts[b], _split(jnp.concatenate([v[b] * beta[b], kb[b] * eg[b]], axis=1))) for b in units]
        a_qk = [_mm_nt(q[b], k[b]) * dec_incl[b] for b in units]
        g_last = [g[b][last:last + 1, :] for b in units]
        st = [st_ref[b] for b in units]
        v_new = [uw[b][:, :DV_B] - _mm(uw[b][:, DV_B:], st[b]) for b in units]
        o = [_mm(q[b] * eg[b], st[b]) + _mm(a_qk[b], v_new[b]) for b in units]
        for b in units:
            k_g = k[b] * jnp.exp(g_last[b] - g[b])
            st_ref[b] = st[b] * jnp.exp(g_last[b]) + _mm_tn(k_g, v_new[b])
            o_ref[0, rows[b], :] = o[b]
        return carry

    lax.fori_loop(0, TB // CHUNK, chunk_body, 0)

    @pl.when(is_ctx)
    def _():
        fin_ref[0, 0] = st_ref[...]


def _delta(z, conv_qkv, nega, dtb, s0, *, d, gc, nd, rev):
    rows = z.shape[1]
    nblk = gc + nd
    blk = lambda h, s: _step_blk(s, gc, nd, rev)
    slab = lambda off: pl.BlockSpec((1, RB, LANES), lambda h, s: (off + h, blk(h, s), 0))
    scr = pltpu.VMEM((RB, LANES), f32)
    return pl.pallas_call(
        functools.partial(_delta_kernel, gc=gc, rev=rev, d=d),
        grid=(H_B, nblk),
        in_specs=[
            slab(E_Q), slab(E_K), slab(E_V),
            pl.BlockSpec((1, RB, LANES), lambda h, s: (E_BA, blk(h, s), 0)),
            pl.BlockSpec((3, 1, 4, LANES), lambda h, s: (0, h, 0, 0)),
            pl.BlockSpec((1, LANES), lambda h, s: (0, 0)),
            pl.BlockSpec((1, LANES), lambda h, s: (0, 0)),
            pl.BlockSpec((1, SUBLANES, DK_B, DV_B), lambda h, s: (h, 0, 0, 0)),
        ],
        out_specs=[
            pl.BlockSpec((1, RB, LANES), lambda h, s: (h, blk(h, s), 0)),
            pl.BlockSpec((1, 1, SUBLANES, DK_B, DV_B), lambda h, s: (jnp.minimum(s, gc - 1), h, 0, 0, 0)),
        ],
        out_shape=[
            jax.ShapeDtypeStruct((H_B, rows, LANES), f32),
            jax.ShapeDtypeStruct((gc, H_B, SUBLANES, DK_B, DV_B), f32),
        ],
        scratch_shapes=[scr, scr, scr, scr, scr, scr, pltpu.VMEM((SUBLANES, DK_B, DV_B), f32)],
        compiler_params=_cparams(("parallel", "arbitrary")),
        name="delta_bwd" if rev else "delta_fwd",
    )(z, z, z, z, conv_qkv, nega, dtb, s0)


def _s5_kernel(u_ref, bt_ref, ct_ref, lam_ref, s0_ref, y_ref, fin_ref, sbuf_ref, carry_ref, *, gc, rev):
    s = pl.program_id(1)
    half = sbuf_ref.shape[1] // 2

    @pl.when(s < gc)
    def _():
        carry_ref[...] = jnp.zeros_like(carry_ref)

    @pl.when(s == gc)
    def _():
        carry_ref[...] = s0_ref[0]

    nsl = u_ref.shape[0]
    lr = jnp.broadcast_to(lam_ref[0, 0:1, :], (SUBLANES, half))
    li = jnp.broadcast_to(lam_ref[0, 1:2, :], (SUBLANES, half))
    nq = TB // S5_TQ
    qrows = S5_TQ * SUBLANES
    order = list(range(nq - 1, -1, -1)) if rev else list(range(nq))

    def project_in(qb):
        rs = slice(qb * qrows, (qb + 1) * qrows)
        sbuf_ref[rs, :] = _mm(jnp.concatenate([u_ref[j, rs, :] for j in range(nsl)], axis=1), bt_ref[0])

    def project_out(qb):
        rs = slice(qb * qrows, (qb + 1) * qrows)
        y = _mm(sbuf_ref[rs, :], ct_ref[0])
        for j in range(nsl):
            y_ref[j, rs, :] = y[:, j * LANES:(j + 1) * LANES]

    hr, hi = carry_ref[:, :half], carry_ref[:, half:]
    project_in(order[0])
    for n, qb in enumerate(order):
        if n + 1 < nq:
            project_in(order[n + 1])
        for i in range(S5_TQ):
            t = qb * S5_TQ + ((S5_TQ - 1 - i) if rev else i)
            rs = slice(t * SUBLANES, (t + 1) * SUBLANES)
            hr, hi = (lr * hr - li * hi + sbuf_ref[rs, :half], lr * hi + li * hr + sbuf_ref[rs, half:])
            sbuf_ref[rs, :half] = hr
            sbuf_ref[rs, half:] = hi
        project_out(qb)
    carry_ref[:, :half] = hr
    carry_ref[:, half:] = hi

    @pl.when(s < gc)
    def _():
        fin_ref[0, 0] = carry_ref[...]


S5_TQ = 64
S5_COLS = LANES
S5_GPC = S5_COLS // S5_GW
S5_SW = 2 * S5_GPC * S5_N


def _s5(z, bt, ct, lam, s0, *, gc, nd, rev):
    rows = z.shape[1]
    nblk = gc + nd
    ngt = D_C // S5_COLS
    nsl = S5_COLS // LANES
    sw = S5_SW
    blk = lambda g, s: _step_blk(s, gc, nd, rev)
    return pl.pallas_call(
        functools.partial(_s5_kernel, gc=gc, rev=rev),
        grid=(ngt, nblk),
        in_specs=[
            pl.BlockSpec((nsl, RB, LANES), lambda g, s: (O_U // nsl + g, blk(g, s), 0)),
            pl.BlockSpec((1, S5_COLS, sw), lambda g, s: (g, 0, 0)),
            pl.BlockSpec((1, sw, S5_COLS), lambda g, s: (g, 0, 0)),
            pl.BlockSpec((1, 2, sw // 2), lambda g, s: (g, 0, 0)),
            pl.BlockSpec((1, SUBLANES, sw), lambda g, s: (g, 0, 0)),
        ],
        out_specs=[
            pl.BlockSpec((nsl, RB, LANES), lambda g, s: (g, blk(g, s), 0)),
            pl.BlockSpec((1, 1, SUBLANES, sw), lambda g, s: (jnp.minimum(s, gc - 1), g, 0, 0)),
        ],
        out_shape=[
            jax.ShapeDtypeStruct((D_C // LANES, rows, LANES), f32),
            jax.ShapeDtypeStruct((gc, ngt, SUBLANES, sw), f32),
        ],
        scratch_shapes=[pltpu.VMEM((RB, sw), f32), pltpu.VMEM((SUBLANES, sw), f32)],
        compiler_params=_cparams(("parallel", "arbitrary")),
        name="s5_bwd" if rev else "s5_fwd",
    )(z, bt, ct, lam, s0)


def _s5_out_kernel(yf_ref, yb_ref, u_ref, d_ref, w_ref, b_ref, o_ref):
    ns = yf_ref.shape[0]
    yc = jnp.concatenate([yf_ref[s] + yb_ref[s] + d_ref[s] * u_ref[s] for s in range(ns)], axis=1)
    zc = jax.nn.gelu(yc)
    gate = _sigmoid(jnp.dot(zc.astype(bf16), w_ref[...], preferred_element_type=f32) + b_ref[...])
    o_ref[...] = (zc * gate).astype(o_ref.dtype)


def _s5_out(yf, yb, z, dskip, w_glu, b_glu, *, tm):
    ns, rows, _ = yf.shape
    return pl.pallas_call(
        _s5_out_kernel,
        grid=(rows // tm,),
        in_specs=[
            pl.BlockSpec((ns, tm, LANES), lambda i: (0, i, 0)),
            pl.BlockSpec((ns, tm, LANES), lambda i: (0, i, 0)),
            pl.BlockSpec((ns, tm, LANES), lambda i: (O_U // ns, i, 0)),
            pl.BlockSpec((ns, 1, LANES), lambda i: (0, 0, 0)),
            pl.BlockSpec((D_C, D_C), lambda i: (0, 0)),
            pl.BlockSpec((1, D_C), lambda i: (0, 0)),
        ],
        out_specs=pl.BlockSpec((tm, D_C), lambda i: (i, 0)),
        out_shape=jax.ShapeDtypeStruct((rows, D_C), bf16),
        compiler_params=_cparams(("parallel",)),
        name="s5_out",
    )(yf, yb, z, dskip, w_glu, b_glu)


def _gla_kernel(q_ref, k_ref, v_ref, glr_ref, wg_ref, bg_ref, s0_ref, o_ref, fin_ref,
                la_ref, gs_ref, st_ref, *, gc, rev):
    s = pl.program_id(1)
    is_ctx = s < gc

    @pl.when(is_ctx)
    def _():
        st_ref[...] = jnp.zeros_like(st_ref)

    @pl.when(s == gc)
    def _():
        st_ref[...] = s0_ref[0]

    la_ref[...] = jax.nn.log_sigmoid(_mm(glr_ref[0], wg_ref[0]) + bg_ref[0]) * (1.0 / GLA_TAU)
    _tile_cumsum(la_ref, gs_ref, rev)

    ii = lax.broadcasted_iota(jnp.int32, (SUB, CHUNK), 0)
    jj = lax.broadcasted_iota(jnp.int32, (SUB, CHUNK), 1)
    last = 0 if rev else CHUNK - 1

    def chunk_body(ci, carry):
        c = (TB // CHUNK - 1 - ci) if rev else ci
        units = range(SUBLANES)
        rows = [pl.ds(c * CROWS + b, CHUNK, stride=SUBLANES) for b in units]
        q = [q_ref[0, r, :] * (DK_D ** -0.5) for r in rows]
        k = [k_ref[0, r, :] for r in rows]
        v = [jnp.concatenate([v_ref[0, r, :], v_ref[1, r, :]], axis=1) for r in rows]
        g = [gs_ref[r, :] for r in rows]
        a_rows = [[] for _ in units]
        for blk in range(CHUNK // SUB):
            lo = blk * SUB
            first = lo + SUB - 1 if rev else lo
            causal = (jj >= ii + lo) if rev else (jj <= ii + lo)
            for b in units:
                g_first = g[b][first:first + 1, :]
                qt = q[b][lo:lo + SUB, :] * jnp.exp(g[b][lo:lo + SUB, :] - g_first)
                kt = k[b] * jnp.exp(g_first - g[b])
                a_rows[b].append(jnp.where(causal, _mm_nt(qt, kt), 0.0))
        a_qk = [jnp.concatenate(a_rows[b], axis=0) for b in units]
        g_last = [g[b][last:last + 1, :] for b in units]
        st = [st_ref[b] for b in units]
        o = [_mm_nt(q[b] * jnp.exp(g[b]), st[b]) + _mm(a_qk[b], v[b]) for b in units]
        for b in units:
            k_g = k[b] * jnp.exp(g_last[b] - g[b])
            st_ref[b] = st[b] * jnp.exp(g_last[b]) + _mm_tn(v[b], k_g)
            o_ref[0, rows[b], :] = o[b][:, :LANES]
            o_ref[1, rows[b], :] = o[b][:, LANES:]
        return carry

    lax.fori_loop(0, TB // CHUNK, chunk_body, 0)

    @pl.when(is_ctx)
    def _():
        fin_ref[0, 0] = st_ref[...]


def _gla(z, wg, bg, s0, *, gc, nd, rev):
    rows = z.shape[1]
    nblk = gc + nd
    nv = DV_D // LANES
    blk = lambda h, s: _step_blk(s, gc, nd, rev)
    scr = pltpu.VMEM((RB, LANES), f32)
    return pl.pallas_call(
        functools.partial(_gla_kernel, gc=gc, rev=rev),
        grid=(H_D, nblk),
        in_specs=[
            pl.BlockSpec((1, RB, LANES), lambda h, s: (O_Q + h, blk(h, s), 0)),
            pl.BlockSpec((1, RB, LANES), lambda h, s: (O_K + h, blk(h, s), 0)),
            pl.BlockSpec((nv, RB, LANES), lambda h, s: (O_V // nv + h, blk(h, s), 0)),
            pl.BlockSpec((1, RB, LANES), lambda h, s: (O_GLR, blk(h, s), 0)),
            pl.BlockSpec((1, LANES, LANES), lambda h, s: (h, 0, 0)),
            pl.BlockSpec((1, 1, LANES), lambda h, s: (h, 0, 0)),
            pl.BlockSpec((1, SUBLANES, DV_D, DK_D), lambda h, s: (h, 0, 0, 0)),
        ],
        out_specs=[
            pl.BlockSpec((nv, RB, LANES), lambda h, s: (h, blk(h, s), 0)),
            pl.BlockSpec((1, 1, SUBLANES, DV_D, DK_D), lambda h, s: (jnp.minimum(s, gc - 1), h, 0, 0, 0)),
        ],
        out_shape=[
            jax.ShapeDtypeStruct((H_D * nv, rows, LANES), f32),
            jax.ShapeDtypeStruct((gc, H_D, SUBLANES, DV_D, DK_D), f32),
        ],
        scratch_shapes=[scr, scr, pltpu.VMEM((SUBLANES, DV_D, DK_D), f32)],
        compiler_params=_cparams(("parallel", "arbitrary")),
        name="gla_bwd" if rev else "gla_fwd",
    )(z, z, z, z, wg, bg, s0)


TL = 64


def _rows_in_kernel(xp_ref, xs_ref, o_ref, *, nctx):
    def emit(ref):
        x = ref[0]
        o_ref[...] = jnp.swapaxes(x, 0, 1).reshape(TL * SUBLANES, x.shape[2])

    pl.when(pl.program_id(0) < nctx)(lambda: emit(xp_ref))
    pl.when(pl.program_id(0) >= nctx)(lambda: emit(xs_ref))


def _rows_in(x_prompt, x_sample):
    bp, tp, d = x_prompt.shape
    bs, ts, _ = x_sample.shape
    xp = x_prompt.reshape(bp // SUBLANES, SUBLANES, tp, d)
    xs = x_sample.reshape(bs // SUBLANES, SUBLANES, ts, d)
    per = tp // TL
    nctx = (bp // SUBLANES) * per
    nblk = nctx + ts // TL

    def ctx_idx(i):
        j = jnp.minimum(i, nctx - 1)
        return (j // per, 0, j % per, 0)

    return pl.pallas_call(
        functools.partial(_rows_in_kernel, nctx=nctx),
        grid=(nblk,),
        in_specs=[
            pl.BlockSpec((1, SUBLANES, TL, d), ctx_idx),
            pl.BlockSpec((1, SUBLANES, TL, d), lambda i: (0, 0, jnp.maximum(i - nctx, 0), 0)),
        ],
        out_specs=pl.BlockSpec((TL * SUBLANES, d), lambda i: (i, 0)),
        out_shape=jax.ShapeDtypeStruct((bp * tp + bs * ts, d), x_prompt.dtype),
        compiler_params=_cparams(("arbitrary",)),
        name="rows_in",
    )(xp, xs)


def _rows_out_kernel(x_ref, yp_ref, ys_ref, *, nctx):
    x = x_ref[...]
    y = jnp.swapaxes(x.reshape(TL, SUBLANES, x.shape[1]), 0, 1)

    @pl.when(pl.program_id(0) < nctx)
    def _():
        yp_ref[0] = y

    @pl.when(pl.program_id(0) >= nctx)
    def _():
        ys_ref[0] = y


def _rows_out(x, bp, tp, bs, ts):
    d = x.shape[1]
    per = tp // TL
    nctx = (bp // SUBLANES) * per
    nblk = nctx + ts // TL

    def ctx_idx(i):
        j = jnp.minimum(i, nctx - 1)
        return (j // per, 0, j % per, 0)

    yp, ys = pl.pallas_call(
        functools.partial(_rows_out_kernel, nctx=nctx),
        grid=(nblk,),
        in_specs=[pl.BlockSpec((TL * SUBLANES, d), lambda i: (i, 0))],
        out_specs=[
            pl.BlockSpec((1, SUBLANES, TL, d), ctx_idx),
            pl.BlockSpec((1, SUBLANES, TL, d), lambda i: (0, 0, jnp.maximum(i - nctx, 0), 0)),
        ],
        out_shape=[
            jax.ShapeDtypeStruct((bp // SUBLANES, SUBLANES, tp, d), x.dtype),
            jax.ShapeDtypeStruct((bs // SUBLANES, SUBLANES, ts, d), x.dtype),
        ],
        compiler_params=_cparams(("arbitrary",)),
        name="rows_out",
    )(x)
    return yp.reshape(bp, tp, d), ys.reshape(bs, ts, d)


def _pad_cols(w, n):
    return jnp.pad(w, ((0, 0), (0, n - w.shape[1])))


def _unit_states(fin, b):
    fin = jnp.moveaxis(fin, b, 1)
    return fin.reshape((fin.shape[0] * SUBLANES,) + fin.shape[2:])


def kernel(x_prompt, x_sample, state_lru, state_delta, state_s5_re, state_s5_im, state_gla, c, c_ctx, w_ada, b_ada, norm_g, w_up, w_down, w_in_e, w_out_e, conv_a, lru_wa, lru_ba, lru_wx, lru_bx, lru_lambda, conv_qkv, dn_a_log, dn_dt_bias, dn_norm_g, w_in_o, w_out_o, s5_lam_re, s5_lam_im, s5_log_step, s5_b_re, s5_b_im, s5_c_re, s5_c_im, s5_d, s5_w_glu, s5_b_glu, gla_wg2, gla_bg, gla_norm_g):
    bp, tp, d = x_prompt.shape
    bs, ts, _ = x_sample.shape
    assert tp == TB and bp % SUBLANES == 0 and bs == SUBLANES and ts % TB == 0 and d == D_MODEL
    depth = w_ada.shape[0]
    gc = bp // SUBLANES
    nd = ts // TB
    nblk = gc + nd
    rows_ctx = bp * tp
    seq = dict(gc=gc, nd=nd)

    x = _rows_in(x_prompt, x_sample)

    cvec = jnp.concatenate([c, jnp.broadcast_to(c_ctx[None], (SUBLANES, d))], axis=0)
    mod = _ada(cvec, w_ada, b_ada)
    mod = jnp.stack([mod[:, SUBLANES:], mod[:, :SUBLANES]], axis=1)
    mods = [[mod[l, :, :, i * d:(i + 1) * d] for i in range(6)] for l in range(depth)]

    fin_lru, fin_delta, fin_re, fin_im, fin_gla = [], [], [], [], []
    for l in range(depth):
        sh1, sc1, gt1, sh2, sc2, gt2 = mods[l]
        if l % 2 == 0:
            e = l // 2
            w_in = _pad_cols(w_in_e[e], E_SLABS * LANES).astype(bf16)
            z = _nmm(x, norm_g[l, 0], sh1, sc1, w_in, rows_ctx=rows_ctx, tm=256, tn=10 * LANES, slab=True)
            ns = D_A // LANES
            wg = jnp.stack([lru_wa[e], lru_wx[e]], axis=1).reshape(2, 2, ns, 2, LRU_BW, LRU_BW)
            wbd = jnp.zeros((2, 2, ns, LANES, LANES), f32)
            wbd = wbd.at[:, :, :, :LRU_BW, :LRU_BW].set(wg[:, :, :, 0]).at[:, :, :, LRU_BW:, LRU_BW:].set(wg[:, :, :, 1])
            bias = jnp.stack([lru_ba[e], lru_bx[e]], axis=1).reshape(2, 2, ns, 1, LANES)
            sp = jax.nn.softplus(-lru_lambda[e]).reshape(2, ns, 1, LANES)
            cwa = conv_a[e].reshape(4, ns, LANES).transpose(1, 0, 2)
            a, b = _lru_gates(z, cwa, wbd.astype(bf16), bias, sp, gc=gc, nblk=nblk)
            h_f, f_f = _lru_scan(a, b, state_lru[:, e, 0], d=0, rev=False, **seq)
            h_b, f_b = _lru_scan(a, b, state_lru[:, e, 1], d=1, rev=True, **seq)
            fin_lru.append(jnp.stack([_unit_states(f_f, 1), _unit_states(f_b, 1)], axis=1))
            cq = conv_qkv[e].reshape(4, 3, H_B, LANES).transpose(1, 2, 0, 3)
            lane_pad = lambda v: jnp.pad(v.reshape(1, 2 * H_B), ((0, 0), (2 * H_B, LANES - 4 * H_B)))
            nega = lane_pad(-jnp.exp(dn_a_log[e]))
            dtb = lane_pad(dn_dt_bias[e])
            s0 = state_delta[:, e].transpose(1, 2, 0, 3, 4)
            o_f, s_f = _delta(z, cq, nega, dtb, s0[0], d=0, rev=False, **seq)
            o_b, s_b = _delta(z, cq, nega, dtb, s0[1], d=1, rev=True, **seq)
            fin_delta.append(jnp.stack([_unit_states(s_f, 2), _unit_states(s_b, 2)], axis=1))
            x = _mixout_even(h_f, h_b, o_f, o_b, z, dn_norm_g[e], w_out_e[e].astype(bf16), x, norm_g[l, 1], gt1,
                             rows_ctx=rows_ctx, tm=512)
        else:
            o = l // 2
            w_in = _pad_cols(w_in_o[o], O_SLABS * LANES).astype(bf16)
            z = _nmm(x, norm_g[l, 0], sh1, sc1, w_in, rows_ctx=rows_ctx, tm=256, tn=11 * LANES, slab=True)
            ngt = D_C // S5_COLS
            gps = S5_GPC
            lam = lax.complex(s5_lam_re[o], s5_lam_im[o])
            lam_bar = jnp.exp(lam * jnp.exp(s5_log_step[o])[..., None])
            b_bar = ((lam_bar - 1.0) / lam)[..., None] * lax.complex(s5_b_re[o], s5_b_im[o])
            eye_g = jnp.eye(gps, dtype=f32)

            def bmat(p):
                p = p.reshape(2, ngt, gps, S5_N, S5_GW)
                return jnp.einsum('dtgnw,gh->dtgwhn', p, eye_g).reshape(2, ngt, S5_COLS, gps * S5_N)

            def cmat(p):
                p = p.reshape(2, ngt, gps, S5_GW, S5_N)
                return jnp.einsum('dtgwn,gh->dtgnhw', p, eye_g).reshape(2, ngt, gps * S5_N, S5_COLS)

            bt = jnp.concatenate([bmat(b_bar.real), bmat(b_bar.imag)], axis=-1).astype(bf16)
            ct = jnp.concatenate([cmat(s5_c_re[o]), -cmat(s5_c_im[o])], axis=-2).astype(bf16)
            lamv = jnp.stack([lam_bar.real, lam_bar.imag], axis=2).reshape(2, ngt, gps, 2, S5_N)
            lamv = lamv.transpose(0, 1, 3, 2, 4).reshape(2, ngt, 2, gps * S5_N)

            def s5_state(sre, sim):
                f = lambda p: p.reshape(SUBLANES, ngt, gps * S5_N).transpose(1, 0, 2)
                return jnp.concatenate([f(sre), f(sim)], axis=-1)

            y_f, c_f = _s5(z, bt[0], ct[0], lamv[0], s5_state(state_s5_re[:, o, 0], state_s5_im[:, o, 0]), rev=False, **seq)
            y_bk, c_b = _s5(z, bt[1], ct[1], lamv[1], s5_state(state_s5_re[:, o, 1], state_s5_im[:, o, 1]), rev=True, **seq)
            y_c = _s5_out(y_f, y_bk, z, s5_d[o].reshape(D_C // LANES, 1, LANES), s5_w_glu[o].astype(bf16),
                          s5_b_glu[o].reshape(1, D_C), tm=512)

            def s5_fin(cf):
                cf = cf.transpose(0, 2, 1, 3).reshape(gc * SUBLANES, ngt, 2, gps, S5_N)
                return (cf[:, :, 0].reshape(gc * SUBLANES, S5_G, S5_N), cf[:, :, 1].reshape(gc * SUBLANES, S5_G, S5_N))

            (rf, imf), (rb, imb) = s5_fin(c_f), s5_fin(c_b)
            fin_re.append(jnp.stack([rf, rb], axis=1))
            fin_im.append(jnp.stack([imf, imb], axis=1))
            wgp = jnp.zeros((2, H_D, LANES, LANES), f32)
            wg2 = gla_wg2[o].reshape(2, GLA_RANK, H_D, DK_D).transpose(0, 2, 1, 3)
            wgp = wgp.at[0, :, :GLA_RANK].set(wg2[0]).at[1, :, GLA_RANK:2 * GLA_RANK].set(wg2[1]).astype(bf16)
            bgp = gla_bg[o].reshape(2, H_D, 1, DK_D)
            g0 = state_gla[:, o].transpose(1, 2, 0, 4, 3)
            o_f, g_f = _gla(z, wgp[0], bgp[0], g0[0], rev=False, **seq)
            o_b, g_b = _gla(z, wgp[1], bgp[1], g0[1], rev=True, **seq)
            gfin = lambda gf: jnp.swapaxes(_unit_states(gf, 2), -1, -2)
            fin_gla.append(jnp.stack([gfin(g_f), gfin(g_b)], axis=1))
            x = _mixout_odd(y_c, o_f, o_b, z, gla_norm_g[o], w_out_o[o].astype(bf16), x, norm_g[l, 1], gt1,
                            rows_ctx=rows_ctx, tm=512)

        hmid = _nmm(x, norm_g[l, 2], sh2, sc2, w_up[l].astype(bf16), rows_ctx=rows_ctx, tm=256, tn=2048,
                    relu2=True, out_dtype=bf16)
        x = _mmres(hmid, w_down[l].astype(bf16), x, norm_g[l, 3], gt2, rows_ctx=rows_ctx, tm=256)

    y_prompt, y_sample = _rows_out(x, bp, tp, bs, ts)
    return (y_prompt, y_sample, jnp.stack(fin_lru, axis=1), jnp.stack(fin_delta, axis=1),
            jnp.stack(fin_re, axis=1), jnp.stack(fin_im, axis=1), jnp.stack(fin_gla, axis=1))
```

```python
import functools
import math

import jax
import jax.numpy as jnp
from jax import lax
from jax.experimental import pallas as pl
from jax.experimental.pallas import tpu as pltpu

f32 = jnp.float32
bf16 = jnp.bfloat16

LANES = 128
SUBLANES = 8
VMEM_LIMIT_BYTES = 56 * 1024 * 1024

D_MODEL = 2048
D_FF = 4 * D_MODEL
GRID_W = 64
CHUNK = 64
EPS = 1e-6
D_A = D_MODEL // 2
LRU_BLOCKS = 16
LRU_BW = D_A // LRU_BLOCKS
LRU_C = 8.0
H_B = 8
DK_B = 128
DV_B = 128
D_C = D_MODEL // 2
S5_GW = 16
S5_G = D_C // S5_GW
S5_N = 64
H_D = 4
DK_D = 128
DV_D = 256
GLA_RANK = 16
GLA_TAU = 16.0

TB = 256
RB = TB * SUBLANES
CROWS = CHUNK * SUBLANES
SUB = 16

E_XA, E_GA, E_Q, E_K, E_V, E_GO, E_BA = 0, 8, 16, 24, 32, 40, 48
E_SLABS = 50
O_U, O_Q, O_K, O_V, O_R, O_GLR = 0, 8, 12, 16, 24, 32
O_SLABS = 33


def _cparams(sem):
    return pltpu.CompilerParams(dimension_semantics=sem, vmem_limit_bytes=VMEM_LIMIT_BYTES)


def _mm(a, b):
    return jnp.dot(a.astype(bf16), b.astype(bf16), preferred_element_type=f32)


def _mm_nt(a, b):
    return lax.dot_general(a.astype(bf16), b.astype(bf16), (((1,), (1,)), ((), ())), preferred_element_type=f32)


def _mm_tn(a, b):
    return lax.dot_general(a.astype(bf16), b.astype(bf16), (((0,), (0,)), ((), ())), preferred_element_type=f32)


def _split(x):
    hi = x.astype(bf16)
    return hi, (x - hi.astype(f32)).astype(bf16)


def _mm3(a, b):
    dot = functools.partial(jnp.dot, preferred_element_type=f32)
    return dot(a[0], b[0]) + (dot(a[0], b[1]) + dot(a[1], b[0]))


def _sigmoid(x):
    return 0.5 * jnp.tanh(0.5 * x) + 0.5


def _silu(x):
    t = 0.5 * x
    return t + t * jnp.tanh(t)


def _ada_kernel(c_ref, w_ref, b_ref, o_ref):
    a = _silu(c_ref[...])
    o_ref[0] = _mm(a, w_ref[0]) + b_ref[0]


def _ada(cvec, w_ada, b_ada):
    depth, d, n = w_ada.shape
    tn = 1024
    return pl.pallas_call(
        _ada_kernel,
        grid=(depth, n // tn),
        in_specs=[
            pl.BlockSpec(cvec.shape, lambda l, j: (0, 0)),
            pl.BlockSpec((1, d, tn), lambda l, j: (l, 0, j)),
            pl.BlockSpec((1, 1, tn), lambda l, j: (l, 0, j)),
        ],
        out_specs=pl.BlockSpec((1, cvec.shape[0], tn), lambda l, j: (l, 0, j)),
        out_shape=jax.ShapeDtypeStruct((depth, cvec.shape[0], n), f32),
        compiler_params=_cparams(("parallel", "parallel")),
        name="ada",
    )(cvec, w_ada, b_ada.reshape(depth, 1, n))


def _nmm_kernel(x_ref, g_ref, sh_ref, sc_ref, w_ref, o_ref, *, relu2, slab, tn):
    x = x_ref[...]
    tm, d = x.shape
    y = x * lax.rsqrt(jnp.mean(x * x, axis=-1, keepdims=True) + EPS)
    gain = g_ref[...] * (1.0 + sc_ref[0])
    h = (y.reshape(tm // SUBLANES, SUBLANES, d) * gain[None] + sh_ref[0][None]).reshape(tm, d).astype(bf16)
    for c in range(w_ref.shape[1] // tn):
        acc = jnp.dot(h, w_ref[:, c * tn:(c + 1) * tn], preferred_element_type=f32)
        if relu2:
            acc = jnp.square(jnp.maximum(acc, 0.0))
        if slab:
            for s in range(tn // LANES):
                o_ref[c * (tn // LANES) + s] = acc[:, s * LANES:(s + 1) * LANES].astype(o_ref.dtype)
        else:
            o_ref[:, c * tn:(c + 1) * tn] = acc.astype(o_ref.dtype)


def _nmm(x, g, sh, sc, w, *, rows_ctx, tm, tn, relu2=False, slab=False, out_dtype=f32):
    rows, d = x.shape
    n = w.shape[1]
    assert n % tn == 0
    grp = lambda i: ((i * tm >= rows_ctx).astype(jnp.int32), 0, 0)
    if slab:
        out_shape = jax.ShapeDtypeStruct((n // LANES, rows, LANES), out_dtype)
        out_spec = pl.BlockSpec((n // LANES, tm, LANES), lambda i: (0, i, 0))
    else:
        out_shape = jax.ShapeDtypeStruct((rows, n), out_dtype)
        out_spec = pl.BlockSpec((tm, n), lambda i: (i, 0))
    return pl.pallas_call(
        functools.partial(_nmm_kernel, relu2=relu2, slab=slab, tn=tn),
        grid=(rows // tm,),
        in_specs=[
            pl.BlockSpec((tm, d), lambda i: (i, 0)),
            pl.BlockSpec((1, d), lambda i: (0, 0)),
            pl.BlockSpec((1, SUBLANES, d), grp),
            pl.BlockSpec((1, SUBLANES, d), grp),
            pl.BlockSpec((d, n), lambda i: (0, 0), pipeline_mode=pl.Buffered(1)),
        ],
        out_specs=out_spec,
        out_shape=out_shape,
        compiler_params=_cparams(("parallel",)),
        name="nmm",
    )(x, g.reshape(1, d), sh, sc, w)


def _mmres_kernel(a_ref, w_ref, x_ref, g_ref, gt_ref, o_ref):
    y = jnp.dot(a_ref[...], w_ref[...], preferred_element_type=f32)
    _gated_residual(y, x_ref, g_ref, gt_ref, o_ref)


def _gated_residual(y, x_ref, g_ref, gt_ref, o_ref):
    tm, d = y.shape
    yn = y * lax.rsqrt(jnp.mean(y * y, axis=-1, keepdims=True) + EPS) * g_ref[...]
    o = x_ref[...].reshape(tm // SUBLANES, SUBLANES, d) + gt_ref[0][None] * yn.reshape(tm // SUBLANES, SUBLANES, d)
    o_ref[...] = o.reshape(tm, d)


def _headnorm_gate(o_slabs, gain, gate_slabs):
    width = len(o_slabs) * LANES
    ms = sum(jnp.sum(o * o, axis=-1, keepdims=True) for o in o_slabs) * (1.0 / width)
    inv = lax.rsqrt(ms + EPS)
    return [(o * inv * gain[j] * _silu(r)).astype(bf16) for j, (o, r) in enumerate(zip(o_slabs, gate_slabs))]


def _mixout_even_kernel(hf_ref, hb_ref, ga_ref, of_ref, ob_ref, go_ref, dg_ref, w1_ref, w2_ref,
                        x_ref, g_ref, gt_ref, o_ref):
    ns = ga_ref.shape[0]
    ga = jnp.concatenate([ga_ref[s] for s in range(ns)], axis=1)
    y_a = ((hf_ref[...] + hb_ref[...]) * jax.nn.gelu(ga)).astype(bf16)
    y = jnp.dot(y_a, w1_ref[...], preferred_element_type=f32)
    y_b = [_headnorm_gate([of_ref[h] + ob_ref[h]], [dg_ref[...]], [go_ref[h]])[0] for h in range(of_ref.shape[0])]
    y = y + jnp.dot(jnp.concatenate(y_b, axis=1), w2_ref[...], preferred_element_type=f32)
    _gated_residual(y, x_ref, g_ref, gt_ref, o_ref)


def _mixout_odd_kernel(yc_ref, of_ref, ob_ref, r_ref, dg_ref, w1_ref, w2_ref, x_ref, g_ref, gt_ref, o_ref):
    nv = DV_D // LANES
    y = jnp.dot(yc_ref[...], w1_ref[...], preferred_element_type=f32)
    y_d = []
    for h in range(of_ref.shape[0] // nv):
        sl = range(h * nv, (h + 1) * nv)
        y_d += _headnorm_gate([of_ref[j] + ob_ref[j] for j in sl], [dg_ref[j] for j in range(nv)], [r_ref[j] for j in sl])
    y = y + jnp.dot(jnp.concatenate(y_d, axis=1), w2_ref[...], preferred_element_type=f32)
    _gated_residual(y, x_ref, g_ref, gt_ref, o_ref)


def _mixout_specs(w, x, g, gt, rows_ctx, tm):
    kh = w.shape[0] // 2
    d = w.shape[1]
    grp = lambda i: ((i * tm >= rows_ctx).astype(jnp.int32), 0, 0)
    specs = [
        pl.BlockSpec((kh, d), lambda i: (0, 0), pipeline_mode=pl.Buffered(1)),
        pl.BlockSpec((kh, d), lambda i: (1, 0), pipeline_mode=pl.Buffered(1)),
        pl.BlockSpec((tm, d), lambda i: (i, 0)),
        pl.BlockSpec((1, d), lambda i: (0, 0)),
        pl.BlockSpec((1, SUBLANES, d), grp),
    ]
    return specs, (w, w, x, g.reshape(1, d), gt)


def _mixout_even(hf, hb, of, ob, z, dn_g, w, x, g, gt, *, rows_ctx, tm):
    rows, d = x.shape
    ns = D_A // LANES
    slabs = lambda off: pl.BlockSpec((ns, tm, LANES), lambda i: (off // ns, i, 0))
    tail_specs, tail_args = _mixout_specs(w, x, g, gt, rows_ctx, tm)
    return pl.pallas_call(
        _mixout_even_kernel,
        grid=(rows // tm,),
        in_specs=[
            pl.BlockSpec((tm, D_A), lambda i: (i, 0)),
            pl.BlockSpec((tm, D_A), lambda i: (i, 0)),
            slabs(E_GA), slabs(0), slabs(0), slabs(E_GO),
            pl.BlockSpec((1, LANES), lambda i: (0, 0)),
        ] + tail_specs,
        out_specs=pl.BlockSpec((tm, d), lambda i: (i, 0)),
        out_shape=jax.ShapeDtypeStruct((rows, d), f32),
        compiler_params=_cparams(("parallel",)),
        name="mixout_even",
    )(hf, hb, z, of, ob, z, dn_g.reshape(1, LANES), *tail_args)


def _mixout_odd(y_c, of, ob, z, gla_g, w, x, g, gt, *, rows_ctx, tm):
    rows, d = x.shape
    ns = H_D * DV_D // LANES
    nv = DV_D // LANES
    slabs = lambda off: pl.BlockSpec((ns, tm, LANES), lambda i: (off // ns, i, 0))
    tail_specs, tail_args = _mixout_specs(w, x, g, gt, rows_ctx, tm)
    return pl.pallas_call(
        _mixout_odd_kernel,
        grid=(rows // tm,),
        in_specs=[
            pl.BlockSpec((tm, D_C), lambda i: (i, 0)),
            slabs(0), slabs(0), slabs(O_R),
            pl.BlockSpec((nv, 1, LANES), lambda i: (0, 0, 0)),
        ] + tail_specs,
        out_specs=pl.BlockSpec((tm, d), lambda i: (i, 0)),
        out_shape=jax.ShapeDtypeStruct((rows, d), f32),
        compiler_params=_cparams(("parallel",)),
        name="mixout_odd",
    )(y_c, of, ob, z, gla_g.reshape(nv, 1, LANES), *tail_args)


def _mmres(a, w, x, g, gt, *, rows_ctx, tm):
    rows, kdim = a.shape
    d = w.shape[1]
    grp = lambda i: ((i * tm >= rows_ctx).astype(jnp.int32), 0, 0)
    return pl.pallas_call(
        _mmres_kernel,
        grid=(rows // tm,),
        in_specs=[
            pl.BlockSpec((tm, kdim), lambda i: (i, 0)),
            pl.BlockSpec((kdim, d), lambda i: (0, 0), pipeline_mode=pl.Buffered(1)),
            pl.BlockSpec((tm, d), lambda i: (i, 0)),
            pl.BlockSpec((1, d), lambda i: (0, 0)),
            pl.BlockSpec((1, SUBLANES, d), grp),
        ],
        out_specs=pl.BlockSpec((tm, d), lambda i: (i, 0)),
        out_shape=jax.ShapeDtypeStruct((rows, d), f32),
        compiler_params=_cparams(("parallel",)),
        name="mmres",
    )(a, w, x, g.reshape(1, d), gt)


def _step_blk(s, gc, nd, rev):
    if not rev:
        return s
    return jnp.where(s < gc, s, 2 * gc + nd - 1 - s)


def _conv4(x, w4, is_ctx):
    rows = x.shape[0]
    t = lax.broadcasted_iota(jnp.int32, x.shape, 0) // SUBLANES
    tl = jnp.where(is_ctx, t, t % GRID_W)
    last = jnp.where(is_ctx, TB - 1, GRID_W - 1)
    y = x * w4[2:3, :]
    xm2 = pltpu.roll(x, 2 * SUBLANES, 0)
    y = y + jnp.where(tl >= 2, xm2, 0.0) * w4[0:1, :]
    xm1 = pltpu.roll(x, SUBLANES, 0)
    y = y + jnp.where(tl >= 1, xm1, 0.0) * w4[1:2, :]
    xp1 = pltpu.roll(x, rows - SUBLANES, 0)
    y = y + jnp.where(tl < last, xp1, 0.0) * w4[3:4, :]
    return y


def _tile_cumsum(src_ref, dst_ref, rev):
    ntile = src_ref.shape[0] // SUBLANES
    run = None
    for i in range(ntile):
        r = ((ntile - 1 - i) if rev else i) * SUBLANES
        x = src_ref[r:r + SUBLANES, :]
        run = x if i % CHUNK == 0 else run + x
        dst_ref[r:r + SUBLANES, :] = run


def _pick_lane(x, lane):
    onehot = lax.broadcasted_iota(jnp.int32, x.shape, 1) == lane
    col = jnp.sum(jnp.where(onehot, x, 0.0), axis=-1, keepdims=True)
    return jnp.broadcast_to(col, x.shape)


def _lru_gate_kernel(xa_ref, cw_ref, w_ref, bias_ref, sp_ref, a_ref, b_ref, *, gc):
    is_ctx = pl.program_id(0) < gc
    u = _conv4(xa_ref[0], cw_ref[0], is_ctx)
    for d in range(2):
        r = _sigmoid(_mm(u, w_ref[d, 0, 0]) + bias_ref[d, 0, 0])
        i = _sigmoid(_mm(u, w_ref[d, 1, 0]) + bias_ref[d, 1, 0])
        log_a = -LRU_C * r * sp_ref[d, 0]
        a = jnp.exp(log_a)
        b = jnp.sqrt(-jnp.tanh(log_a) * (a * a + 1.0)) * (i * u)
        a_ref[d] = a
        b_ref[d] = b


def _lru_gates(z, conv_a, wbd, bias, sp, *, gc, nblk):
    rows = z.shape[1]
    ns = D_A // LANES
    out = jax.ShapeDtypeStruct((2, rows, D_A), f32)
    return pl.pallas_call(
        functools.partial(_lru_gate_kernel, gc=gc),
        grid=(nblk, ns),
        in_specs=[
            pl.BlockSpec((1, RB, LANES), lambda i, s: (E_XA + s, i, 0)),
            pl.BlockSpec((1, 4, LANES), lambda i, s: (s, 0, 0)),
            pl.BlockSpec((2, 2, 1, LANES, LANES), lambda i, s: (0, 0, s, 0, 0)),
            pl.BlockSpec((2, 2, 1, 1, LANES), lambda i, s: (0, 0, s, 0, 0)),
            pl.BlockSpec((2, 1, 1, LANES), lambda i, s: (0, s, 0, 0)),
        ],
        out_specs=[pl.BlockSpec((2, RB, LANES), lambda i, s: (0, i, s))] * 2,
        out_shape=[out, out],
        compiler_params=_cparams(("parallel", "parallel")),
        name="lru_gates",
    )(z, conv_a, wbd, bias, sp)


def _lru_scan_kernel(a_ref, b_ref, h0_ref, h_ref, fin_ref, carry_ref, *, gc, rev):
    s = pl.program_id(1)

    @pl.when(s < gc)
    def _():
        carry_ref[...] = jnp.zeros_like(carry_ref)

    @pl.when(s == gc)
    def _():
        carry_ref[...] = h0_ref[...]

    def body(i, h):
        t = (TB - 1 - i) if rev else i
        r = pl.multiple_of(t * SUBLANES, SUBLANES)
        h = a_ref[0, pl.ds(r, SUBLANES), :] * h + b_ref[0, pl.ds(r, SUBLANES), :]
        h_ref[pl.ds(r, SUBLANES), :] = h
        return h

    h = lax.fori_loop(0, TB, body, carry_ref[...], unroll=8)
    carry_ref[...] = h

    @pl.when(s < gc)
    def _():
        fin_ref[0] = h


def _lru_scan(a, b, h0, *, d, gc, nd, rev):
    rows = a.shape[1]
    nblk = gc + nd
    wc = 512
    blk = lambda c, s: _step_blk(s, gc, nd, rev)
    return pl.pallas_call(
        functools.partial(_lru_scan_kernel, gc=gc, rev=rev),
        grid=(D_A // wc, nblk),
        in_specs=[
            pl.BlockSpec((1, RB, wc), lambda c, s: (d, blk(c, s), c)),
            pl.BlockSpec((1, RB, wc), lambda c, s: (d, blk(c, s), c)),
            pl.BlockSpec((SUBLANES, wc), lambda c, s: (0, c)),
        ],
        out_specs=[
            pl.BlockSpec((RB, wc), lambda c, s: (blk(c, s), c)),
            pl.BlockSpec((1, SUBLANES, wc), lambda c, s: (jnp.minimum(s, gc - 1), 0, c)),
        ],
        out_shape=[
            jax.ShapeDtypeStruct((rows, D_A), f32),
            jax.ShapeDtypeStruct((gc, SUBLANES, D_A), f32),
        ],
        scratch_shapes=[pltpu.VMEM((SUBLANES, wc), f32)],
        compiler_params=_cparams(("parallel", "arbitrary")),
        name="lru_scan_bwd" if rev else "lru_scan_fwd",
    )(a, b, h0)


def _delta_prep_kernel(q_ref, k_ref, v_ref, ba_ref, cw_ref, nega_ref, dtb_ref, qn_ref, kn_ref, vn_ref, gate_ref, *, gc):
    is_ctx = pl.program_id(0) < gc

    def l2n(x):
        return x * lax.rsqrt(jnp.sum(x * x, axis=-1, keepdims=True) + EPS)

    qn_ref[0] = l2n(_silu(_conv4(q_ref[0], cw_ref[0, 0], is_ctx))) * (DK_B ** -0.5)
    kn_ref[0] = l2n(_silu(_conv4(k_ref[0], cw_ref[1, 0], is_ctx)))
    vn_ref[0] = _silu(_conv4(v_ref[0], cw_ref[2, 0], is_ctx))

    @pl.when(pl.program_id(1) == 0)
    def _():
        ba = ba_ref[0]
        lane = lax.broadcasted_iota(jnp.int32, ba.shape, 1)
        gate_ref[...] = jnp.where(lane < 2 * H_B, _sigmoid(ba), nega_ref[...] * jax.nn.softplus(ba + dtb_ref[...]))


def _delta_prep(z, conv_qkv, nega, dtb, *, gc, nblk):
    rows = z.shape[1]
    slab = lambda off: pl.BlockSpec((1, RB, LANES), lambda i, h: (off + h, i, 0))
    out = jax.ShapeDtypeStruct((H_B, rows, LANES), f32)
    return pl.pallas_call(
        functools.partial(_delta_prep_kernel, gc=gc),
        grid=(nblk, H_B),
        in_specs=[
            slab(E_Q), slab(E_K), slab(E_V),
            pl.BlockSpec((1, RB, LANES), lambda i, h: (E_BA, i, 0)),
            pl.BlockSpec((3, 1, 4, LANES), lambda i, h: (0, h, 0, 0)),
            pl.BlockSpec((1, LANES), lambda i, h: (0, 0)),
            pl.BlockSpec((1, LANES), lambda i, h: (0, 0)),
        ],
        out_specs=[
            pl.BlockSpec((1, RB, LANES), lambda i, h: (h, i, 0)),
            pl.BlockSpec((1, RB, LANES), lambda i, h: (h, i, 0)),
            pl.BlockSpec((1, RB, LANES), lambda i, h: (h, i, 0)),
            pl.BlockSpec((RB, LANES), lambda i, h: (i, 0)),
        ],
        out_shape=[out, out, out, jax.ShapeDtypeStruct((rows, LANES), f32)],
        compiler_params=_cparams(("parallel", "arbitrary")),
        name="delta_prep",
    )(z, z, z, z, conv_qkv, nega, dtb)


def _delta_kernel(qs_ref, ks_ref, vs_ref, gate_ref, s0_ref, o_ref, fin_ref,
                  bs_ref, la_ref, gs_ref, st_ref, *, gc, rev, d):
    h = pl.program_id(0)
    s = pl.program_id(1)
    is_ctx = s < gc

    @pl.when(is_ctx)
    def _():
        st_ref[...] = jnp.zeros_like(st_ref)

    @pl.when(s == gc)
    def _():
        st_ref[...] = s0_ref[0]

    gates = gate_ref[...]
    bs_ref[...] = _pick_lane(gates, d * H_B + h)
    la_ref[...] = _pick_lane(gates, 2 * H_B + d * H_B + h)
    _tile_cumsum(la_ref, gs_ref, rev)

    ii = lax.broadcasted_iota(jnp.int32, (CHUNK, CHUNK), 0)
    jj = lax.broadcasted_iota(jnp.int32, (CHUNK, CHUNK), 1)
    incl = (jj >= ii) if rev else (jj <= ii)
    strict = (jj > ii) if rev else (jj < ii)
    eye = (ii == jj).astype(f32)
    last = 0 if rev else CHUNK - 1
    diag8 = (ii // 8) == (jj // 8)
    merge_masks = [((ii // (2 * sz)) == (jj // (2 * sz))) & ((ii // sz) != (jj // sz)) for sz in (8, 16, 32)]

    def chunk_body(ci, carry):
        c = (TB // CHUNK - 1 - ci) if rev else ci
        units = range(SUBLANES)
        rows = [pl.ds(c * CROWS + b, CHUNK, stride=SUBLANES) for b in units]
        q = [qs_ref[0, r, :] for r in rows]
        k = [ks_ref[0, r, :] for r in rows]
        v = [vs_ref[0, r, :] for r in rows]
        beta = [bs_ref[r, :] for r in rows]
        g = [gs_ref[r, :] for r in rows]
        kb = [k[b] * beta[b] for b in units]
        dec_incl, low = [], []
        for b in units:
            gsq = g[b][:, :CHUNK]
            g_row = jnp.sum(gsq * eye, axis=0, keepdims=True)
            diff = gsq - g_row
            dec = jnp.where(incl, jnp.exp(jnp.where(incl, diff, 0.0)), 0.0)
            dec_incl.append(dec)
            low.append(_mm_nt(kb[b], k[b]) * jnp.where(strict, dec, 0.0))
        ld = [jnp.where(diag8, low[b], 0.0) for b in units]
        ls = [_split(ld[b]) for b in units]
        d2 = [_mm3(ls[b], ls[b]) for b in units]
        d2s = [_split(d2[b]) for b in units]
        t_inv = [eye - ld[b] for b in units]
        ts = [_split(t_inv[b]) for b in units]
        prod, d4 = [], []
        for b in units:
            prod.append(_mm3(ts[b], d2s[b]))
            d4.append(_mm3(d2s[b], d2s[b]))
        t_inv = [t_inv[b] + prod[b] for b in units]
        t_inv = [t_inv[b] + _mm3(_split(t_inv[b]), _split(d4[b])) for b in units]
        for lvl in merge_masks:
            ts = [_split(t_inv[b]) for b in units]
            tl = [_mm3(ts[b], _split(jnp.where(lvl, low[b], 0.0))) for b in units]
            t_inv = [t_inv[b] - _mm3(_split(tl[b]), ts[b]) for b in units]
        eg = [jnp.exp(g[b]) for b in units]
        ts = [_split(t_inv[b]) for b in units]
        uw = [_mm3(ts[b], _split(jnp.concatenate([v[b] * beta[b], kb[b] * eg[b]], axis=1))) for b in units]
        a_qk = [_mm_nt(q[b], k[b]) * dec_incl[b] for b in units]
        g_last = [g[b][last:last + 1, :] for b in units]
        st = [st_ref[b] for b in units]
        v_new = [uw[b][:, :DV_B] - _mm(uw[b][:, DV_B:], st[b]) for b in units]
        o = [_mm(q[b] * eg[b], st[b]) + _mm(a_qk[b], v_new[b]) for b in units]
        for b in units:
            k_g = k[b] * jnp.exp(g_last[b] - g[b])
            st_ref[b] = st[b] * jnp.exp(g_last[b]) + _mm_tn(k_g, v_new[b])
            o_ref[0, rows[b], :] = o[b]
        return carry

    lax.fori_loop(0, TB // CHUNK, chunk_body, 0)

    @pl.when(is_ctx)
    def _():
        fin_ref[0, 0] = st_ref[...]


def _delta(qn, kn, vn, gates, s0, *, d, gc, nd, rev):
    rows = qn.shape[1]
    nblk = gc + nd
    blk = lambda h, s: _step_blk(s, gc, nd, rev)
    slab = pl.BlockSpec((1, RB, LANES), lambda h, s: (h, blk(h, s), 0))
    scr = pltpu.VMEM((RB, LANES), f32)
    return pl.pallas_call(
        functools.partial(_delta_kernel, gc=gc, rev=rev, d=d),
        grid=(H_B, nblk),
        in_specs=[
            slab, slab, slab,
            pl.BlockSpec((RB, LANES), lambda h, s: (blk(h, s), 0)),
            pl.BlockSpec((1, SUBLANES, DK_B, DV_B), lambda h, s: (h, 0, 0, 0)),
        ],
        out_specs=[
            pl.BlockSpec((1, RB, LANES), lambda h, s: (h, blk(h, s), 0)),
            pl.BlockSpec((1, 1, SUBLANES, DK_B, DV_B), lambda h, s: (jnp.minimum(s, gc - 1), h, 0, 0, 0)),
        ],
        out_shape=[
            jax.ShapeDtypeStruct((H_B, rows, LANES), f32),
            jax.ShapeDtypeStruct((gc, H_B, SUBLANES, DK_B, DV_B), f32),
        ],
        scratch_shapes=[scr, scr, scr, pltpu.VMEM((SUBLANES, DK_B, DV_B), f32)],
        compiler_params=_cparams(("parallel", "arbitrary")),
        name="delta_bwd" if rev else "delta_fwd",
    )(qn, kn, vn, gates, s0)


def _s5_kernel(u_ref, bt_ref, ct_ref, lam_ref, s0_ref, y_ref, fin_ref, sbuf_ref, carry_ref, *, gc, rev):
    s = pl.program_id(1)
    half = sbuf_ref.shape[1] // 2

    @pl.when(s < gc)
    def _():
        carry_ref[...] = jnp.zeros_like(carry_ref)

    @pl.when(s == gc)
    def _():
        carry_ref[...] = s0_ref[0]

    nsl = u_ref.shape[0]
    lr = jnp.broadcast_to(lam_ref[0, 0:1, :], (SUBLANES, half))
    li = jnp.broadcast_to(lam_ref[0, 1:2, :], (SUBLANES, half))
    nq = TB // S5_TQ
    qrows = S5_TQ * SUBLANES
    order = list(range(nq - 1, -1, -1)) if rev else list(range(nq))

    def project_in(qb):
        rs = slice(qb * qrows, (qb + 1) * qrows)
        sbuf_ref[rs, :] = _mm(jnp.concatenate([u_ref[j, rs, :] for j in range(nsl)], axis=1), bt_ref[0])

    def project_out(qb):
        rs = slice(qb * qrows, (qb + 1) * qrows)
        y = _mm(sbuf_ref[rs, :], ct_ref[0])
        for j in range(nsl):
            y_ref[j, rs, :] = y[:, j * LANES:(j + 1) * LANES]

    hr, hi = carry_ref[:, :half], carry_ref[:, half:]
    project_in(order[0])
    for n, qb in enumerate(order):
        if n + 1 < nq:
            project_in(order[n + 1])
        for i in range(S5_TQ):
            t = qb * S5_TQ + ((S5_TQ - 1 - i) if rev else i)
            rs = slice(t * SUBLANES, (t + 1) * SUBLANES)
            hr, hi = (lr * hr - li * hi + sbuf_ref[rs, :half], lr * hi + li * hr + sbuf_ref[rs, half:])
            sbuf_ref[rs, :half] = hr
            sbuf_ref[rs, half:] = hi
        project_out(qb)
    carry_ref[:, :half] = hr
    carry_ref[:, half:] = hi

    @pl.when(s < gc)
    def _():
        fin_ref[0, 0] = carry_ref[...]


S5_TQ = 64
S5_COLS = LANES
S5_GPC = S5_COLS // S5_GW
S5_SW = 2 * S5_GPC * S5_N


def _s5(z, bt, ct, lam, s0, *, gc, nd, rev):
    rows = z.shape[1]
    nblk = gc + nd
    ngt = D_C // S5_COLS
    nsl = S5_COLS // LANES
    sw = S5_SW
    blk = lambda g, s: _step_blk(s, gc, nd, rev)
    return pl.pallas_call(
        functools.partial(_s5_kernel, gc=gc, rev=rev),
        grid=(ngt, nblk),
        in_specs=[
            pl.BlockSpec((nsl, RB, LANES), lambda g, s: (O_U // nsl + g, blk(g, s), 0)),
            pl.BlockSpec((1, S5_COLS, sw), lambda g, s: (g, 0, 0)),
            pl.BlockSpec((1, sw, S5_COLS), lambda g, s: (g, 0, 0)),
            pl.BlockSpec((1, 2, sw // 2), lambda g, s: (g, 0, 0)),
            pl.BlockSpec((1, SUBLANES, sw), lambda g, s: (g, 0, 0)),
        ],
        out_specs=[
            pl.BlockSpec((nsl, RB, LANES), lambda g, s: (g, blk(g, s), 0)),
            pl.BlockSpec((1, 1, SUBLANES, sw), lambda g, s: (jnp.minimum(s, gc - 1), g, 0, 0)),
        ],
        out_shape=[
            jax.ShapeDtypeStruct((D_C // LANES, rows, LANES), f32),
            jax.ShapeDtypeStruct((gc, ngt, SUBLANES, sw), f32),
        ],
        scratch_shapes=[pltpu.VMEM((RB, sw), f32), pltpu.VMEM((SUBLANES, sw), f32)],
        compiler_params=_cparams(("parallel", "arbitrary")),
        name="s5_bwd" if rev else "s5_fwd",
    )(z, bt, ct, lam, s0)


def _s5_out_kernel(yf_ref, yb_ref, u_ref, d_ref, w_ref, b_ref, o_ref):
    ns = yf_ref.shape[0]
    yc = jnp.concatenate([yf_ref[s] + yb_ref[s] + d_ref[s] * u_ref[s] for s in range(ns)], axis=1)
    zc = jax.nn.gelu(yc)
    gate = _sigmoid(jnp.dot(zc.astype(bf16), w_ref[...], preferred_element_type=f32) + b_ref[...])
    o_ref[...] = (zc * gate).astype(o_ref.dtype)


def _s5_out(yf, yb, z, dskip, w_glu, b_glu, *, tm):
    ns, rows, _ = yf.shape
    return pl.pallas_call(
        _s5_out_kernel,
        grid=(rows // tm,),
        in_specs=[
            pl.BlockSpec((ns, tm, LANES), lambda i: (0, i, 0)),
            pl.BlockSpec((ns, tm, LANES), lambda i: (0, i, 0)),
            pl.BlockSpec((ns, tm, LANES), lambda i: (O_U // ns, i, 0)),
            pl.BlockSpec((ns, 1, LANES), lambda i: (0, 0, 0)),
            pl.BlockSpec((D_C, D_C), lambda i: (0, 0)),
            pl.BlockSpec((1, D_C), lambda i: (0, 0)),
        ],
        out_specs=pl.BlockSpec((tm, D_C), lambda i: (i, 0)),
        out_shape=jax.ShapeDtypeStruct((rows, D_C), bf16),
        compiler_params=_cparams(("parallel",)),
        name="s5_out",
    )(yf, yb, z, dskip, w_glu, b_glu)


def _gla_kernel(q_ref, k_ref, v_ref, glr_ref, wg_ref, bg_ref, s0_ref, o_ref, fin_ref,
                la_ref, gs_ref, st_ref, *, gc, rev):
    s = pl.program_id(1)
    is_ctx = s < gc

    @pl.when(is_ctx)
    def _():
        st_ref[...] = jnp.zeros_like(st_ref)

    @pl.when(s == gc)
    def _():
        st_ref[...] = s0_ref[0]

    la_ref[...] = jax.nn.log_sigmoid(_mm(glr_ref[0], wg_ref[0]) + bg_ref[0]) * (1.0 / GLA_TAU)
    _tile_cumsum(la_ref, gs_ref, rev)

    ii = lax.broadcasted_iota(jnp.int32, (SUB, CHUNK), 0)
    jj = lax.broadcasted_iota(jnp.int32, (SUB, CHUNK), 1)
    last = 0 if rev else CHUNK - 1

    def chunk_body(ci, carry):
        c = (TB // CHUNK - 1 - ci) if rev else ci
        units = range(SUBLANES)
        rows = [pl.ds(c * CROWS + b, CHUNK, stride=SUBLANES) for b in units]
        q = [q_ref[0, r, :] * (DK_D ** -0.5) for r in rows]
        k = [k_ref[0, r, :] for r in rows]
        v = [jnp.concatenate([v_ref[0, r, :], v_ref[1, r, :]], axis=1) for r in rows]
        g = [gs_ref[r, :] for r in rows]
        a_rows = [[] for _ in units]
        for blk in range(CHUNK // SUB):
            lo = blk * SUB
            first = lo + SUB - 1 if rev else lo
            causal = (jj >= ii + lo) if rev else (jj <= ii + lo)
            for b in units:
                g_first = g[b][first:first + 1, :]
                qt = q[b][lo:lo + SUB, :] * jnp.exp(g[b][lo:lo + SUB, :] - g_first)
                kt = k[b] * jnp.exp(g_first - g[b])
                a_rows[b].append(jnp.where(causal, _mm_nt(qt, kt), 0.0))
        a_qk = [jnp.concatenate(a_rows[b], axis=0) for b in units]
        g_last = [g[b][last:last + 1, :] for b in units]
        st = [st_ref[b] for b in units]
        o = [_mm_nt(q[b] * jnp.exp(g[b]), st[b]) + _mm(a_qk[b], v[b]) for b in units]
        for b in units:
            k_g = k[b] * jnp.exp(g_last[b] - g[b])
            st_ref[b] = st[b] * jnp.exp(g_last[b]) + _mm_tn(v[b], k_g)
            o_ref[0, rows[b], :] = o[b][:, :LANES]
            o_ref[1, rows[b], :] = o[b][:, LANES:]
        return carry

    lax.fori_loop(0, TB // CHUNK, chunk_body, 0)

    @pl.when(is_ctx)
    def _():
        fin_ref[0, 0] = st_ref[...]


def _gla(z, wg, bg, s0, *, gc, nd, rev):
    rows = z.shape[1]
    nblk = gc + nd
    nv = DV_D // LANES
    blk = lambda h, s: _step_blk(s, gc, nd, rev)
    scr = pltpu.VMEM((RB, LANES), f32)
    return pl.pallas_call(
        functools.partial(_gla_kernel, gc=gc, rev=rev),
        grid=(H_D, nblk),
        in_specs=[
            pl.BlockSpec((1, RB, LANES), lambda h, s: (O_Q + h, blk(h, s), 0)),
            pl.BlockSpec((1, RB, LANES), lambda h, s: (O_K + h, blk(h, s), 0)),
            pl.BlockSpec((nv, RB, LANES), lambda h, s: (O_V // nv + h, blk(h, s), 0)),
            pl.BlockSpec((1, RB, LANES), lambda h, s: (O_GLR, blk(h, s), 0)),
            pl.BlockSpec((1, LANES, LANES), lambda h, s: (h, 0, 0)),
            pl.BlockSpec((1, 1, LANES), lambda h, s: (h, 0, 0)),
            pl.BlockSpec((1, SUBLANES, DV_D, DK_D), lambda h, s: (h, 0, 0, 0)),
        ],
        out_specs=[
            pl.BlockSpec((nv, RB, LANES), lambda h, s: (h, blk(h, s), 0)),
            pl.BlockSpec((1, 1, SUBLANES, DV_D, DK_D), lambda h, s: (jnp.minimum(s, gc - 1), h, 0, 0, 0)),
        ],
        out_shape=[
            jax.ShapeDtypeStruct((H_D * nv, rows, LANES), f32),
            jax.ShapeDtypeStruct((gc, H_D, SUBLANES, DV_D, DK_D), f32),
        ],
        scratch_shapes=[scr, scr, pltpu.VMEM((SUBLANES, DV_D, DK_D), f32)],
        compiler_params=_cparams(("parallel", "arbitrary")),
        name="gla_bwd" if rev else "gla_fwd",
    )(z, z, z, z, wg, bg, s0)


TL = 64


def _rows_in_kernel(xp_ref, xs_ref, o_ref, *, nctx):
    def emit(ref):
        x = ref[0]
        o_ref[...] = jnp.swapaxes(x, 0, 1).reshape(TL * SUBLANES, x.shape[2])

    pl.when(pl.program_id(0) < nctx)(lambda: emit(xp_ref))
    pl.when(pl.program_id(0) >= nctx)(lambda: emit(xs_ref))


def _rows_in(x_prompt, x_sample):
    bp, tp, d = x_prompt.shape
    bs, ts, _ = x_sample.shape
    xp = x_prompt.reshape(bp // SUBLANES, SUBLANES, tp, d)
    xs = x_sample.reshape(bs // SUBLANES, SUBLANES, ts, d)
    per = tp // TL
    nctx = (bp // SUBLANES) * per
    nblk = nctx + ts // TL

    def ctx_idx(i):
        j = jnp.minimum(i, nctx - 1)
        return (j // per, 0, j % per, 0)

    return pl.pallas_call(
        functools.partial(_rows_in_kernel, nctx=nctx),
        grid=(nblk,),
        in_specs=[
            pl.BlockSpec((1, SUBLANES, TL, d), ctx_idx),
            pl.BlockSpec((1, SUBLANES, TL, d), lambda i: (0, 0, jnp.maximum(i - nctx, 0), 0)),
        ],
        out_specs=pl.BlockSpec((TL * SUBLANES, d), lambda i: (i, 0)),
        out_shape=jax.ShapeDtypeStruct((bp * tp + bs * ts, d), x_prompt.dtype),
        compiler_params=_cparams(("arbitrary",)),
        name="rows_in",
    )(xp, xs)


def _rows_out_kernel(x_ref, yp_ref, ys_ref, *, nctx):
    x = x_ref[...]
    y = jnp.swapaxes(x.reshape(TL, SUBLANES, x.shape[1]), 0, 1)

    @pl.when(pl.program_id(0) < nctx)
    def _():
        yp_ref[0] = y

    @pl.when(pl.program_id(0) >= nctx)
    def _():
        ys_ref[0] = y


def _rows_out(x, bp, tp, bs, ts):
    d = x.shape[1]
    per = tp // TL
    nctx = (bp // SUBLANES) * per
    nblk = nctx + ts // TL

    def ctx_idx(i):
        j = jnp.minimum(i, nctx - 1)
        return (j // per, 0, j % per, 0)

    yp, ys = pl.pallas_call(
        functools.partial(_rows_out_kernel, nctx=nctx),
        grid=(nblk,),
        in_specs=[pl.BlockSpec((TL * SUBLANES, d), lambda i: (i, 0))],
        out_specs=[
            pl.BlockSpec((1, SUBLANES, TL, d), ctx_idx),
            pl.BlockSpec((1, SUBLANES, TL, d), lambda i: (0, 0, jnp.maximum(i - nctx, 0), 0)),
        ],
        out_shape=[
            jax.ShapeDtypeStruct((bp // SUBLANES, SUBLANES, tp, d), x.dtype),
            jax.ShapeDtypeStruct((bs // SUBLANES, SUBLANES, ts, d), x.dtype),
        ],
        compiler_params=_cparams(("arbitrary",)),
        name="rows_out",
    )(x)
    return yp.reshape(bp, tp, d), ys.reshape(bs, ts, d)


def _pad_cols(w, n):
    return jnp.pad(w, ((0, 0), (0, n - w.shape[1])))


def _unit_states(fin, b):
    fin = jnp.moveaxis(fin, b, 1)
    return fin.reshape((fin.shape[0] * SUBLANES,) + fin.shape[2:])


def kernel(x_prompt, x_sample, state_lru, state_delta, state_s5_re, state_s5_im, state_gla, c, c_ctx, w_ada, b_ada, norm_g, w_up, w_down, w_in_e, w_out_e, conv_a, lru_wa, lru_ba, lru_wx, lru_bx, lru_lambda, conv_qkv, dn_a_log, dn_dt_bias, dn_norm_g, w_in_o, w_out_o, s5_lam_re, s5_lam_im, s5_log_step, s5_b_re, s5_b_im, s5_c_re, s5_c_im, s5_d, s5_w_glu, s5_b_glu, gla_wg2, gla_bg, gla_norm_g):
    bp, tp, d = x_prompt.shape
    bs, ts, _ = x_sample.shape
    assert tp == TB and bp % SUBLANES == 0 and bs == SUBLANES and ts % TB == 0 and d == D_MODEL
    depth = w_ada.shape[0]
    gc = bp // SUBLANES
    nd = ts // TB
    nblk = gc + nd
    rows_ctx = bp * tp
    seq = dict(gc=gc, nd=nd)

    x = _rows_in(x_prompt, x_sample)

    cvec = jnp.concatenate([c, jnp.broadcast_to(c_ctx[None], (SUBLANES, d))], axis=0)
    mod = _ada(cvec, w_ada, b_ada)
    mod = jnp.stack([mod[:, SUBLANES:], mod[:, :SUBLANES]], axis=1)
    mods = [[mod[l, :, :, i * d:(i + 1) * d] for i in range(6)] for l in range(depth)]

    fin_lru, fin_delta, fin_re, fin_im, fin_gla = [], [], [], [], []
    for l in range(depth):
        sh1, sc1, gt1, sh2, sc2, gt2 = mods[l]
        if l % 2 == 0:
            e = l // 2
            w_in = _pad_cols(w_in_e[e], E_SLABS * LANES).astype(bf16)
            z = _nmm(x, norm_g[l, 0], sh1, sc1, w_in, rows_ctx=rows_ctx, tm=256, tn=10 * LANES, slab=True)
            ns = D_A // LANES
            wg = jnp.stack([lru_wa[e], lru_wx[e]], axis=1).reshape(2, 2, ns, 2, LRU_BW, LRU_BW)
            wbd = jnp.zeros((2, 2, ns, LANES, LANES), f32)
            wbd = wbd.at[:, :, :, :LRU_BW, :LRU_BW].set(wg[:, :, :, 0]).at[:, :, :, LRU_BW:, LRU_BW:].set(wg[:, :, :, 1])
            bias = jnp.stack([lru_ba[e], lru_bx[e]], axis=1).reshape(2, 2, ns, 1, LANES)
            sp = jax.nn.softplus(-lru_lambda[e]).reshape(2, ns, 1, LANES)
            cwa = conv_a[e].reshape(4, ns, LANES).transpose(1, 0, 2)
            a, b = _lru_gates(z, cwa, wbd.astype(bf16), bias, sp, gc=gc, nblk=nblk)
            h_f, f_f = _lru_scan(a, b, state_lru[:, e, 0], d=0, rev=False, **seq)
            h_b, f_b = _lru_scan(a, b, state_lru[:, e, 1], d=1, rev=True, **seq)
            fin_lru.append(jnp.stack([_unit_states(f_f, 1), _unit_states(f_b, 1)], axis=1))
            cq = conv_qkv[e].reshape(4, 3, H_B, LANES).transpose(1, 2, 0, 3)
            lane_pad = lambda v: jnp.pad(v.reshape(1, 2 * H_B), ((0, 0), (2 * H_B, LANES - 4 * H_B)))
            nega = lane_pad(-jnp.exp(dn_a_log[e]))
            dtb = lane_pad(dn_dt_bias[e])
            s0 = state_delta[:, e].transpose(1, 2, 0, 3, 4)
            qn, kn, vn, gates = _delta_prep(z, cq, nega, dtb, gc=gc, nblk=nblk)
            o_f, s_f = _delta(qn, kn, vn, gates, s0[0], d=0, rev=False, **seq)
            o_b, s_b = _delta(qn, kn, vn, gates, s0[1], d=1, rev=True, **seq)
            fin_delta.append(jnp.stack([_unit_states(s_f, 2), _unit_states(s_b, 2)], axis=1))
            x = _mixout_even(h_f, h_b, o_f, o_b, z, dn_norm_g[e], w_out_e[e].astype(bf16), x, norm_g[l, 1], gt1,
                             rows_ctx=rows_ctx, tm=512)
        else:
            o = l // 2
            w_in = _pad_cols(w_in_o[o], O_SLABS * LANES).astype(bf16)
            z = _nmm(x, norm_g[l, 0], sh1, sc1, w_in, rows_ctx=rows_ctx, tm=256, tn=11 * LANES, slab=True)
            ngt = D_C // S5_COLS
            gps = S5_GPC
            lam = lax.complex(s5_lam_re[o], s5_lam_im[o])
            lam_bar = jnp.exp(lam * jnp.exp(s5_log_step[o])[..., None])
            b_bar = ((lam_bar - 1.0) / lam)[..., None] * lax.complex(s5_b_re[o], s5_b_im[o])
            eye_g = jnp.eye(gps, dtype=f32)

            def bmat(p):
                p = p.reshape(2, ngt, gps, S5_N, S5_GW)
                return jnp.einsum('dtgnw,gh->dtgwhn', p, eye_g).reshape(2, ngt, S5_COLS, gps * S5_N)

            def cmat(p):
                p = p.reshape(2, ngt, gps, S5_GW, S5_N)
                return jnp.einsum('dtgwn,gh->dtgnhw', p, eye_g).reshape(2, ngt, gps * S5_N, S5_COLS)

            bt = jnp.concatenate([bmat(b_bar.real), bmat(b_bar.imag)], axis=-1).astype(bf16)
            ct = jnp.concatenate([cmat(s5_c_re[o]), -cmat(s5_c_im[o])], axis=-2).astype(bf16)
            lamv = jnp.stack([lam_bar.real, lam_bar.imag], axis=2).reshape(2, ngt, gps, 2, S5_N)
            lamv = lamv.transpose(0, 1, 3, 2, 4).reshape(2, ngt, 2, gps * S5_N)

            def s5_state(sre, sim):
                f = lambda p: p.reshape(SUBLANES, ngt, gps * S5_N).transpose(1, 0, 2)
                return jnp.concatenate([f(sre), f(sim)], axis=-1)

            y_f, c_f = _s5(z, bt[0], ct[0], lamv[0], s5_state(state_s5_re[:, o, 0], state_s5_im[:, o, 0]), rev=False, **seq)
            y_bk, c_b = _s5(z, bt[1], ct[1], lamv[1], s5_state(state_s5_re[:, o, 1], state_s5_im[:, o, 1]), rev=True, **seq)
            y_c = _s5_out(y_f, y_bk, z, s5_d[o].reshape(D_C // LANES, 1, LANES), s5_w_glu[o].astype(bf16),
                          s5_b_glu[o].reshape(1, D_C), tm=512)

            def s5_fin(cf):
                cf = cf.transpose(0, 2, 1, 3).reshape(gc * SUBLANES, ngt, 2, gps, S5_N)
                return (cf[:, :, 0].reshape(gc * SUBLANES, S5_G, S5_N), cf[:, :, 1].reshape(gc * SUBLANES, S5_G, S5_N))

            (rf, imf), (rb, imb) = s5_fin(c_f), s5_fin(c_b)
            fin_re.append(jnp.stack([rf, rb], axis=1))
            fin_im.append(jnp.stack([imf, imb], axis=1))
            wgp = jnp.zeros((2, H_D, LANES, LANES), f32)
            wg2 = gla_wg2[o].reshape(2, GLA_RANK, H_D, DK_D).transpose(0, 2, 1, 3)
            wgp = wgp.at[0, :, :GLA_RANK].set(wg2[0]).at[1, :, GLA_RANK:2 * GLA_RANK].set(wg2[1]).astype(bf16)
            bgp = gla_bg[o].reshape(2, H_D, 1, DK_D)
            g0 = state_gla[:, o].transpose(1, 2, 0, 4, 3)
            o_f, g_f = _gla(z, wgp[0], bgp[0], g0[0], rev=False, **seq)
            o_b, g_b = _gla(z, wgp[1], bgp[1], g0[1], rev=True, **seq)
            gfin = lambda gf: jnp.swapaxes(_unit_states(gf, 2), -1, -2)
            fin_gla.append(jnp.stack([gfin(g_f), gfin(g_b)], axis=1))
            x = _mixout_odd(y_c, o_f, o_b, z, gla_norm_g[o], w_out_o[o].astype(bf16), x, norm_g[l, 1], gt1,
                            rows_ctx=rows_ctx, tm=512)

        hmid = _nmm(x, norm_g[l, 2], sh2, sc2, w_up[l].astype(bf16), rows_ctx=rows_ctx, tm=256, tn=2048,
                    relu2=True, out_dtype=bf16)
        x = _mmres(hmid, w_down[l].astype(bf16), x, norm_g[l, 3], gt2, rows_ctx=rows_ctx, tm=256)

    y_prompt, y_sample = _rows_out(x, bp, tp, bs, ts)
    return (y_prompt, y_sample, jnp.stack(fin_lru, axis=1), jnp.stack(fin_delta, axis=1),
            jnp.stack(fin_re, axis=1), jnp.stack(fin_im, axis=1), jnp.stack(fin_gla, axis=1))
```

```python
import functools
import math

import jax
import jax.numpy as jnp
from jax import lax
from jax.experimental import pallas as pl
from jax.experimental.pallas import tpu as pltpu

f32 = jnp.float32
bf16 = jnp.bfloat16

LANES = 128
SUBLANES = 8
VMEM_LIMIT_BYTES = 56 * 1024 * 1024

D_MODEL = 2048
D_FF = 4 * D_MODEL
GRID_W = 64
CHUNK = 64
EPS = 1e-6
D_A = D_MODEL // 2
LRU_BLOCKS = 16
LRU_BW = D_A // LRU_BLOCKS
LRU_C = 8.0
H_B = 8
DK_B = 128
DV_B = 128
D_C = D_MODEL // 2
S5_GW = 16
S5_G = D_C // S5_GW
S5_N = 64
H_D = 4
DK_D = 128
DV_D = 256
GLA_RANK = 16
GLA_TAU = 16.0

TB = 256
RB = TB * SUBLANES
CROWS = CHUNK * SUBLANES
SUB = 16

E_XA, E_GA, E_Q, E_K, E_V, E_GO, E_BA = 0, 8, 16, 24, 32, 40, 48
E_SLABS = 50
O_U, O_Q, O_K, O_V, O_R, O_GLR = 0, 8, 12, 16, 24, 32
O_SLABS = 33


def _cparams(sem):
    return pltpu.CompilerParams(dimension_semantics=sem, vmem_limit_bytes=VMEM_LIMIT_BYTES)


def _mm(a, b):
    return jnp.dot(a.astype(bf16), b.astype(bf16), preferred_element_type=f32)


def _mm_nt(a, b):
    return lax.dot_general(a.astype(bf16), b.astype(bf16), (((1,), (1,)), ((), ())), preferred_element_type=f32)


def _mm_tn(a, b):
    return lax.dot_general(a.astype(bf16), b.astype(bf16), (((0,), (0,)), ((), ())), preferred_element_type=f32)


def _split(x):
    hi = x.astype(bf16)
    return hi, (x - hi.astype(f32)).astype(bf16)


def _mm3(a, b):
    dot = functools.partial(jnp.dot, preferred_element_type=f32)
    return dot(a[0], b[0]) + (dot(a[0], b[1]) + dot(a[1], b[0]))


def _sigmoid(x):
    return 0.5 * jnp.tanh(0.5 * x) + 0.5


def _silu(x):
    t = 0.5 * x
    return t + t * jnp.tanh(t)


def _ada_kernel(c_ref, w_ref, b_ref, o_ref):
    a = _silu(c_ref[...])
    o_ref[0] = _mm(a, w_ref[0]) + b_ref[0]


def _ada(cvec, w_ada, b_ada):
    depth, d, n = w_ada.shape
    tn = 1024
    return pl.pallas_call(
        _ada_kernel,
        grid=(depth, n // tn),
        in_specs=[
            pl.BlockSpec(cvec.shape, lambda l, j: (0, 0)),
            pl.BlockSpec((1, d, tn), lambda l, j: (l, 0, j)),
            pl.BlockSpec((1, 1, tn), lambda l, j: (l, 0, j)),
        ],
        out_specs=pl.BlockSpec((1, cvec.shape[0], tn), lambda l, j: (l, 0, j)),
        out_shape=jax.ShapeDtypeStruct((depth, cvec.shape[0], n), f32),
        compiler_params=_cparams(("parallel", "parallel")),
        name="ada",
    )(cvec, w_ada, b_ada.reshape(depth, 1, n))


def _nmm_kernel(x_ref, g_ref, sh_ref, sc_ref, w_ref, o_ref, *, relu2, slab, tn):
    x = x_ref[...]
    tm, d = x.shape
    y = x * lax.rsqrt(jnp.mean(x * x, axis=-1, keepdims=True) + EPS)
    gain = g_ref[...] * (1.0 + sc_ref[0])
    h = (y.reshape(tm // SUBLANES, SUBLANES, d) * gain[None] + sh_ref[0][None]).reshape(tm, d).astype(bf16)
    for c in range(w_ref.shape[1] // tn):
        acc = jnp.dot(h, w_ref[:, c * tn:(c + 1) * tn], preferred_element_type=f32)
        if relu2:
            acc = jnp.square(jnp.maximum(acc, 0.0))
        if slab:
            for s in range(tn // LANES):
                o_ref[c * (tn // LANES) + s] = acc[:, s * LANES:(s + 1) * LANES].astype(o_ref.dtype)
        else:
            o_ref[:, c * tn:(c + 1) * tn] = acc.astype(o_ref.dtype)


def _nmm(x, g, sh, sc, w, *, rows_ctx, tm, tn, relu2=False, slab=False, out_dtype=f32):
    rows, d = x.shape
    n = w.shape[1]
    assert n % tn == 0
    grp = lambda i: ((i * tm >= rows_ctx).astype(jnp.int32), 0, 0)
    if slab:
        out_shape = jax.ShapeDtypeStruct((n // LANES, rows, LANES), out_dtype)
        out_spec = pl.BlockSpec((n // LANES, tm, LANES), lambda i: (0, i, 0))
    else:
        out_shape = jax.ShapeDtypeStruct((rows, n), out_dtype)
        out_spec = pl.BlockSpec((tm, n), lambda i: (i, 0))
    return pl.pallas_call(
        functools.partial(_nmm_kernel, relu2=relu2, slab=slab, tn=tn),
        grid=(rows // tm,),
        in_specs=[
            pl.BlockSpec((tm, d), lambda i: (i, 0)),
            pl.BlockSpec((1, d), lambda i: (0, 0)),
            pl.BlockSpec((1, SUBLANES, d), grp),
            pl.BlockSpec((1, SUBLANES, d), grp),
            pl.BlockSpec((d, n), lambda i: (0, 0), pipeline_mode=pl.Buffered(1)),
        ],
        out_specs=out_spec,
        out_shape=out_shape,
        compiler_params=_cparams(("parallel",)),
        name="nmm",
    )(x, g.reshape(1, d), sh, sc, w)


def _mmres_kernel(a_ref, w_ref, x_ref, g_ref, gt_ref, o_ref):
    y = jnp.dot(a_ref[...], w_ref[...], preferred_element_type=f32)
    _gated_residual(y, x_ref, g_ref, gt_ref, o_ref)


def _gated_residual(y, x_ref, g_ref, gt_ref, o_ref):
    tm, d = y.shape
    yn = y * lax.rsqrt(jnp.mean(y * y, axis=-1, keepdims=True) + EPS) * g_ref[...]
    o = x_ref[...].reshape(tm // SUBLANES, SUBLANES, d) + gt_ref[0][None] * yn.reshape(tm // SUBLANES, SUBLANES, d)
    o_ref[...] = o.reshape(tm, d)


def _headnorm_gate(o_slabs, gain, gate_slabs):
    width = len(o_slabs) * LANES
    ms = sum(jnp.sum(o * o, axis=-1, keepdims=True) for o in o_slabs) * (1.0 / width)
    inv = lax.rsqrt(ms + EPS)
    return [(o * inv * gain[j] * _silu(r)).astype(bf16) for j, (o, r) in enumerate(zip(o_slabs, gate_slabs))]


def _mixout_even_kernel(hf_ref, hb_ref, ga_ref, of_ref, ob_ref, go_ref, dg_ref, w1_ref, w2_ref,
                        x_ref, g_ref, gt_ref, o_ref):
    ns = ga_ref.shape[0]
    ga = jnp.concatenate([ga_ref[s] for s in range(ns)], axis=1)
    y_a = ((hf_ref[...] + hb_ref[...]) * jax.nn.gelu(ga)).astype(bf16)
    y = jnp.dot(y_a, w1_ref[...], preferred_element_type=f32)
    y_b = [_headnorm_gate([of_ref[h] + ob_ref[h]], [dg_ref[...]], [go_ref[h]])[0] for h in range(of_ref.shape[0])]
    y = y + jnp.dot(jnp.concatenate(y_b, axis=1), w2_ref[...], preferred_element_type=f32)
    _gated_residual(y, x_ref, g_ref, gt_ref, o_ref)


def _mixout_odd_kernel(yc_ref, of_ref, ob_ref, r_ref, dg_ref, w1_ref, w2_ref, x_ref, g_ref, gt_ref, o_ref):
    nv = DV_D // LANES
    y = jnp.dot(yc_ref[...], w1_ref[...], preferred_element_type=f32)
    y_d = []
    for h in range(of_ref.shape[0] // nv):
        sl = range(h * nv, (h + 1) * nv)
        y_d += _headnorm_gate([of_ref[j] + ob_ref[j] for j in sl], [dg_ref[j] for j in range(nv)], [r_ref[j] for j in sl])
    y = y + jnp.dot(jnp.concatenate(y_d, axis=1), w2_ref[...], preferred_element_type=f32)
    _gated_residual(y, x_ref, g_ref, gt_ref, o_ref)


def _mixout_specs(w, x, g, gt, rows_ctx, tm):
    kh = w.shape[0] // 2
    d = w.shape[1]
    grp = lambda i: ((i * tm >= rows_ctx).astype(jnp.int32), 0, 0)
    specs = [
        pl.BlockSpec((kh, d), lambda i: (0, 0), pipeline_mode=pl.Buffered(1)),
        pl.BlockSpec((kh, d), lambda i: (1, 0), pipeline_mode=pl.Buffered(1)),
        pl.BlockSpec((tm, d), lambda i: (i, 0)),
        pl.BlockSpec((1, d), lambda i: (0, 0)),
        pl.BlockSpec((1, SUBLANES, d), grp),
    ]
    return specs, (w, w, x, g.reshape(1, d), gt)


def _mixout_even(hf, hb, of, ob, z, dn_g, w, x, g, gt, *, rows_ctx, tm):
    rows, d = x.shape
    ns = D_A // LANES
    slabs = lambda off: pl.BlockSpec((ns, tm, LANES), lambda i: (off // ns, i, 0))
    tail_specs, tail_args = _mixout_specs(w, x, g, gt, rows_ctx, tm)
    return pl.pallas_call(
        _mixout_even_kernel,
        grid=(rows // tm,),
        in_specs=[
            pl.BlockSpec((tm, D_A), lambda i: (i, 0)),
            pl.BlockSpec((tm, D_A), lambda i: (i, 0)),
            slabs(E_GA), slabs(0), slabs(0), slabs(E_GO),
            pl.BlockSpec((1, LANES), lambda i: (0, 0)),
        ] + tail_specs,
        out_specs=pl.BlockSpec((tm, d), lambda i: (i, 0)),
        out_shape=jax.ShapeDtypeStruct((rows, d), f32),
        compiler_params=_cparams(("parallel",)),
        name="mixout_even",
    )(hf, hb, z, of, ob, z, dn_g.reshape(1, LANES), *tail_args)


def _mixout_odd(y_c, of, ob, z, gla_g, w, x, g, gt, *, rows_ctx, tm):
    rows, d = x.shape
    ns = H_D * DV_D // LANES
    nv = DV_D // LANES
    slabs = lambda off: pl.BlockSpec((ns, tm, LANES), lambda i: (off // ns, i, 0))
    tail_specs, tail_args = _mixout_specs(w, x, g, gt, rows_ctx, tm)
    return pl.pallas_call(
        _mixout_odd_kernel,
        grid=(rows // tm,),
        in_specs=[
            pl.BlockSpec((tm, D_C), lambda i: (i, 0)),
            slabs(0), slabs(0), slabs(O_R),
            pl.BlockSpec((nv, 1, LANES), lambda i: (0, 0, 0)),
        ] + tail_specs,
        out_specs=pl.BlockSpec((tm, d), lambda i: (i, 0)),
        out_shape=jax.ShapeDtypeStruct((rows, d), f32),
        compiler_params=_cparams(("parallel",)),
        name="mixout_odd",
    )(y_c, of, ob, z, gla_g.reshape(nv, 1, LANES), *tail_args)


def _mmres(a, w, x, g, gt, *, rows_ctx, tm):
    rows, kdim = a.shape
    d = w.shape[1]
    grp = lambda i: ((i * tm >= rows_ctx).astype(jnp.int32), 0, 0)
    return pl.pallas_call(
        _mmres_kernel,
        grid=(rows // tm,),
        in_specs=[
            pl.BlockSpec((tm, kdim), lambda i: (i, 0)),
            pl.BlockSpec((kdim, d), lambda i: (0, 0), pipeline_mode=pl.Buffered(1)),
            pl.BlockSpec((tm, d), lambda i: (i, 0)),
            pl.BlockSpec((1, d), lambda i: (0, 0)),
            pl.BlockSpec((1, SUBLANES, d), grp),
        ],
        out_specs=pl.BlockSpec((tm, d), lambda i: (i, 0)),
        out_shape=jax.ShapeDtypeStruct((rows, d), f32),
        compiler_params=_cparams(("parallel",)),
        name="mmres",
    )(a, w, x, g.reshape(1, d), gt)


def _step_blk(s, gc, nd, rev):
    if not rev:
        return s
    return jnp.where(s < gc, s, 2 * gc + nd - 1 - s)


def _conv4(x, w4, is_ctx):
    rows = x.shape[0]
    t = lax.broadcasted_iota(jnp.int32, x.shape, 0) // SUBLANES
    tl = jnp.where(is_ctx, t, t % GRID_W)
    last = jnp.where(is_ctx, TB - 1, GRID_W - 1)
    y = x * w4[2:3, :]
    xm2 = pltpu.roll(x, 2 * SUBLANES, 0)
    y = y + jnp.where(tl >= 2, xm2, 0.0) * w4[0:1, :]
    xm1 = pltpu.roll(x, SUBLANES, 0)
    y = y + jnp.where(tl >= 1, xm1, 0.0) * w4[1:2, :]
    xp1 = pltpu.roll(x, rows - SUBLANES, 0)
    y = y + jnp.where(tl < last, xp1, 0.0) * w4[3:4, :]
    return y


def _tile_cumsum(src_ref, dst_ref, rev):
    ntile = src_ref.shape[0] // SUBLANES
    run = None
    for i in range(ntile):
        r = ((ntile - 1 - i) if rev else i) * SUBLANES
        x = src_ref[r:r + SUBLANES, :]
        run = x if i % CHUNK == 0 else run + x
        dst_ref[r:r + SUBLANES, :] = run


def _pick_lane(x, lane):
    onehot = lax.broadcasted_iota(jnp.int32, x.shape, 1) == lane
    col = jnp.sum(jnp.where(onehot, x, 0.0), axis=-1, keepdims=True)
    return jnp.broadcast_to(col, x.shape)


def _lru_gate_kernel(xa_ref, cw_ref, w_ref, bias_ref, sp_ref, a_ref, b_ref, *, gc):
    is_ctx = pl.program_id(0) < gc
    u = _conv4(xa_ref[0], cw_ref[0], is_ctx)
    for d in range(2):
        r = _sigmoid(_mm(u, w_ref[d, 0, 0]) + bias_ref[d, 0, 0])
        i = _sigmoid(_mm(u, w_ref[d, 1, 0]) + bias_ref[d, 1, 0])
        log_a = -LRU_C * r * sp_ref[d, 0]
        a = jnp.exp(log_a)
        b = jnp.sqrt(-jnp.tanh(log_a) * (a * a + 1.0)) * (i * u)
        a_ref[d] = a
        b_ref[d] = b


def _lru_gates(z, conv_a, wbd, bias, sp, *, gc, nblk):
    rows = z.shape[1]
    ns = D_A // LANES
    out = jax.ShapeDtypeStruct((2, rows, D_A), f32)
    return pl.pallas_call(
        functools.partial(_lru_gate_kernel, gc=gc),
        grid=(nblk, ns),
        in_specs=[
            pl.BlockSpec((1, RB, LANES), lambda i, s: (E_XA + s, i, 0)),
            pl.BlockSpec((1, 4, LANES), lambda i, s: (s, 0, 0)),
            pl.BlockSpec((2, 2, 1, LANES, LANES), lambda i, s: (0, 0, s, 0, 0)),
            pl.BlockSpec((2, 2, 1, 1, LANES), lambda i, s: (0, 0, s, 0, 0)),
            pl.BlockSpec((2, 1, 1, LANES), lambda i, s: (0, s, 0, 0)),
        ],
        out_specs=[pl.BlockSpec((2, RB, LANES), lambda i, s: (0, i, s))] * 2,
        out_shape=[out, out],
        compiler_params=_cparams(("parallel", "parallel")),
        name="lru_gates",
    )(z, conv_a, wbd, bias, sp)


def _lru_scan_kernel(a_ref, b_ref, h0_ref, h_ref, fin_ref, carry_ref, *, gc, rev):
    s = pl.program_id(1)

    @pl.when(s < gc)
    def _():
        carry_ref[...] = jnp.zeros_like(carry_ref)

    @pl.when(s == gc)
    def _():
        carry_ref[...] = h0_ref[...]

    def body(i, h):
        t = (TB - 1 - i) if rev else i
        r = pl.multiple_of(t * SUBLANES, SUBLANES)
        h = a_ref[0, pl.ds(r, SUBLANES), :] * h + b_ref[0, pl.ds(r, SUBLANES), :]
        h_ref[pl.ds(r, SUBLANES), :] = h
        return h

    h = lax.fori_loop(0, TB, body, carry_ref[...], unroll=8)
    carry_ref[...] = h

    @pl.when(s < gc)
    def _():
        fin_ref[0] = h


def _lru_scan(a, b, h0, *, d, gc, nd, rev):
    rows = a.shape[1]
    nblk = gc + nd
    wc = 512
    blk = lambda c, s: _step_blk(s, gc, nd, rev)
    return pl.pallas_call(
        functools.partial(_lru_scan_kernel, gc=gc, rev=rev),
        grid=(D_A // wc, nblk),
        in_specs=[
            pl.BlockSpec((1, RB, wc), lambda c, s: (d, blk(c, s), c)),
            pl.BlockSpec((1, RB, wc), lambda c, s: (d, blk(c, s), c)),
            pl.BlockSpec((SUBLANES, wc), lambda c, s: (0, c)),
        ],
        out_specs=[
            pl.BlockSpec((RB, wc), lambda c, s: (blk(c, s), c)),
            pl.BlockSpec((1, SUBLANES, wc), lambda c, s: (jnp.minimum(s, gc - 1), 0, c)),
        ],
        out_shape=[
            jax.ShapeDtypeStruct((rows, D_A), f32),
            jax.ShapeDtypeStruct((gc, SUBLANES, D_A), f32),
        ],
        scratch_shapes=[pltpu.VMEM((SUBLANES, wc), f32)],
        compiler_params=_cparams(("parallel", "arbitrary")),
        name="lru_scan_bwd" if rev else "lru_scan_fwd",
    )(a, b, h0)


def _delta_prep_kernel(q_ref, k_ref, v_ref, ba_ref, cw_ref, nega_ref, dtb_ref, qn_ref, kn_ref, vn_ref, gate_ref, *, gc):
    is_ctx = pl.program_id(0) < gc

    def l2n(x):
        return x * lax.rsqrt(jnp.sum(x * x, axis=-1, keepdims=True) + EPS)

    qn_ref[0] = l2n(_silu(_conv4(q_ref[0], cw_ref[0, 0], is_ctx))) * (DK_B ** -0.5)
    kn_ref[0] = l2n(_silu(_conv4(k_ref[0], cw_ref[1, 0], is_ctx)))
    vn_ref[0] = _silu(_conv4(v_ref[0], cw_ref[2, 0], is_ctx))

    @pl.when(pl.program_id(1) == 0)
    def _():
        ba = ba_ref[0]
        lane = lax.broadcasted_iota(jnp.int32, ba.shape, 1)
        gate_ref[...] = jnp.where(lane < 2 * H_B, _sigmoid(ba), nega_ref[...] * jax.nn.softplus(ba + dtb_ref[...]))


def _delta_prep(z, conv_qkv, nega, dtb, *, gc, nblk):
    rows = z.shape[1]
    slab = lambda off: pl.BlockSpec((1, RB, LANES), lambda i, h: (off + h, i, 0))
    out = jax.ShapeDtypeStruct((H_B, rows, LANES), f32)
    return pl.pallas_call(
        functools.partial(_delta_prep_kernel, gc=gc),
        grid=(nblk, H_B),
        in_specs=[
            slab(E_Q), slab(E_K), slab(E_V),
            pl.BlockSpec((1, RB, LANES), lambda i, h: (E_BA, i, 0)),
            pl.BlockSpec((3, 1, 4, LANES), lambda i, h: (0, h, 0, 0)),
            pl.BlockSpec((1, LANES), lambda i, h: (0, 0)),
            pl.BlockSpec((1, LANES), lambda i, h: (0, 0)),
        ],
        out_specs=[
            pl.BlockSpec((1, RB, LANES), lambda i, h: (h, i, 0)),
            pl.BlockSpec((1, RB, LANES), lambda i, h: (h, i, 0)),
            pl.BlockSpec((1, RB, LANES), lambda i, h: (h, i, 0)),
            pl.BlockSpec((RB, LANES), lambda i, h: (i, 0)),
        ],
        out_shape=[out, out, out, jax.ShapeDtypeStruct((rows, LANES), f32)],
        compiler_params=_cparams(("parallel", "arbitrary")),
        name="delta_prep",
    )(z, z, z, z, conv_qkv, nega, dtb)


def _delta_kernel(qs_ref, ks_ref, vs_ref, gate_ref, s0_ref, o_ref, fin_ref,
                  bs_ref, la_ref, gs_ref, st_ref, *, gc, rev, d):
    h = pl.program_id(0)
    s = pl.program_id(1)
    is_ctx = s < gc

    @pl.when(is_ctx)
    def _():
        st_ref[...] = jnp.zeros_like(st_ref)

    @pl.when(s == gc)
    def _():
        st_ref[...] = s0_ref[...].reshape(st_ref.shape)

    gates = gate_ref[...]
    for j in range(DELTA_HEADS):
        head = DELTA_HEADS * h + j
        bs_ref[j] = _pick_lane(gates, d * H_B + head)
        la_ref[j] = _pick_lane(gates, 2 * H_B + d * H_B + head)
        _tile_cumsum(la_ref.at[j], gs_ref.at[j], rev)

    ii = lax.broadcasted_iota(jnp.int32, (CHUNK, CHUNK), 0)
    jj = lax.broadcasted_iota(jnp.int32, (CHUNK, CHUNK), 1)
    incl = (jj >= ii) if rev else (jj <= ii)
    strict = (jj > ii) if rev else (jj < ii)
    eye = (ii == jj).astype(f32)
    last = 0 if rev else CHUNK - 1
    diag8 = (ii // 8) == (jj // 8)
    merge_masks = [((ii // (2 * sz)) == (jj // (2 * sz))) & ((ii // sz) != (jj // sz)) for sz in (8, 16, 32)]

    def chunk_body(ci, carry):
        c = (TB // CHUNK - 1 - ci) if rev else ci
        units = range(DELTA_HEADS * SUBLANES)
        hd = [u // SUBLANES for u in units]
        rows = [pl.ds(c * CROWS + u % SUBLANES, CHUNK, stride=SUBLANES) for u in units]
        q = [qs_ref[j, r, :] for j, r in zip(hd, rows)]
        k = [ks_ref[j, r, :] for j, r in zip(hd, rows)]
        v = [vs_ref[j, r, :] for j, r in zip(hd, rows)]
        beta = [bs_ref[j, r, :] for j, r in zip(hd, rows)]
        g = [gs_ref[j, r, :] for j, r in zip(hd, rows)]
        kb = [k[b] * beta[b] for b in units]
        dec_incl, low = [], []
        for b in units:
            gsq = g[b][:, :CHUNK]
            g_row = jnp.sum(gsq * eye, axis=0, keepdims=True)
            diff = gsq - g_row
            dec = jnp.where(incl, jnp.exp(jnp.where(incl, diff, 0.0)), 0.0)
            dec_incl.append(dec)
            low.append(_mm_nt(kb[b], k[b]) * jnp.where(strict, dec, 0.0))
        ld = [jnp.where(diag8, low[b], 0.0) for b in units]
        ls = [_split(ld[b]) for b in units]
        d2 = [_mm3(ls[b], ls[b]) for b in units]
        d2s = [_split(d2[b]) for b in units]
        t_inv = [eye - ld[b] for b in units]
        ts = [_split(t_inv[b]) for b in units]
        prod, d4 = [], []
        for b in units:
            prod.append(_mm3(ts[b], d2s[b]))
            d4.append(_mm3(d2s[b], d2s[b]))
        t_inv = [t_inv[b] + prod[b] for b in units]
        t_inv = [t_inv[b] + _mm3(_split(t_inv[b]), _split(d4[b])) for b in units]
        for lvl in merge_masks:
            ts = [_split(t_inv[b]) for b in units]
            tl = [_mm3(ts[b], _split(jnp.where(lvl, low[b], 0.0))) for b in units]
            t_inv = [t_inv[b] - _mm3(_split(tl[b]), ts[b]) for b in units]
        eg = [jnp.exp(g[b]) for b in units]
        ts = [_split(t_inv[b]) for b in units]
        uw = [_mm3(ts[b], _split(jnp.concatenate([v[b] * beta[b], kb[b] * eg[b]], axis=1))) for b in units]
        a_qk = [_mm_nt(q[b], k[b]) * dec_incl[b] for b in units]
        g_last = [g[b][last:last + 1, :] for b in units]
        st = [st_ref[b] for b in units]
        v_new = [uw[b][:, :DV_B] - _mm(uw[b][:, DV_B:], st[b]) for b in units]
        o = [_mm(q[b] * eg[b], st[b]) + _mm(a_qk[b], v_new[b]) for b in units]
        for b in units:
            k_g = k[b] * jnp.exp(g_last[b] - g[b])
            st_ref[b] = st[b] * jnp.exp(g_last[b]) + _mm_tn(k_g, v_new[b])
            o_ref[hd[b], rows[b], :] = o[b]
        return carry

    lax.fori_loop(0, TB // CHUNK, chunk_body, 0)

    @pl.when(is_ctx)
    def _():
        fin_ref[0] = st_ref[...].reshape(fin_ref.shape[1:])


DELTA_HEADS = 2


def _delta(qn, kn, vn, gates, s0, *, d, gc, nd, rev):
    rows = qn.shape[1]
    nblk = gc + nd
    dh = DELTA_HEADS
    blk = lambda h, s: _step_blk(s, gc, nd, rev)
    slab = pl.BlockSpec((dh, RB, LANES), lambda h, s: (h, blk(h, s), 0))
    scr = pltpu.VMEM((dh, RB, LANES), f32)
    return pl.pallas_call(
        functools.partial(_delta_kernel, gc=gc, rev=rev, d=d),
        grid=(H_B // dh, nblk),
        in_specs=[
            slab, slab, slab,
            pl.BlockSpec((RB, LANES), lambda h, s: (blk(h, s), 0)),
            pl.BlockSpec((dh, SUBLANES, DK_B, DV_B), lambda h, s: (h, 0, 0, 0)),
        ],
        out_specs=[
            pl.BlockSpec((dh, RB, LANES), lambda h, s: (h, blk(h, s), 0)),
            pl.BlockSpec((1, dh, SUBLANES, DK_B, DV_B), lambda h, s: (jnp.minimum(s, gc - 1), h, 0, 0, 0)),
        ],
        out_shape=[
            jax.ShapeDtypeStruct((H_B, rows, LANES), f32),
            jax.ShapeDtypeStruct((gc, H_B, SUBLANES, DK_B, DV_B), f32),
        ],
        scratch_shapes=[scr, scr, scr, pltpu.VMEM((dh * SUBLANES, DK_B, DV_B), f32)],
        compiler_params=_cparams(("parallel", "arbitrary")),
        name="delta_bwd" if rev else "delta_fwd",
    )(qn, kn, vn, gates, s0)


def _s5_kernel(u_ref, bt_ref, ct_ref, lam_ref, s0_ref, y_ref, fin_ref, sbuf_ref, carry_ref, *, gc, rev):
    s = pl.program_id(1)
    half = sbuf_ref.shape[1] // 2

    @pl.when(s < gc)
    def _():
        carry_ref[...] = jnp.zeros_like(carry_ref)

    @pl.when(s == gc)
    def _():
        carry_ref[...] = s0_ref[0]

    nsl = u_ref.shape[0]
    lr = jnp.broadcast_to(lam_ref[0, 0:1, :], (SUBLANES, half))
    li = jnp.broadcast_to(lam_ref[0, 1:2, :], (SUBLANES, half))
    nq = TB // S5_TQ
    qrows = S5_TQ * SUBLANES
    order = list(range(nq - 1, -1, -1)) if rev else list(range(nq))

    def project_in(qb):
        rs = slice(qb * qrows, (qb + 1) * qrows)
        sbuf_ref[rs, :] = _mm(jnp.concatenate([u_ref[j, rs, :] for j in range(nsl)], axis=1), bt_ref[0])

    def project_out(qb):
        rs = slice(qb * qrows, (qb + 1) * qrows)
        y = _mm(sbuf_ref[rs, :], ct_ref[0])
        for j in range(nsl):
            y_ref[j, rs, :] = y[:, j * LANES:(j + 1) * LANES]

    hr, hi = carry_ref[:, :half], carry_ref[:, half:]
    project_in(order[0])
    for n, qb in enumerate(order):
        if n + 1 < nq:
            project_in(order[n + 1])
        for i in range(S5_TQ):
            t = qb * S5_TQ + ((S5_TQ - 1 - i) if rev else i)
            rs = slice(t * SUBLANES, (t + 1) * SUBLANES)
            hr, hi = (lr * hr - li * hi + sbuf_ref[rs, :half], lr * hi + li * hr + sbuf_ref[rs, half:])
            sbuf_ref[rs, :half] = hr
            sbuf_ref[rs, half:] = hi
        project_out(qb)
    carry_ref[:, :half] = hr
    carry_ref[:, half:] = hi

    @pl.when(s < gc)
    def _():
        fin_ref[0, 0] = carry_ref[...]


S5_TQ = 64
S5_COLS = LANES
S5_GPC = S5_COLS // S5_GW
S5_SW = 2 * S5_GPC * S5_N


def _s5(z, bt, ct, lam, s0, *, gc, nd, rev):
    rows = z.shape[1]
    nblk = gc + nd
    ngt = D_C // S5_COLS
    nsl = S5_COLS // LANES
    sw = S5_SW
    blk = lambda g, s: _step_blk(s, gc, nd, rev)
    return pl.pallas_call(
        functools.partial(_s5_kernel, gc=gc, rev=rev),
        grid=(ngt, nblk),
        in_specs=[
            pl.BlockSpec((nsl, RB, LANES), lambda g, s: (O_U // nsl + g, blk(g, s), 0)),
            pl.BlockSpec((1, S5_COLS, sw), lambda g, s: (g, 0, 0)),
            pl.BlockSpec((1, sw, S5_COLS), lambda g, s: (g, 0, 0)),
            pl.BlockSpec((1, 2, sw // 2), lambda g, s: (g, 0, 0)),
            pl.BlockSpec((1, SUBLANES, sw), lambda g, s: (g, 0, 0)),
        ],
        out_specs=[
            pl.BlockSpec((nsl, RB, LANES), lambda g, s: (g, blk(g, s), 0)),
            pl.BlockSpec((1, 1, SUBLANES, sw), lambda g, s: (jnp.minimum(s, gc - 1), g, 0, 0)),
        ],
        out_shape=[
            jax.ShapeDtypeStruct((D_C // LANES, rows, LANES), f32),
            jax.ShapeDtypeStruct((gc, ngt, SUBLANES, sw), f32),
        ],
        scratch_shapes=[pltpu.VMEM((RB, sw), f32), pltpu.VMEM((SUBLANES, sw), f32)],
        compiler_params=_cparams(("parallel", "arbitrary")),
        name="s5_bwd" if rev else "s5_fwd",
    )(z, bt, ct, lam, s0)


def _s5_out_kernel(yf_ref, yb_ref, u_ref, d_ref, w_ref, b_ref, o_ref):
    ns = yf_ref.shape[0]
    yc = jnp.concatenate([yf_ref[s] + yb_ref[s] + d_ref[s] * u_ref[s] for s in range(ns)], axis=1)
    zc = jax.nn.gelu(yc)
    gate = _sigmoid(jnp.dot(zc.astype(bf16), w_ref[...], preferred_element_type=f32) + b_ref[...])
    o_ref[...] = (zc * gate).astype(o_ref.dtype)


def _s5_out(yf, yb, z, dskip, w_glu, b_glu, *, tm):
    ns, rows, _ = yf.shape
    return pl.pallas_call(
        _s5_out_kernel,
        grid=(rows // tm,),
        in_specs=[
            pl.BlockSpec((ns, tm, LANES), lambda i: (0, i, 0)),
            pl.BlockSpec((ns, tm, LANES), lambda i: (0, i, 0)),
            pl.BlockSpec((ns, tm, LANES), lambda i: (O_U // ns, i, 0)),
            pl.BlockSpec((ns, 1, LANES), lambda i: (0, 0, 0)),
            pl.BlockSpec((D_C, D_C), lambda i: (0, 0)),
            pl.BlockSpec((1, D_C), lambda i: (0, 0)),
        ],
        out_specs=pl.BlockSpec((tm, D_C), lambda i: (i, 0)),
        out_shape=jax.ShapeDtypeStruct((rows, D_C), bf16),
        compiler_params=_cparams(("parallel",)),
        name="s5_out",
    )(yf, yb, z, dskip, w_glu, b_glu)


def _gla_kernel(q_ref, k_ref, v_ref, glr_ref, wg_ref, bg_ref, s0_ref, o_ref, fin_ref,
                la_ref, gs_ref, st_ref, *, gc, rev):
    s = pl.program_id(1)
    is_ctx = s < gc

    @pl.when(is_ctx)
    def _():
        st_ref[...] = jnp.zeros_like(st_ref)

    @pl.when(s == gc)
    def _():
        st_ref[...] = s0_ref[0]

    la_ref[...] = jax.nn.log_sigmoid(_mm(glr_ref[0], wg_ref[0]) + bg_ref[0]) * (1.0 / GLA_TAU)
    _tile_cumsum(la_ref, gs_ref, rev)

    ii = lax.broadcasted_iota(jnp.int32, (SUB, CHUNK), 0)
    jj = lax.broadcasted_iota(jnp.int32, (SUB, CHUNK), 1)
    last = 0 if rev else CHUNK - 1

    def chunk_body(ci, carry):
        c = (TB // CHUNK - 1 - ci) if rev else ci
        units = range(SUBLANES)
        rows = [pl.ds(c * CROWS + b, CHUNK, stride=SUBLANES) for b in units]
        q = [q_ref[0, r, :] * (DK_D ** -0.5) for r in rows]
        k = [k_ref[0, r, :] for r in rows]
        v = [jnp.concatenate([v_ref[0, r, :], v_ref[1, r, :]], axis=1) for r in rows]
        g = [gs_ref[r, :] for r in rows]
        a_rows = [[] for _ in units]
        for blk in range(CHUNK // SUB):
            lo = blk * SUB
            first = lo + SUB - 1 if rev else lo
            causal = (jj >= ii + lo) if rev else (jj <= ii + lo)
            for b in units:
                g_first = g[b][first:first + 1, :]
                qt = q[b][lo:lo + SUB, :] * jnp.exp(g[b][lo:lo + SUB, :] - g_first)
                kt = k[b] * jnp.exp(g_first - g[b])
                a_rows[b].append(jnp.where(causal, _mm_nt(qt, kt), 0.0))
        a_qk = [jnp.concatenate(a_rows[b], axis=0) for b in units]
        g_last = [g[b][last:last + 1, :] for b in units]
        st = [st_ref[b] for b in units]
        o = [_mm_nt(q[b] * jnp.exp(g[b]), st[b]) + _mm(a_qk[b], v[b]) for b in units]
        for b in units:
            k_g = k[b] * jnp.exp(g_last[b] - g[b])
            st_ref[b] = st[b] * jnp.exp(g_last[b]) + _mm_tn(v[b], k_g)
            o_ref[0, rows[b], :] = o[b][:, :LANES]
            o_ref[1, rows[b], :] = o[b][:, LANES:]
        return carry

    lax.fori_loop(0, TB // CHUNK, chunk_body, 0)

    @pl.when(is_ctx)
    def _():
        fin_ref[0, 0] = st_ref[...]


def _gla(z, wg, bg, s0, *, gc, nd, rev):
    rows = z.shape[1]
    nblk = gc + nd
    nv = DV_D // LANES
    blk = lambda h, s: _step_blk(s, gc, nd, rev)
    scr = pltpu.VMEM((RB, LANES), f32)
    return pl.pallas_call(
        functools.partial(_gla_kernel, gc=gc, rev=rev),
        grid=(H_D, nblk),
        in_specs=[
            pl.BlockSpec((1, RB, LANES), lambda h, s: (O_Q + h, blk(h, s), 0)),
            pl.BlockSpec((1, RB, LANES), lambda h, s: (O_K + h, blk(h, s), 0)),
            pl.BlockSpec((nv, RB, LANES), lambda h, s: (O_V // nv + h, blk(h, s), 0)),
            pl.BlockSpec((1, RB, LANES), lambda h, s: (O_GLR, blk(h, s), 0)),
            pl.BlockSpec((1, LANES, LANES), lambda h, s: (h, 0, 0)),
            pl.BlockSpec((1, 1, LANES), lambda h, s: (h, 0, 0)),
            pl.BlockSpec((1, SUBLANES, DV_D, DK_D), lambda h, s: (h, 0, 0, 0)),
        ],
        out_specs=[
            pl.BlockSpec((nv, RB, LANES), lambda h, s: (h, blk(h, s), 0)),
            pl.BlockSpec((1, 1, SUBLANES, DV_D, DK_D), lambda h, s: (jnp.minimum(s, gc - 1), h, 0, 0, 0)),
        ],
        out_shape=[
            jax.ShapeDtypeStruct((H_D * nv, rows, LANES), f32),
            jax.ShapeDtypeStruct((gc, H_D, SUBLANES, DV_D, DK_D), f32),
        ],
        scratch_shapes=[scr, scr, pltpu.VMEM((SUBLANES, DV_D, DK_D), f32)],
        compiler_params=_cparams(("parallel", "arbitrary")),
        name="gla_bwd" if rev else "gla_fwd",
    )(z, z, z, z, wg, bg, s0)


TL = 64


def _rows_in_kernel(xp_ref, xs_ref, o_ref, *, nctx):
    def emit(ref):
        x = ref[0]
        o_ref[...] = jnp.swapaxes(x, 0, 1).reshape(TL * SUBLANES, x.shape[2])

    pl.when(pl.program_id(0) < nctx)(lambda: emit(xp_ref))
    pl.when(pl.program_id(0) >= nctx)(lambda: emit(xs_ref))


def _rows_in(x_prompt, x_sample):
    bp, tp, d = x_prompt.shape
    bs, ts, _ = x_sample.shape
    xp = x_prompt.reshape(bp // SUBLANES, SUBLANES, tp, d)
    xs = x_sample.reshape(bs // SUBLANES, SUBLANES, ts, d)
    per = tp // TL
    nctx = (bp // SUBLANES) * per
    nblk = nctx + ts // TL

    def ctx_idx(i):
        j = jnp.minimum(i, nctx - 1)
        return (j // per, 0, j % per, 0)

    return pl.pallas_call(
        functools.partial(_rows_in_kernel, nctx=nctx),
        grid=(nblk,),
        in_specs=[
            pl.BlockSpec((1, SUBLANES, TL, d), ctx_idx),
            pl.BlockSpec((1, SUBLANES, TL, d), lambda i: (0, 0, jnp.maximum(i - nctx, 0), 0)),
        ],
        out_specs=pl.BlockSpec((TL * SUBLANES, d), lambda i: (i, 0)),
        out_shape=jax.ShapeDtypeStruct((bp * tp + bs * ts, d), x_prompt.dtype),
        compiler_params=_cparams(("arbitrary",)),
        name="rows_in",
    )(xp, xs)


def _rows_out_kernel(x_ref, yp_ref, ys_ref, *, nctx):
    x = x_ref[...]
    y = jnp.swapaxes(x.reshape(TL, SUBLANES, x.shape[1]), 0, 1)

    @pl.when(pl.program_id(0) < nctx)
    def _():
        yp_ref[0] = y

    @pl.when(pl.program_id(0) >= nctx)
    def _():
        ys_ref[0] = y


def _rows_out(x, bp, tp, bs, ts):
    d = x.shape[1]
    per = tp // TL
    nctx = (bp // SUBLANES) * per
    nblk = nctx + ts // TL

    def ctx_idx(i):
        j = jnp.minimum(i, nctx - 1)
        return (j // per, 0, j % per, 0)

    yp, ys = pl.pallas_call(
        functools.partial(_rows_out_kernel, nctx=nctx),
        grid=(nblk,),
        in_specs=[pl.BlockSpec((TL * SUBLANES, d), lambda i: (i, 0))],
        out_specs=[
            pl.BlockSpec((1, SUBLANES, TL, d), ctx_idx),
            pl.BlockSpec((1, SUBLANES, TL, d), lambda i: (0, 0, jnp.maximum(i - nctx, 0), 0)),
        ],
        out_shape=[
            jax.ShapeDtypeStruct((bp // SUBLANES, SUBLANES, tp, d), x.dtype),
            jax.ShapeDtypeStruct((bs // SUBLANES, SUBLANES, ts, d), x.dtype),
        ],
        compiler_params=_cparams(("arbitrary",)),
        name="rows_out",
    )(x)
    return yp.reshape(bp, tp, d), ys.reshape(bs, ts, d)


def _pad_cols(w, n):
    return jnp.pad(w, ((0, 0), (0, n - w.shape[1])))


def _unit_states(fin, b):
    fin = jnp.moveaxis(fin, b, 1)
    return fin.reshape((fin.shape[0] * SUBLANES,) + fin.shape[2:])


def kernel(x_prompt, x_sample, state_lru, state_delta, state_s5_re, state_s5_im, state_gla, c, c_ctx, w_ada, b_ada, norm_g, w_up, w_down, w_in_e, w_out_e, conv_a, lru_wa, lru_ba, lru_wx, lru_bx, lru_lambda, conv_qkv, dn_a_log, dn_dt_bias, dn_norm_g, w_in_o, w_out_o, s5_lam_re, s5_lam_im, s5_log_step, s5_b_re, s5_b_im, s5_c_re, s5_c_im, s5_d, s5_w_glu, s5_b_glu, gla_wg2, gla_bg, gla_norm_g):
    bp, tp, d = x_prompt.shape
    bs, ts, _ = x_sample.shape
    assert tp == TB and bp % SUBLANES == 0 and bs == SUBLANES and ts % TB == 0 and d == D_MODEL
    depth = w_ada.shape[0]
    gc = bp // SUBLANES
    nd = ts // TB
    nblk = gc + nd
    rows_ctx = bp * tp
    seq = dict(gc=gc, nd=nd)

    x = _rows_in(x_prompt, x_sample)

    cvec = jnp.concatenate([c, jnp.broadcast_to(c_ctx[None], (SUBLANES, d))], axis=0)
    mod = _ada(cvec, w_ada, b_ada)
    mod = jnp.stack([mod[:, SUBLANES:], mod[:, :SUBLANES]], axis=1)
    mods = [[mod[l, :, :, i * d:(i + 1) * d] for i in range(6)] for l in range(depth)]

    fin_lru, fin_delta, fin_re, fin_im, fin_gla = [], [], [], [], []
    for l in range(depth):
        sh1, sc1, gt1, sh2, sc2, gt2 = mods[l]
        if l % 2 == 0:
            e = l // 2
            w_in = _pad_cols(w_in_e[e], E_SLABS * LANES).astype(bf16)
            z = _nmm(x, norm_g[l, 0], sh1, sc1, w_in, rows_ctx=rows_ctx, tm=256, tn=10 * LANES, slab=True)
            ns = D_A // LANES
            wg = jnp.stack([lru_wa[e], lru_wx[e]], axis=1).reshape(2, 2, ns, 2, LRU_BW, LRU_BW)
            wbd = jnp.zeros((2, 2, ns, LANES, LANES), f32)
            wbd = wbd.at[:, :, :, :LRU_BW, :LRU_BW].set(wg[:, :, :, 0]).at[:, :, :, LRU_BW:, LRU_BW:].set(wg[:, :, :, 1])
            bias = jnp.stack([lru_ba[e], lru_bx[e]], axis=1).reshape(2, 2, ns, 1, LANES)
            sp = jax.nn.softplus(-lru_lambda[e]).reshape(2, ns, 1, LANES)
            cwa = conv_a[e].reshape(4, ns, LANES).transpose(1, 0, 2)
            a, b = _lru_gates(z, cwa, wbd.astype(bf16), bias, sp, gc=gc, nblk=nblk)
            h_f, f_f = _lru_scan(a, b, state_lru[:, e, 0], d=0, rev=False, **seq)
            h_b, f_b = _lru_scan(a, b, state_lru[:, e, 1], d=1, rev=True, **seq)
            fin_lru.append(jnp.stack([_unit_states(f_f, 1), _unit_states(f_b, 1)], axis=1))
            cq = conv_qkv[e].reshape(4, 3, H_B, LANES).transpose(1, 2, 0, 3)
            lane_pad = lambda v: jnp.pad(v.reshape(1, 2 * H_B), ((0, 0), (2 * H_B, LANES - 4 * H_B)))
            nega = lane_pad(-jnp.exp(dn_a_log[e]))
            dtb = lane_pad(dn_dt_bias[e])
            s0 = state_delta[:, e].transpose(1, 2, 0, 3, 4)
            qn, kn, vn, gates = _delta_prep(z, cq, nega, dtb, gc=gc, nblk=nblk)
            o_f, s_f = _delta(qn, kn, vn, gates, s0[0], d=0, rev=False, **seq)
            o_b, s_b = _delta(qn, kn, vn, gates, s0[1], d=1, rev=True, **seq)
            fin_delta.append(jnp.stack([_unit_states(s_f, 2), _unit_states(s_b, 2)], axis=1))
            x = _mixout_even(h_f, h_b, o_f, o_b, z, dn_norm_g[e], w_out_e[e].astype(bf16), x, norm_g[l, 1], gt1,
                             rows_ctx=rows_ctx, tm=512)
        else:
            o = l // 2
            w_in = _pad_cols(w_in_o[o], O_SLABS * LANES).astype(bf16)
            z = _nmm(x, norm_g[l, 0], sh1, sc1, w_in, rows_ctx=rows_ctx, tm=256, tn=11 * LANES, slab=True)
            ngt = D_C // S5_COLS
            gps = S5_GPC
            lam = lax.complex(s5_lam_re[o], s5_lam_im[o])
            lam_bar = jnp.exp(lam * jnp.exp(s5_log_step[o])[..., None])
            b_bar = ((lam_bar - 1.0) / lam)[..., None] * lax.complex(s5_b_re[o], s5_b_im[o])
            eye_g = jnp.eye(gps, dtype=f32)

            def bmat(p):
                p = p.reshape(2, ngt, gps, S5_N, S5_GW)
                return jnp.einsum('dtgnw,gh->dtgwhn', p, eye_g).reshape(2, ngt, S5_COLS, gps * S5_N)

            def cmat(p):
                p = p.reshape(2, ngt, gps, S5_GW, S5_N)
                return jnp.einsum('dtgwn,gh->dtgnhw', p, eye_g).reshape(2, ngt, gps * S5_N, S5_COLS)

            bt = jnp.concatenate([bmat(b_bar.real), bmat(b_bar.imag)], axis=-1).astype(bf16)
            ct = jnp.concatenate([cmat(s5_c_re[o]), -cmat(s5_c_im[o])], axis=-2).astype(bf16)
            lamv = jnp.stack([lam_bar.real, lam_bar.imag], axis=2).reshape(2, ngt, gps, 2, S5_N)
            lamv = lamv.transpose(0, 1, 3, 2, 4).reshape(2, ngt, 2, gps * S5_N)

            def s5_state(sre, sim):
                f = lambda p: p.reshape(SUBLANES, ngt, gps * S5_N).transpose(1, 0, 2)
                return jnp.concatenate([f(sre), f(sim)], axis=-1)

            y_f, c_f = _s5(z, bt[0], ct[0], lamv[0], s5_state(state_s5_re[:, o, 0], state_s5_im[:, o, 0]), rev=False, **seq)
            y_bk, c_b = _s5(z, bt[1], ct[1], lamv[1], s5_state(state_s5_re[:, o, 1], state_s5_im[:, o, 1]), rev=True, **seq)
            y_c = _s5_out(y_f, y_bk, z, s5_d[o].reshape(D_C // LANES, 1, LANES), s5_w_glu[o].astype(bf16),
                          s5_b_glu[o].reshape(1, D_C), tm=512)

            def s5_fin(cf):
                cf = cf.transpose(0, 2, 1, 3).reshape(gc * SUBLANES, ngt, 2, gps, S5_N)
                return (cf[:, :, 0].reshape(gc * SUBLANES, S5_G, S5_N), cf[:, :, 1].reshape(gc * SUBLANES, S5_G, S5_N))

            (rf, imf), (rb, imb) = s5_fin(c_f), s5_fin(c_b)
            fin_re.append(jnp.stack([rf, rb], axis=1))
            fin_im.append(jnp.stack([imf, imb], axis=1))
            wgp = jnp.zeros((2, H_D, LANES, LANES), f32)
            wg2 = gla_wg2[o].reshape(2, GLA_RANK, H_D, DK_D).transpose(0, 2, 1, 3)
            wgp = wgp.at[0, :, :GLA_RANK].set(wg2[0]).at[1, :, GLA_RANK:2 * GLA_RANK].set(wg2[1]).astype(bf16)
            bgp = gla_bg[o].reshape(2, H_D, 1, DK_D)
            g0 = state_gla[:, o].transpose(1, 2, 0, 4, 3)
            o_f, g_f = _gla(z, wgp[0], bgp[0], g0[0], rev=False, **seq)
            o_b, g_b = _gla(z, wgp[1], bgp[1], g0[1], rev=True, **seq)
            gfin = lambda gf: jnp.swapaxes(_unit_states(gf, 2), -1, -2)
            fin_gla.append(jnp.stack([gfin(g_f), gfin(g_b)], axis=1))
            x = _mixout_odd(y_c, o_f, o_b, z, gla_norm_g[o], w_out_o[o].astype(bf16), x, norm_g[l, 1], gt1,
                            rows_ctx=rows_ctx, tm=512)

        hmid = _nmm(x, norm_g[l, 2], sh2, sc2, w_up[l].astype(bf16), rows_ctx=rows_ctx, tm=256, tn=2048,
                    relu2=True, out_dtype=bf16)
        x = _mmres(hmid, w_down[l].astype(bf16), x, norm_g[l, 3], gt2, rows_ctx=rows_ctx, tm=256)

    y_prompt, y_sample = _rows_out(x, bp, tp, bs, ts)
    return (y_prompt, y_sample, jnp.stack(fin_lru, axis=1), jnp.stack(fin_delta, axis=1),
            jnp.stack(fin_re, axis=1), jnp.stack(fin_im, axis=1), jnp.stack(fin_gla, axis=1))
```

```python
import functools
import math

import jax
import jax.numpy as jnp
from jax import lax
from jax.experimental import pallas as pl
from jax.experimental.pallas import tpu as pltpu

f32 = jnp.float32
bf16 = jnp.bfloat16

LANES = 128
SUBLANES = 8
VMEM_LIMIT_BYTES = 56 * 1024 * 1024

D_MODEL = 2048
D_FF = 4 * D_MODEL
GRID_W = 64
CHUNK = 64
EPS = 1e-6
D_A = D_MODEL // 2
LRU_BLOCKS = 16
LRU_BW = D_A // LRU_BLOCKS
LRU_C = 8.0
H_B = 8
DK_B = 128
DV_B = 128
D_C = D_MODEL // 2
S5_GW = 16
S5_G = D_C // S5_GW
S5_N = 64
H_D = 4
DK_D = 128
DV_D = 256
GLA_RANK = 16
GLA_TAU = 16.0

TB = 256
RB = TB * SUBLANES
CROWS = CHUNK * SUBLANES
SUB = 16

E_XA, E_GA, E_Q, E_K, E_V, E_GO, E_BA = 0, 8, 16, 24, 32, 40, 48
E_SLABS = 50
O_U, O_Q, O_K, O_V, O_R, O_GLR = 0, 8, 12, 16, 24, 32
O_SLABS = 33


def _cparams(sem):
    return pltpu.CompilerParams(dimension_semantics=sem, vmem_limit_bytes=VMEM_LIMIT_BYTES)


def _mm(a, b):
    return jnp.dot(a.astype(bf16), b.astype(bf16), preferred_element_type=f32)


def _mm_nt(a, b):
    return lax.dot_general(a.astype(bf16), b.astype(bf16), (((1,), (1,)), ((), ())), preferred_element_type=f32)


def _mm_tn(a, b):
    return lax.dot_general(a.astype(bf16), b.astype(bf16), (((0,), (0,)), ((), ())), preferred_element_type=f32)


def _split(x):
    hi = x.astype(bf16)
    return hi, (x - hi.astype(f32)).astype(bf16)


def _mm3(a, b):
    dot = functools.partial(jnp.dot, preferred_element_type=f32)
    return dot(a[0], b[0]) + (dot(a[0], b[1]) + dot(a[1], b[0]))


def _sigmoid(x):
    return 0.5 * jnp.tanh(0.5 * x) + 0.5


def _silu(x):
    t = 0.5 * x
    return t + t * jnp.tanh(t)


def _ada_kernel(c_ref, w_ref, b_ref, o_ref):
    a = _silu(c_ref[...])
    o_ref[0] = _mm(a, w_ref[0]) + b_ref[0]


def _ada(cvec, w_ada, b_ada):
    depth, d, n = w_ada.shape
    tn = 1024
    return pl.pallas_call(
        _ada_kernel,
        grid=(depth, n // tn),
        in_specs=[
            pl.BlockSpec(cvec.shape, lambda l, j: (0, 0)),
            pl.BlockSpec((1, d, tn), lambda l, j: (l, 0, j)),
            pl.BlockSpec((1, 1, tn), lambda l, j: (l, 0, j)),
        ],
        out_specs=pl.BlockSpec((1, cvec.shape[0], tn), lambda l, j: (l, 0, j)),
        out_shape=jax.ShapeDtypeStruct((depth, cvec.shape[0], n), f32),
        compiler_params=_cparams(("parallel", "parallel")),
        name="ada",
    )(cvec, w_ada, b_ada.reshape(depth, 1, n))


def _nmm_kernel(x_ref, g_ref, sh_ref, sc_ref, w_ref, o_ref, *, relu2, slab, tn):
    x = x_ref[...]
    tm, d = x.shape
    y = x * lax.rsqrt(jnp.mean(x * x, axis=-1, keepdims=True) + EPS)
    gain = g_ref[...] * (1.0 + sc_ref[0])
    h = (y.reshape(tm // SUBLANES, SUBLANES, d) * gain[None] + sh_ref[0][None]).reshape(tm, d).astype(bf16)
    for c in range(w_ref.shape[1] // tn):
        acc = jnp.dot(h, w_ref[:, c * tn:(c + 1) * tn], preferred_element_type=f32)
        if relu2:
            acc = jnp.square(jnp.maximum(acc, 0.0))
        if slab:
            for s in range(tn // LANES):
                o_ref[c * (tn // LANES) + s] = acc[:, s * LANES:(s + 1) * LANES].astype(o_ref.dtype)
        else:
            o_ref[:, c * tn:(c + 1) * tn] = acc.astype(o_ref.dtype)


def _nmm(x, g, sh, sc, w, *, rows_ctx, tm, tn, relu2=False, slab=False, out_dtype=f32):
    rows, d = x.shape
    n = w.shape[1]
    assert n % tn == 0
    grp = lambda i: ((i * tm >= rows_ctx).astype(jnp.int32), 0, 0)
    if slab:
        out_shape = jax.ShapeDtypeStruct((n // LANES, rows, LANES), out_dtype)
        out_spec = pl.BlockSpec((n // LANES, tm, LANES), lambda i: (0, i, 0))
    else:
        out_shape = jax.ShapeDtypeStruct((rows, n), out_dtype)
        out_spec = pl.BlockSpec((tm, n), lambda i: (i, 0))
    return pl.pallas_call(
        functools.partial(_nmm_kernel, relu2=relu2, slab=slab, tn=tn),
        grid=(rows // tm,),
        in_specs=[
            pl.BlockSpec((tm, d), lambda i: (i, 0)),
            pl.BlockSpec((1, d), lambda i: (0, 0)),
            pl.BlockSpec((1, SUBLANES, d), grp),
            pl.BlockSpec((1, SUBLANES, d), grp),
            pl.BlockSpec((d, n), lambda i: (0, 0), pipeline_mode=pl.Buffered(1)),
        ],
        out_specs=out_spec,
        out_shape=out_shape,
        compiler_params=_cparams(("parallel",)),
        name="nmm",
    )(x, g.reshape(1, d), sh, sc, w)


def _mmres_kernel(a_ref, w_ref, x_ref, g_ref, gt_ref, o_ref):
    y = jnp.dot(a_ref[...], w_ref[...], preferred_element_type=f32)
    _gated_residual(y, x_ref, g_ref, gt_ref, o_ref)


def _gated_residual(y, x_ref, g_ref, gt_ref, o_ref):
    tm, d = y.shape
    yn = y * lax.rsqrt(jnp.mean(y * y, axis=-1, keepdims=True) + EPS) * g_ref[...]
    o = x_ref[...].reshape(tm // SUBLANES, SUBLANES, d) + gt_ref[0][None] * yn.reshape(tm // SUBLANES, SUBLANES, d)
    o_ref[...] = o.reshape(tm, d)


def _headnorm_gate(o_slabs, gain, gate_slabs):
    width = len(o_slabs) * LANES
    ms = sum(jnp.sum(o * o, axis=-1, keepdims=True) for o in o_slabs) * (1.0 / width)
    inv = lax.rsqrt(ms + EPS)
    return [(o * inv * gain[j] * _silu(r)).astype(bf16) for j, (o, r) in enumerate(zip(o_slabs, gate_slabs))]


def _mixout_even_kernel(hf_ref, hb_ref, ga_ref, of_ref, ob_ref, go_ref, dg_ref, w1_ref, w2_ref,
                        x_ref, g_ref, gt_ref, o_ref):
    ns = ga_ref.shape[0]
    ga = jnp.concatenate([ga_ref[s] for s in range(ns)], axis=1)
    y_a = ((hf_ref[...] + hb_ref[...]) * jax.nn.gelu(ga)).astype(bf16)
    y = jnp.dot(y_a, w1_ref[...], preferred_element_type=f32)
    y_b = [_headnorm_gate([of_ref[h] + ob_ref[h]], [dg_ref[...]], [go_ref[h]])[0] for h in range(of_ref.shape[0])]
    y = y + jnp.dot(jnp.concatenate(y_b, axis=1), w2_ref[...], preferred_element_type=f32)
    _gated_residual(y, x_ref, g_ref, gt_ref, o_ref)


def _mixout_odd_kernel(yc_ref, of_ref, ob_ref, r_ref, dg_ref, w1_ref, w2_ref, x_ref, g_ref, gt_ref, o_ref):
    nv = DV_D // LANES
    y = jnp.dot(yc_ref[...], w1_ref[...], preferred_element_type=f32)
    y_d = []
    for h in range(of_ref.shape[0] // nv):
        sl = range(h * nv, (h + 1) * nv)
        y_d += _headnorm_gate([of_ref[j] + ob_ref[j] for j in sl], [dg_ref[j] for j in range(nv)], [r_ref[j] for j in sl])
    y = y + jnp.dot(jnp.concatenate(y_d, axis=1), w2_ref[...], preferred_element_type=f32)
    _gated_residual(y, x_ref, g_ref, gt_ref, o_ref)


def _mixout_specs(w, x, g, gt, rows_ctx, tm):
    kh = w.shape[0] // 2
    d = w.shape[1]
    grp = lambda i: ((i * tm >= rows_ctx).astype(jnp.int32), 0, 0)
    specs = [
        pl.BlockSpec((kh, d), lambda i: (0, 0), pipeline_mode=pl.Buffered(1)),
        pl.BlockSpec((kh, d), lambda i: (1, 0), pipeline_mode=pl.Buffered(1)),
        pl.BlockSpec((tm, d), lambda i: (i, 0)),
        pl.BlockSpec((1, d), lambda i: (0, 0)),
        pl.BlockSpec((1, SUBLANES, d), grp),
    ]
    return specs, (w, w, x, g.reshape(1, d), gt)


def _mixout_even(hf, hb, of, ob, z, dn_g, w, x, g, gt, *, rows_ctx, tm):
    rows, d = x.shape
    ns = D_A // LANES
    slabs = lambda off: pl.BlockSpec((ns, tm, LANES), lambda i: (off // ns, i, 0))
    tail_specs, tail_args = _mixout_specs(w, x, g, gt, rows_ctx, tm)
    return pl.pallas_call(
        _mixout_even_kernel,
        grid=(rows // tm,),
        in_specs=[
            pl.BlockSpec((tm, D_A), lambda i: (i, 0)),
            pl.BlockSpec((tm, D_A), lambda i: (i, 0)),
            slabs(E_GA), slabs(0), slabs(0), slabs(E_GO),
            pl.BlockSpec((1, LANES), lambda i: (0, 0)),
        ] + tail_specs,
        out_specs=pl.BlockSpec((tm, d), lambda i: (i, 0)),
        out_shape=jax.ShapeDtypeStruct((rows, d), f32),
        compiler_params=_cparams(("parallel",)),
        name="mixout_even",
    )(hf, hb, z, of, ob, z, dn_g.reshape(1, LANES), *tail_args)


def _mixout_odd(y_c, of, ob, z, gla_g, w, x, g, gt, *, rows_ctx, tm):
    rows, d = x.shape
    ns = H_D * DV_D // LANES
    nv = DV_D // LANES
    slabs = lambda off: pl.BlockSpec((ns, tm, LANES), lambda i: (off // ns, i, 0))
    tail_specs, tail_args = _mixout_specs(w, x, g, gt, rows_ctx, tm)
    return pl.pallas_call(
        _mixout_odd_kernel,
        grid=(rows // tm,),
        in_specs=[
            pl.BlockSpec((tm, D_C), lambda i: (i, 0)),
            slabs(0), slabs(0), slabs(O_R),
            pl.BlockSpec((nv, 1, LANES), lambda i: (0, 0, 0)),
        ] + tail_specs,
        out_specs=pl.BlockSpec((tm, d), lambda i: (i, 0)),
        out_shape=jax.ShapeDtypeStruct((rows, d), f32),
        compiler_params=_cparams(("parallel",)),
        name="mixout_odd",
    )(y_c, of, ob, z, gla_g.reshape(nv, 1, LANES), *tail_args)


def _mmres(a, w, x, g, gt, *, rows_ctx, tm):
    rows, kdim = a.shape
    d = w.shape[1]
    grp = lambda i: ((i * tm >= rows_ctx).astype(jnp.int32), 0, 0)
    return pl.pallas_call(
        _mmres_kernel,
        grid=(rows // tm,),
        in_specs=[
            pl.BlockSpec((tm, kdim), lambda i: (i, 0)),
            pl.BlockSpec((kdim, d), lambda i: (0, 0), pipeline_mode=pl.Buffered(1)),
            pl.BlockSpec((tm, d), lambda i: (i, 0)),
            pl.BlockSpec((1, d), lambda i: (0, 0)),
            pl.BlockSpec((1, SUBLANES, d), grp),
        ],
        out_specs=pl.BlockSpec((tm, d), lambda i: (i, 0)),
        out_shape=jax.ShapeDtypeStruct((rows, d), f32),
        compiler_params=_cparams(("parallel",)),
        name="mmres",
    )(a, w, x, g.reshape(1, d), gt)


def _step_blk(s, gc, nd, rev):
    if not rev:
        return s
    return jnp.where(s < gc, s, 2 * gc + nd - 1 - s)


def _conv4(x, w4, is_ctx):
    rows = x.shape[0]
    t = lax.broadcasted_iota(jnp.int32, x.shape, 0) // SUBLANES
    tl = jnp.where(is_ctx, t, t % GRID_W)
    last = jnp.where(is_ctx, TB - 1, GRID_W - 1)
    y = x * w4[2:3, :]
    xm2 = pltpu.roll(x, 2 * SUBLANES, 0)
    y = y + jnp.where(tl >= 2, xm2, 0.0) * w4[0:1, :]
    xm1 = pltpu.roll(x, SUBLANES, 0)
    y = y + jnp.where(tl >= 1, xm1, 0.0) * w4[1:2, :]
    xp1 = pltpu.roll(x, rows - SUBLANES, 0)
    y = y + jnp.where(tl < last, xp1, 0.0) * w4[3:4, :]
    return y


def _tile_cumsum(src_ref, dst_ref, rev):
    ntile = src_ref.shape[0] // SUBLANES
    run = None
    for i in range(ntile):
        r = ((ntile - 1 - i) if rev else i) * SUBLANES
        x = src_ref[r:r + SUBLANES, :]
        run = x if i % CHUNK == 0 else run + x
        dst_ref[r:r + SUBLANES, :] = run


def _pick_lane(x, lane):
    onehot = lax.broadcasted_iota(jnp.int32, x.shape, 1) == lane
    col = jnp.sum(jnp.where(onehot, x, 0.0), axis=-1, keepdims=True)
    return jnp.broadcast_to(col, x.shape)


def _lru_gate_kernel(xa_ref, cw_ref, w_ref, bias_ref, sp_ref, a_ref, b_ref, *, gc):
    is_ctx = pl.program_id(0) < gc
    u = _conv4(xa_ref[0], cw_ref[0], is_ctx)
    for d in range(2):
        r = _sigmoid(_mm(u, w_ref[d, 0, 0]) + bias_ref[d, 0, 0])
        i = _sigmoid(_mm(u, w_ref[d, 1, 0]) + bias_ref[d, 1, 0])
        log_a = -LRU_C * r * sp_ref[d, 0]
        a = jnp.exp(log_a)
        b = jnp.sqrt(-jnp.tanh(log_a) * (a * a + 1.0)) * (i * u)
        a_ref[d] = a
        b_ref[d] = b


def _lru_gates(z, conv_a, wbd, bias, sp, *, gc, nblk):
    rows = z.shape[1]
    ns = D_A // LANES
    out = jax.ShapeDtypeStruct((2, rows, D_A), f32)
    return pl.pallas_call(
        functools.partial(_lru_gate_kernel, gc=gc),
        grid=(nblk, ns),
        in_specs=[
            pl.BlockSpec((1, RB, LANES), lambda i, s: (E_XA + s, i, 0)),
            pl.BlockSpec((1, 4, LANES), lambda i, s: (s, 0, 0)),
            pl.BlockSpec((2, 2, 1, LANES, LANES), lambda i, s: (0, 0, s, 0, 0)),
            pl.BlockSpec((2, 2, 1, 1, LANES), lambda i, s: (0, 0, s, 0, 0)),
            pl.BlockSpec((2, 1, 1, LANES), lambda i, s: (0, s, 0, 0)),
        ],
        out_specs=[pl.BlockSpec((2, RB, LANES), lambda i, s: (0, i, s))] * 2,
        out_shape=[out, out],
        compiler_params=_cparams(("parallel", "parallel")),
        name="lru_gates",
    )(z, conv_a, wbd, bias, sp)


def _lru_scan_kernel(a_ref, b_ref, h0_ref, h_ref, fin_ref, carry_ref, *, gc, rev):
    s = pl.program_id(1)

    @pl.when(s < gc)
    def _():
        carry_ref[...] = jnp.zeros_like(carry_ref)

    @pl.when(s == gc)
    def _():
        carry_ref[...] = h0_ref[...]

    def body(i, h):
        t = (TB - 1 - i) if rev else i
        r = pl.multiple_of(t * SUBLANES, SUBLANES)
        h = a_ref[0, pl.ds(r, SUBLANES), :] * h + b_ref[0, pl.ds(r, SUBLANES), :]
        h_ref[pl.ds(r, SUBLANES), :] = h
        return h

    h = lax.fori_loop(0, TB, body, carry_ref[...], unroll=8)
    carry_ref[...] = h

    @pl.when(s < gc)
    def _():
        fin_ref[0] = h


def _lru_scan(a, b, h0, *, d, gc, nd, rev):
    rows = a.shape[1]
    nblk = gc + nd
    wc = 512
    blk = lambda c, s: _step_blk(s, gc, nd, rev)
    return pl.pallas_call(
        functools.partial(_lru_scan_kernel, gc=gc, rev=rev),
        grid=(D_A // wc, nblk),
        in_specs=[
            pl.BlockSpec((1, RB, wc), lambda c, s: (d, blk(c, s), c)),
            pl.BlockSpec((1, RB, wc), lambda c, s: (d, blk(c, s), c)),
            pl.BlockSpec((SUBLANES, wc), lambda c, s: (0, c)),
        ],
        out_specs=[
            pl.BlockSpec((RB, wc), lambda c, s: (blk(c, s), c)),
            pl.BlockSpec((1, SUBLANES, wc), lambda c, s: (jnp.minimum(s, gc - 1), 0, c)),
        ],
        out_shape=[
            jax.ShapeDtypeStruct((rows, D_A), f32),
            jax.ShapeDtypeStruct((gc, SUBLANES, D_A), f32),
        ],
        scratch_shapes=[pltpu.VMEM((SUBLANES, wc), f32)],
        compiler_params=_cparams(("parallel", "arbitrary")),
        name="lru_scan_bwd" if rev else "lru_scan_fwd",
    )(a, b, h0)


def _delta_prep_kernel(q_ref, k_ref, v_ref, ba_ref, cw_ref, nega_ref, dtb_ref, qn_ref, kn_ref, vn_ref, gate_ref, *, gc):
    is_ctx = pl.program_id(0) < gc

    def l2n(x):
        return x * lax.rsqrt(jnp.sum(x * x, axis=-1, keepdims=True) + EPS)

    qn_ref[0] = l2n(_silu(_conv4(q_ref[0], cw_ref[0, 0], is_ctx))) * (DK_B ** -0.5)
    kn_ref[0] = l2n(_silu(_conv4(k_ref[0], cw_ref[1, 0], is_ctx)))
    vn_ref[0] = _silu(_conv4(v_ref[0], cw_ref[2, 0], is_ctx))

    @pl.when(pl.program_id(1) == 0)
    def _():
        ba = ba_ref[0]
        lane = lax.broadcasted_iota(jnp.int32, ba.shape, 1)
        gate_ref[...] = jnp.where(lane < 2 * H_B, _sigmoid(ba), nega_ref[...] * jax.nn.softplus(ba + dtb_ref[...]))


def _delta_prep(z, conv_qkv, nega, dtb, *, gc, nblk):
    rows = z.shape[1]
    slab = lambda off: pl.BlockSpec((1, RB, LANES), lambda i, h: (off + h, i, 0))
    out = jax.ShapeDtypeStruct((H_B, rows, LANES), f32)
    return pl.pallas_call(
        functools.partial(_delta_prep_kernel, gc=gc),
        grid=(nblk, H_B),
        in_specs=[
            slab(E_Q), slab(E_K), slab(E_V),
            pl.BlockSpec((1, RB, LANES), lambda i, h: (E_BA, i, 0)),
            pl.BlockSpec((3, 1, 4, LANES), lambda i, h: (0, h, 0, 0)),
            pl.BlockSpec((1, LANES), lambda i, h: (0, 0)),
            pl.BlockSpec((1, LANES), lambda i, h: (0, 0)),
        ],
        out_specs=[
            pl.BlockSpec((1, RB, LANES), lambda i, h: (h, i, 0)),
            pl.BlockSpec((1, RB, LANES), lambda i, h: (h, i, 0)),
            pl.BlockSpec((1, RB, LANES), lambda i, h: (h, i, 0)),
            pl.BlockSpec((RB, LANES), lambda i, h: (i, 0)),
        ],
        out_shape=[out, out, out, jax.ShapeDtypeStruct((rows, LANES), f32)],
        compiler_params=_cparams(("parallel", "arbitrary")),
        name="delta_prep",
    )(z, z, z, z, conv_qkv, nega, dtb)


def _delta_kernel(qs_ref, ks_ref, vs_ref, gate_ref, s0_ref, o_ref, fin_ref,
                  bs_ref, la_ref, gs_ref, st_ref, *, gc, rev, d):
    h = pl.program_id(0)
    s = pl.program_id(1)
    is_ctx = s < gc

    @pl.when(is_ctx)
    def _():
        st_ref[...] = jnp.zeros_like(st_ref)

    @pl.when(s == gc)
    def _():
        st_ref[...] = s0_ref[...].reshape(st_ref.shape)

    gates = gate_ref[...]
    for j in range(DELTA_HEADS):
        head = DELTA_HEADS * h + j
        bs_ref[j] = _pick_lane(gates, d * H_B + head)
        la_ref[j] = _pick_lane(gates, 2 * H_B + d * H_B + head)
        _tile_cumsum(la_ref.at[j], gs_ref.at[j], rev)

    ii = lax.broadcasted_iota(jnp.int32, (CHUNK, CHUNK), 0)
    jj = lax.broadcasted_iota(jnp.int32, (CHUNK, CHUNK), 1)
    incl = (jj >= ii) if rev else (jj <= ii)
    strict = (jj > ii) if rev else (jj < ii)
    eye = (ii == jj).astype(f32)
    last = 0 if rev else CHUNK - 1
    diag8 = (ii // 8) == (jj // 8)
    merge_masks = [((ii // (2 * sz)) == (jj // (2 * sz))) & ((ii // sz) != (jj // sz)) for sz in (8, 16, 32)]

    def chunk_body(ci, carry):
        c = (TB // CHUNK - 1 - ci) if rev else ci
        units = range(DELTA_HEADS * SUBLANES)
        hd = [u // SUBLANES for u in units]
        rows = [pl.ds(c * CROWS + u % SUBLANES, CHUNK, stride=SUBLANES) for u in units]
        q = [qs_ref[j, r, :] for j, r in zip(hd, rows)]
        k = [ks_ref[j, r, :] for j, r in zip(hd, rows)]
        v = [vs_ref[j, r, :] for j, r in zip(hd, rows)]
        beta = [bs_ref[j, r, :] for j, r in zip(hd, rows)]
        g = [gs_ref[j, r, :] for j, r in zip(hd, rows)]
        kb = [k[b] * beta[b] for b in units]
        dec_incl, low = [], []
        for b in units:
            gsq = g[b][:, :CHUNK]
            g_row = jnp.sum(gsq * eye, axis=0, keepdims=True)
            diff = gsq - g_row
            dec = jnp.where(incl, jnp.exp(jnp.where(incl, diff, 0.0)), 0.0)
            dec_incl.append(dec)
            low.append(_mm_nt(kb[b], k[b]) * jnp.where(strict, dec, 0.0))
        ld = [jnp.where(diag8, low[b], 0.0) for b in units]
        ls = [_split(ld[b]) for b in units]
        d2 = [_mm3(ls[b], ls[b]) for b in units]
        d2s = [_split(d2[b]) for b in units]
        t_inv = [eye - ld[b] for b in units]
        ts = [_split(t_inv[b]) for b in units]
        prod, d4 = [], []
        for b in units:
            prod.append(_mm3(ts[b], d2s[b]))
            d4.append(_mm3(d2s[b], d2s[b]))
        t_inv = [t_inv[b] + prod[b] for b in units]
        t_inv = [t_inv[b] + _mm3(_split(t_inv[b]), _split(d4[b])) for b in units]
        for lvl in merge_masks:
            ts = [_split(t_inv[b]) for b in units]
            tl = [_mm3(ts[b], _split(jnp.where(lvl, low[b], 0.0))) for b in units]
            t_inv = [t_inv[b] - _mm3(_split(tl[b]), ts[b]) for b in units]
        eg = [jnp.exp(g[b]) for b in units]
        ts = [_split(t_inv[b]) for b in units]
        uw = [_mm3(ts[b], _split(jnp.concatenate([v[b] * beta[b], kb[b] * eg[b]], axis=1))) for b in units]
        a_qk = [_mm_nt(q[b], k[b]) * dec_incl[b] for b in units]
        g_last = [g[b][last:last + 1, :] for b in units]
        st = [st_ref[b] for b in units]
        v_new = [uw[b][:, :DV_B] - _mm(uw[b][:, DV_B:], st[b]) for b in units]
        o = [_mm(q[b] * eg[b], st[b]) + _mm(a_qk[b], v_new[b]) for b in units]
        for b in units:
            k_g = k[b] * jnp.exp(g_last[b] - g[b])
            st_ref[b] = st[b] * jnp.exp(g_last[b]) + _mm_tn(k_g, v_new[b])
            o_ref[hd[b], rows[b], :] = o[b]
        return carry

    lax.fori_loop(0, TB // CHUNK, chunk_body, 0)

    @pl.when(is_ctx)
    def _():
        fin_ref[0] = st_ref[...].reshape(fin_ref.shape[1:])


DELTA_HEADS = 2


def _delta(qn, kn, vn, gates, s0, *, d, gc, nd, rev):
    rows = qn.shape[1]
    nblk = gc + nd
    dh = DELTA_HEADS
    blk = lambda h, s: _step_blk(s, gc, nd, rev)
    slab = pl.BlockSpec((dh, RB, LANES), lambda h, s: (h, blk(h, s), 0))
    scr = pltpu.VMEM((dh, RB, LANES), f32)
    return pl.pallas_call(
        functools.partial(_delta_kernel, gc=gc, rev=rev, d=d),
        grid=(H_B // dh, nblk),
        in_specs=[
            slab, slab, slab,
            pl.BlockSpec((RB, LANES), lambda h, s: (blk(h, s), 0)),
            pl.BlockSpec((dh, SUBLANES, DK_B, DV_B), lambda h, s: (h, 0, 0, 0)),
        ],
        out_specs=[
            pl.BlockSpec((dh, RB, LANES), lambda h, s: (h, blk(h, s), 0)),
            pl.BlockSpec((1, dh, SUBLANES, DK_B, DV_B), lambda h, s: (jnp.minimum(s, gc - 1), h, 0, 0, 0)),
        ],
        out_shape=[
            jax.ShapeDtypeStruct((H_B, rows, LANES), f32),
            jax.ShapeDtypeStruct((gc, H_B, SUBLANES, DK_B, DV_B), f32),
        ],
        scratch_shapes=[scr, scr, scr, pltpu.VMEM((dh * SUBLANES, DK_B, DV_B), f32)],
        compiler_params=_cparams(("parallel", "arbitrary")),
        name="delta_bwd" if rev else "delta_fwd",
    )(qn, kn, vn, gates, s0)


def _s5_kernel(u_ref, bt_ref, ct_ref, lam_ref, s0_ref, y_ref, fin_ref, sbuf_ref, carry_ref, *, gc, rev):
    s = pl.program_id(1)
    half = sbuf_ref.shape[1] // 2

    @pl.when(s < gc)
    def _():
        carry_ref[...] = jnp.zeros_like(carry_ref)

    @pl.when(s == gc)
    def _():
        carry_ref[...] = s0_ref[0]

    nsl = u_ref.shape[0]
    lr = jnp.broadcast_to(lam_ref[0, 0:1, :], (SUBLANES, half))
    li = jnp.broadcast_to(lam_ref[0, 1:2, :], (SUBLANES, half))
    nq = TB // S5_TQ
    qrows = S5_TQ * SUBLANES
    order = list(range(nq - 1, -1, -1)) if rev else list(range(nq))

    def project_in(qb):
        rs = slice(qb * qrows, (qb + 1) * qrows)
        sbuf_ref[rs, :] = _mm(jnp.concatenate([u_ref[j, rs, :] for j in range(nsl)], axis=1), bt_ref[0])

    def project_out(qb):
        rs = slice(qb * qrows, (qb + 1) * qrows)
        y = _mm(sbuf_ref[rs, :], ct_ref[0])
        for j in range(nsl):
            y_ref[j, rs, :] = y[:, j * LANES:(j + 1) * LANES]

    hr, hi = carry_ref[:, :half], carry_ref[:, half:]
    project_in(order[0])
    for n, qb in enumerate(order):
        if n + 1 < nq:
            project_in(order[n + 1])
        for i in range(S5_TQ):
            t = qb * S5_TQ + ((S5_TQ - 1 - i) if rev else i)
            rs = slice(t * SUBLANES, (t + 1) * SUBLANES)
            hr, hi = (lr * hr - li * hi + sbuf_ref[rs, :half], lr * hi + li * hr + sbuf_ref[rs, half:])
            sbuf_ref[rs, :half] = hr
            sbuf_ref[rs, half:] = hi
        project_out(qb)
    carry_ref[:, :half] = hr
    carry_ref[:, half:] = hi

    @pl.when(s < gc)
    def _():
        fin_ref[0, 0] = carry_ref[...]


S5_TQ = 32
S5_COLS = LANES
S5_GPC = S5_COLS // S5_GW
S5_SW = 2 * S5_GPC * S5_N


def _s5(z, bt, ct, lam, s0, *, gc, nd, rev):
    rows = z.shape[1]
    nblk = gc + nd
    ngt = D_C // S5_COLS
    nsl = S5_COLS // LANES
    sw = S5_SW
    blk = lambda g, s: _step_blk(s, gc, nd, rev)
    return pl.pallas_call(
        functools.partial(_s5_kernel, gc=gc, rev=rev),
        grid=(ngt, nblk),
        in_specs=[
            pl.BlockSpec((nsl, RB, LANES), lambda g, s: (O_U // nsl + g, blk(g, s), 0)),
            pl.BlockSpec((1, S5_COLS, sw), lambda g, s: (g, 0, 0)),
            pl.BlockSpec((1, sw, S5_COLS), lambda g, s: (g, 0, 0)),
            pl.BlockSpec((1, 2, sw // 2), lambda g, s: (g, 0, 0)),
            pl.BlockSpec((1, SUBLANES, sw), lambda g, s: (g, 0, 0)),
        ],
        out_specs=[
            pl.BlockSpec((nsl, RB, LANES), lambda g, s: (g, blk(g, s), 0)),
            pl.BlockSpec((1, 1, SUBLANES, sw), lambda g, s: (jnp.minimum(s, gc - 1), g, 0, 0)),
        ],
        out_shape=[
            jax.ShapeDtypeStruct((D_C // LANES, rows, LANES), f32),
            jax.ShapeDtypeStruct((gc, ngt, SUBLANES, sw), f32),
        ],
        scratch_shapes=[pltpu.VMEM((RB, sw), f32), pltpu.VMEM((SUBLANES, sw), f32)],
        compiler_params=_cparams(("parallel", "arbitrary")),
        name="s5_bwd" if rev else "s5_fwd",
    )(z, bt, ct, lam, s0)


def _s5_out_kernel(yf_ref, yb_ref, u_ref, d_ref, w_ref, b_ref, o_ref):
    ns = yf_ref.shape[0]
    yc = jnp.concatenate([yf_ref[s] + yb_ref[s] + d_ref[s] * u_ref[s] for s in range(ns)], axis=1)
    zc = jax.nn.gelu(yc)
    gate = _sigmoid(jnp.dot(zc.astype(bf16), w_ref[...], preferred_element_type=f32) + b_ref[...])
    o_ref[...] = (zc * gate).astype(o_ref.dtype)


def _s5_out(yf, yb, z, dskip, w_glu, b_glu, *, tm):
    ns, rows, _ = yf.shape
    return pl.pallas_call(
        _s5_out_kernel,
        grid=(rows // tm,),
        in_specs=[
            pl.BlockSpec((ns, tm, LANES), lambda i: (0, i, 0)),
            pl.BlockSpec((ns, tm, LANES), lambda i: (0, i, 0)),
            pl.BlockSpec((ns, tm, LANES), lambda i: (O_U // ns, i, 0)),
            pl.BlockSpec((ns, 1, LANES), lambda i: (0, 0, 0)),
            pl.BlockSpec((D_C, D_C), lambda i: (0, 0)),
            pl.BlockSpec((1, D_C), lambda i: (0, 0)),
        ],
        out_specs=pl.BlockSpec((tm, D_C), lambda i: (i, 0)),
        out_shape=jax.ShapeDtypeStruct((rows, D_C), bf16),
        compiler_params=_cparams(("parallel",)),
        name="s5_out",
    )(yf, yb, z, dskip, w_glu, b_glu)


def _gla_kernel(q_ref, k_ref, v_ref, glr_ref, wg_ref, bg_ref, s0_ref, o_ref, fin_ref,
                la_ref, gs_ref, st_ref, *, gc, rev):
    s = pl.program_id(1)
    is_ctx = s < gc

    @pl.when(is_ctx)
    def _():
        st_ref[...] = jnp.zeros_like(st_ref)

    @pl.when(s == gc)
    def _():
        st_ref[...] = s0_ref[0]

    la_ref[...] = jax.nn.log_sigmoid(_mm(glr_ref[0], wg_ref[0]) + bg_ref[0]) * (1.0 / GLA_TAU)
    _tile_cumsum(la_ref, gs_ref, rev)

    ii = lax.broadcasted_iota(jnp.int32, (SUB, CHUNK), 0)
    jj = lax.broadcasted_iota(jnp.int32, (SUB, CHUNK), 1)
    last = 0 if rev else CHUNK - 1

    def chunk_body(ci, carry):
        c = (TB // CHUNK - 1 - ci) if rev else ci
        units = range(SUBLANES)
        rows = [pl.ds(c * CROWS + b, CHUNK, stride=SUBLANES) for b in units]
        q = [q_ref[0, r, :] * (DK_D ** -0.5) for r in rows]
        k = [k_ref[0, r, :] for r in rows]
        v = [jnp.concatenate([v_ref[0, r, :], v_ref[1, r, :]], axis=1) for r in rows]
        g = [gs_ref[r, :] for r in rows]
        a_rows = [[] for _ in units]
        for blk in range(CHUNK // SUB):
            lo = blk * SUB
            first = lo + SUB - 1 if rev else lo
            causal = (jj >= ii + lo) if rev else (jj <= ii + lo)
            for b in units:
                g_first = g[b][first:first + 1, :]
                qt = q[b][lo:lo + SUB, :] * jnp.exp(g[b][lo:lo + SUB, :] - g_first)
                kt = k[b] * jnp.exp(g_first - g[b])
                a_rows[b].append(jnp.where(causal, _mm_nt(qt, kt), 0.0))
        a_qk = [jnp.concatenate(a_rows[b], axis=0) for b in units]
        g_last = [g[b][last:last + 1, :] for b in units]
        st = [st_ref[b] for b in units]
        o = [_mm_nt(q[b] * jnp.exp(g[b]), st[b]) + _mm(a_qk[b], v[b]) for b in units]
        for b in units:
            k_g = k[b] * jnp.exp(g_last[b] - g[b])
            st_ref[b] = st[b] * jnp.exp(g_last[b]) + _mm_tn(v[b], k_g)
            o_ref[0, rows[b], :] = o[b][:, :LANES]
            o_ref[1, rows[b], :] = o[b][:, LANES:]
        return carry

    lax.fori_loop(0, TB // CHUNK, chunk_body, 0)

    @pl.when(is_ctx)
    def _():
        fin_ref[0, 0] = st_ref[...]


def _gla(z, wg, bg, s0, *, gc, nd, rev):
    rows = z.shape[1]
    nblk = gc + nd
    nv = DV_D // LANES
    blk = lambda h, s: _step_blk(s, gc, nd, rev)
    scr = pltpu.VMEM((RB, LANES), f32)
    return pl.pallas_call(
        functools.partial(_gla_kernel, gc=gc, rev=rev),
        grid=(H_D, nblk),
        in_specs=[
            pl.BlockSpec((1, RB, LANES), lambda h, s: (O_Q + h, blk(h, s), 0)),
            pl.BlockSpec((1, RB, LANES), lambda h, s: (O_K + h, blk(h, s), 0)),
            pl.BlockSpec((nv, RB, LANES), lambda h, s: (O_V // nv + h, blk(h, s), 0)),
            pl.BlockSpec((1, RB, LANES), lambda h, s: (O_GLR, blk(h, s), 0)),
            pl.BlockSpec((1, LANES, LANES), lambda h, s: (h, 0, 0)),
            pl.BlockSpec((1, 1, LANES), lambda h, s: (h, 0, 0)),
            pl.BlockSpec((1, SUBLANES, DV_D, DK_D), lambda h, s: (h, 0, 0, 0)),
        ],
        out_specs=[
            pl.BlockSpec((nv, RB, LANES), lambda h, s: (h, blk(h, s), 0)),
            pl.BlockSpec((1, 1, SUBLANES, DV_D, DK_D), lambda h, s: (jnp.minimum(s, gc - 1), h, 0, 0, 0)),
        ],
        out_shape=[
            jax.ShapeDtypeStruct((H_D * nv, rows, LANES), f32),
            jax.ShapeDtypeStruct((gc, H_D, SUBLANES, DV_D, DK_D), f32),
        ],
        scratch_shapes=[scr, scr, pltpu.VMEM((SUBLANES, DV_D, DK_D), f32)],
        compiler_params=_cparams(("parallel", "arbitrary")),
        name="gla_bwd" if rev else "gla_fwd",
    )(z, z, z, z, wg, bg, s0)


TL = 64


def _rows_in_kernel(xp_ref, xs_ref, o_ref, *, nctx):
    def emit(ref):
        x = ref[0]
        o_ref[...] = jnp.swapaxes(x, 0, 1).reshape(TL * SUBLANES, x.shape[2])

    pl.when(pl.program_id(0) < nctx)(lambda: emit(xp_ref))
    pl.when(pl.program_id(0) >= nctx)(lambda: emit(xs_ref))


def _rows_in(x_prompt, x_sample):
    bp, tp, d = x_prompt.shape
    bs, ts, _ = x_sample.shape
    xp = x_prompt.reshape(bp // SUBLANES, SUBLANES, tp, d)
    xs = x_sample.reshape(bs // SUBLANES, SUBLANES, ts, d)
    per = tp // TL
    nctx = (bp // SUBLANES) * per
    nblk = nctx + ts // TL

    def ctx_idx(i):
        j = jnp.minimum(i, nctx - 1)
        return (j // per, 0, j % per, 0)

    return pl.pallas_call(
        functools.partial(_rows_in_kernel, nctx=nctx),
        grid=(nblk,),
        in_specs=[
            pl.BlockSpec((1, SUBLANES, TL, d), ctx_idx),
            pl.BlockSpec((1, SUBLANES, TL, d), lambda i: (0, 0, jnp.maximum(i - nctx, 0), 0)),
        ],
        out_specs=pl.BlockSpec((TL * SUBLANES, d), lambda i: (i, 0)),
        out_shape=jax.ShapeDtypeStruct((bp * tp + bs * ts, d), x_prompt.dtype),
        compiler_params=_cparams(("arbitrary",)),
        name="rows_in",
    )(xp, xs)


def _rows_out_kernel(x_ref, yp_ref, ys_ref, *, nctx):
    x = x_ref[...]
    y = jnp.swapaxes(x.reshape(TL, SUBLANES, x.shape[1]), 0, 1)

    @pl.when(pl.program_id(0) < nctx)
    def _():
        yp_ref[0] = y

    @pl.when(pl.program_id(0) >= nctx)
    def _():
        ys_ref[0] = y


def _rows_out(x, bp, tp, bs, ts):
    d = x.shape[1]
    per = tp // TL
    nctx = (bp // SUBLANES) * per
    nblk = nctx + ts // TL

    def ctx_idx(i):
        j = jnp.minimum(i, nctx - 1)
        return (j // per, 0, j % per, 0)

    yp, ys = pl.pallas_call(
        functools.partial(_rows_out_kernel, nctx=nctx),
        grid=(nblk,),
        in_specs=[pl.BlockSpec((TL * SUBLANES, d), lambda i: (i, 0))],
        out_specs=[
            pl.BlockSpec((1, SUBLANES, TL, d), ctx_idx),
            pl.BlockSpec((1, SUBLANES, TL, d), lambda i: (0, 0, jnp.maximum(i - nctx, 0), 0)),
        ],
        out_shape=[
            jax.ShapeDtypeStruct((bp // SUBLANES, SUBLANES, tp, d), x.dtype),
            jax.ShapeDtypeStruct((bs // SUBLANES, SUBLANES, ts, d), x.dtype),
        ],
        compiler_params=_cparams(("arbitrary",)),
        name="rows_out",
    )(x)
    return yp.reshape(bp, tp, d), ys.reshape(bs, ts, d)


def _pad_cols(w, n):
    return jnp.pad(w, ((0, 0), (0, n - w.shape[1])))


def _unit_states(fin, b):
    fin = jnp.moveaxis(fin, b, 1)
    return fin.reshape((fin.shape[0] * SUBLANES,) + fin.shape[2:])


def kernel(x_prompt, x_sample, state_lru, state_delta, state_s5_re, state_s5_im, state_gla, c, c_ctx, w_ada, b_ada, norm_g, w_up, w_down, w_in_e, w_out_e, conv_a, lru_wa, lru_ba, lru_wx, lru_bx, lru_lambda, conv_qkv, dn_a_log, dn_dt_bias, dn_norm_g, w_in_o, w_out_o, s5_lam_re, s5_lam_im, s5_log_step, s5_b_re, s5_b_im, s5_c_re, s5_c_im, s5_d, s5_w_glu, s5_b_glu, gla_wg2, gla_bg, gla_norm_g):
    bp, tp, d = x_prompt.shape
    bs, ts, _ = x_sample.shape
    assert tp == TB and bp % SUBLANES == 0 and bs == SUBLANES and ts % TB == 0 and d == D_MODEL
    depth = w_ada.shape[0]
    gc = bp // SUBLANES
    nd = ts // TB
    nblk = gc + nd
    rows_ctx = bp * tp
    seq = dict(gc=gc, nd=nd)

    x = _rows_in(x_prompt, x_sample)

    cvec = jnp.concatenate([c, jnp.broadcast_to(c_ctx[None], (SUBLANES, d))], axis=0)
    mod = _ada(cvec, w_ada, b_ada)
    mod = jnp.stack([mod[:, SUBLANES:], mod[:, :SUBLANES]], axis=1)
    mods = [[mod[l, :, :, i * d:(i + 1) * d] for i in range(6)] for l in range(depth)]

    fin_lru, fin_delta, fin_re, fin_im, fin_gla = [], [], [], [], []
    for l in range(depth):
        sh1, sc1, gt1, sh2, sc2, gt2 = mods[l]
        if l % 2 == 0:
            e = l // 2
            w_in = _pad_cols(w_in_e[e], E_SLABS * LANES).astype(bf16)
            z = _nmm(x, norm_g[l, 0], sh1, sc1, w_in, rows_ctx=rows_ctx, tm=256, tn=10 * LANES, slab=True)
            ns = D_A // LANES
            wg = jnp.stack([lru_wa[e], lru_wx[e]], axis=1).reshape(2, 2, ns, 2, LRU_BW, LRU_BW)
            wbd = jnp.zeros((2, 2, ns, LANES, LANES), f32)
            wbd = wbd.at[:, :, :, :LRU_BW, :LRU_BW].set(wg[:, :, :, 0]).at[:, :, :, LRU_BW:, LRU_BW:].set(wg[:, :, :, 1])
            bias = jnp.stack([lru_ba[e], lru_bx[e]], axis=1).reshape(2, 2, ns, 1, LANES)
            sp = jax.nn.softplus(-lru_lambda[e]).reshape(2, ns, 1, LANES)
            cwa = conv_a[e].reshape(4, ns, LANES).transpose(1, 0, 2)
            a, b = _lru_gates(z, cwa, wbd.astype(bf16), bias, sp, gc=gc, nblk=nblk)
            h_f, f_f = _lru_scan(a, b, state_lru[:, e, 0], d=0, rev=False, **seq)
            h_b, f_b = _lru_scan(a, b, state_lru[:, e, 1], d=1, rev=True, **seq)
            fin_lru.append(jnp.stack([_unit_states(f_f, 1), _unit_states(f_b, 1)], axis=1))
            cq = conv_qkv[e].reshape(4, 3, H_B, LANES).transpose(1, 2, 0, 3)
            lane_pad = lambda v: jnp.pad(v.reshape(1, 2 * H_B), ((0, 0), (2 * H_B, LANES - 4 * H_B)))
            nega = lane_pad(-jnp.exp(dn_a_log[e]))
            dtb = lane_pad(dn_dt_bias[e])
            s0 = state_delta[:, e].transpose(1, 2, 0, 3, 4)
            qn, kn, vn, gates = _delta_prep(z, cq, nega, dtb, gc=gc, nblk=nblk)
            o_f, s_f = _delta(qn, kn, vn, gates, s0[0], d=0, rev=False, **seq)
            o_b, s_b = _delta(qn, kn, vn, gates, s0[1], d=1, rev=True, **seq)
            fin_delta.append(jnp.stack([_unit_states(s_f, 2), _unit_states(s_b, 2)], axis=1))
            x = _mixout_even(h_f, h_b, o_f, o_b, z, dn_norm_g[e], w_out_e[e].astype(bf16), x, norm_g[l, 1], gt1,
                             rows_ctx=rows_ctx, tm=512)
        else:
            o = l // 2
            w_in = _pad_cols(w_in_o[o], O_SLABS * LANES).astype(bf16)
            z = _nmm(x, norm_g[l, 0], sh1, sc1, w_in, rows_ctx=rows_ctx, tm=256, tn=11 * LANES, slab=True)
            ngt = D_C // S5_COLS
            gps = S5_GPC
            lam = lax.complex(s5_lam_re[o], s5_lam_im[o])
            lam_bar = jnp.exp(lam * jnp.exp(s5_log_step[o])[..., None])
            b_bar = ((lam_bar - 1.0) / lam)[..., None] * lax.complex(s5_b_re[o], s5_b_im[o])
            eye_g = jnp.eye(gps, dtype=f32)

            def bmat(p):
                p = p.reshape(2, ngt, gps, S5_N, S5_GW)
                return jnp.einsum('dtgnw,gh->dtgwhn', p, eye_g).reshape(2, ngt, S5_COLS, gps * S5_N)

            def cmat(p):
                p = p.reshape(2, ngt, gps, S5_GW, S5_N)
                return jnp.einsum('dtgwn,gh->dtgnhw', p, eye_g).reshape(2, ngt, gps * S5_N, S5_COLS)

            bt = jnp.concatenate([bmat(b_bar.real), bmat(b_bar.imag)], axis=-1).astype(bf16)
            ct = jnp.concatenate([cmat(s5_c_re[o]), -cmat(s5_c_im[o])], axis=-2).astype(bf16)
            lamv = jnp.stack([lam_bar.real, lam_bar.imag], axis=2).reshape(2, ngt, gps, 2, S5_N)
            lamv = lamv.transpose(0, 1, 3, 2, 4).reshape(2, ngt, 2, gps * S5_N)

            def s5_state(sre, sim):
                f = lambda p: p.reshape(SUBLANES, ngt, gps * S5_N).transpose(1, 0, 2)
                return jnp.concatenate([f(sre), f(sim)], axis=-1)

            y_f, c_f = _s5(z, bt[0], ct[0], lamv[0], s5_state(state_s5_re[:, o, 0], state_s5_im[:, o, 0]), rev=False, **seq)
            y_bk, c_b = _s5(z, bt[1], ct[1], lamv[1], s5_state(state_s5_re[:, o, 1], state_s5_im[:, o, 1]), rev=True, **seq)
            y_c = _s5_out(y_f, y_bk, z, s5_d[o].reshape(D_C // LANES, 1, LANES), s5_w_glu[o].astype(bf16),
                          s5_b_glu[o].reshape(1, D_C), tm=512)

            def s5_fin(cf):
                cf = cf.transpose(0, 2, 1, 3).reshape(gc * SUBLANES, ngt, 2, gps, S5_N)
                return (cf[:, :, 0].reshape(gc * SUBLANES, S5_G, S5_N), cf[:, :, 1].reshape(gc * SUBLANES, S5_G, S5_N))

            (rf, imf), (rb, imb) = s5_fin(c_f), s5_fin(c_b)
            fin_re.append(jnp.stack([rf, rb], axis=1))
            fin_im.append(jnp.stack([imf, imb], axis=1))
            wgp = jnp.zeros((2, H_D, LANES, LANES), f32)
            wg2 = gla_wg2[o].reshape(2, GLA_RANK, H_D, DK_D).transpose(0, 2, 1, 3)
            wgp = wgp.at[0, :, :GLA_RANK].set(wg2[0]).at[1, :, GLA_RANK:2 * GLA_RANK].set(wg2[1]).astype(bf16)
            bgp = gla_bg[o].reshape(2, H_D, 1, DK_D)
            g0 = state_gla[:, o].transpose(1, 2, 0, 4, 3)
            o_f, g_f = _gla(z, wgp[0], bgp[0], g0[0], rev=False, **seq)
            o_b, g_b = _gla(z, wgp[1], bgp[1], g0[1], rev=True, **seq)
            gfin = lambda gf: jnp.swapaxes(_unit_states(gf, 2), -1, -2)
            fin_gla.append(jnp.stack([gfin(g_f), gfin(g_b)], axis=1))
            x = _mixout_odd(y_c, o_f, o_b, z, gla_norm_g[o], w_out_o[o].astype(bf16), x, norm_g[l, 1], gt1,
                            rows_ctx=rows_ctx, tm=512)

        hmid = _nmm(x, norm_g[l, 2], sh2, sc2, w_up[l].astype(bf16), rows_ctx=rows_ctx, tm=256, tn=2048,
                    relu2=True, out_dtype=bf16)
        x = _mmres(hmid, w_down[l].astype(bf16), x, norm_g[l, 3], gt2, rows_ctx=rows_ctx, tm=256)

    y_prompt, y_sample = _rows_out(x, bp, tp, bs, ts)
    return (y_prompt, y_sample, jnp.stack(fin_lru, axis=1), jnp.stack(fin_delta, axis=1),
            jnp.stack(fin_re, axis=1), jnp.stack(fin_im, axis=1), jnp.stack(fin_gla, axis=1))
```
